```python
import math
import jax, jax.numpy as jnp
from jax import lax
import numpy as np

D_MODEL = 1024
BATCH = 32
SEQ = 256
DEPTH = 2
DEC_BATCH = 4
DEC_SEQ = 2048
PAST_LEN = 512

GRID_W = 64
FF_HIDDEN = 2816
N_MOD = 9
HEAD_DIM = 64
FN_WIDTH = D_MODEL // 4
FN_GROUP_DIM = HEAD_DIM
FN_GROUPS = FN_WIDTH // FN_GROUP_DIM
SSD_INNER = 3 * D_MODEL // 8
SSD_HEAD_DIM = HEAD_DIM
SSD_HEADS = SSD_INNER // SSD_HEAD_DIM
SSD_STATE = 64
SSD_NGROUPS = 2
SSD_CONV = 5
SSD_CHUNK = 128
SSD_CONV_DIM = SSD_INNER + 2 * SSD_NGROUPS * SSD_STATE
SSD_IN = SSD_INNER + SSD_CONV_DIM + 2 * SSD_HEADS
NA_WIDTH = 3 * D_MODEL // 8
NA_HEAD_DIM = HEAD_DIM
NA_HEADS = NA_WIDTH // NA_HEAD_DIM
NA_WIN_ROWS = 8
NA_WIN_COLS = 16
Q_BLOCK = 128
MIX_WIDTH = FN_WIDTH + SSD_INNER + NA_WIDTH
IN_COLS = FN_WIDTH + SSD_IN + 3 * NA_WIDTH
ROPE_BASE = 10000.0
EPS = 1e-6

kernel_name = "hybrid_fourier_ssd_natten_prefix_diffusion_step"


def rmsnorm(x, g):
    xf = x.astype(jnp.float32)
    y = xf * lax.rsqrt(jnp.mean(xf * xf, axis=-1, keepdims=True) + EPS)
    return (y * g.astype(jnp.float32)).astype(x.dtype)


def modulated_norm(x, g, shift, scale):
    return rmsnorm(x, g) * (1 + scale) + shift


def swiglu(h, w13, w2):
    gate, up = jnp.split(h @ w13, 2, axis=-1)
    return (jax.nn.silu(gate) * up) @ w2


def half_ffn(x, shift, scale, gate, g_pre, g_post, w13, w2):
    h = modulated_norm(x, g_pre, shift, scale)
    return x + 0.5 * gate * rmsnorm(swiglu(h, w13, w2), g_post)


def fourier_mix(u):
    b, L, _ = u.shape
    uf = u.astype(jnp.float32).reshape(b, L, FN_GROUPS, FN_GROUP_DIM)
    y = jnp.fft.fftn(uf, axes=(1, 3), norm="ortho").real
    return y.reshape(b, L, FN_WIDTH).astype(u.dtype)


def centred_depthwise_conv(x, w, bias):
    ch = x.shape[-1]
    y = lax.conv_general_dilated(x, w.astype(x.dtype)[:, None, :], window_strides=(1,),
                                 padding=[(SSD_CONV // 2, SSD_CONV // 2)],
                                 dimension_numbers=("NWC", "WIO", "NWC"),
                                 feature_group_count=ch)
    return y + bias.astype(x.dtype)


def axial_rope(x):
    L, n = x.shape[1], x.shape[-1]
    half = n // 2
    quarter = half // 2
    inv = ROPE_BASE ** (-jnp.arange(quarter, dtype=jnp.float32) / quarter)
    t = jnp.arange(L)
    rows = (t // GRID_W).astype(jnp.float32)
    cols = (t % GRID_W).astype(jnp.float32)

    def rot(v, pos):
        ang = pos[:, None] * inv[None, :]
        cos = jnp.cos(ang)[None, :, None, :]
        sin = jnp.sin(ang)[None, :, None, :]
        v1, v2 = v[..., :quarter], v[..., quarter:]
        return jnp.concatenate([v1 * cos - v2 * sin, v1 * sin + v2 * cos], axis=-1)

    xf = x.astype(jnp.float32)
    return jnp.concatenate([rot(xf[..., :half], rows), rot(xf[..., half:], cols)], axis=-1).astype(x.dtype)


def ssd_scan(x, dt, a, b_mat, c_mat, init):
    bsz, L, H, P = x.shape
    nc = L // SSD_CHUNK

    def chunk(t):
        return t.reshape((bsz, nc, SSD_CHUNK) + t.shape[2:])

    xdt = chunk(x * dt[..., None])
    cs = jnp.cumsum(chunk(dt * a), axis=2)
    bc, cc = chunk(b_mat), chunk(c_mat)
    lower = jnp.tril(jnp.ones((SSD_CHUNK, SSD_CHUNK), dtype=bool))
    seg = cs[:, :, :, None, :] - cs[:, :, None, :, :]
    decay = jnp.exp(jnp.where(lower[None, None, :, :, None], seg, -jnp.inf))
    scores = jnp.einsum("bclhn,bcshn->bclsh", cc, bc) * decay
    y_diag = jnp.einsum("bclsh,bcshp->bclhp", scores, xdt)
    decay_to_end = jnp.exp(cs[:, :, -1:, :] - cs)
    states = jnp.einsum("bclhn,bclh,bclhp->bchpn", bc, decay_to_end, xdt)
    chunk_decay = jnp.exp(cs[:, :, -1, :])

    def step(s, inp):
        st, dec = inp
        return s * dec[:, :, None, None] + st, s

    final, prev = lax.scan(step, init, (jnp.moveaxis(states, 1, 0), jnp.moveaxis(chunk_decay, 1, 0)))
    prev = jnp.moveaxis(prev, 0, 1)
    y_off = jnp.einsum("bclhn,bchpn,bclh->bclhp", cc, prev, jnp.exp(cs))
    return (y_diag + y_off).reshape(bsz, L, H, P), final


def ssd_mixer(u, p, init_f, init_b, use_rope):
    b, L, _ = u.shape
    z = u[..., :SSD_INNER]
    xbc = jax.nn.silu(centred_depthwise_conv(u[..., SSD_INNER:SSD_INNER + SSD_CONV_DIM], p["conv_w"], p["conv_b"]))
    dt_raw = u[..., SSD_INNER + SSD_CONV_DIM:]
    gn = SSD_NGROUPS * SSD_STATE
    xs = xbc[..., :SSD_INNER].astype(jnp.float32).reshape(b, L, SSD_HEADS, SSD_HEAD_DIM)
    bm = xbc[..., SSD_INNER:SSD_INNER + gn].reshape(b, L, SSD_NGROUPS, SSD_STATE)
    cm = xbc[..., SSD_INNER + gn:].reshape(b, L, SSD_NGROUPS, SSD_STATE)
    if use_rope:
        bm, cm = axial_rope(bm), axial_rope(cm)
    rep = SSD_HEADS // SSD_NGROUPS
    bh = jnp.repeat(bm.astype(jnp.float32), rep, axis=2)
    ch = jnp.repeat(cm.astype(jnp.float32), rep, axis=2)
    dt = jax.nn.softplus(dt_raw.astype(jnp.float32).reshape(b, L, 2, SSD_HEADS)
                         + p["dt_bias"].astype(jnp.float32))
    a = -jnp.exp(p["a_log"].astype(jnp.float32))
    y_f, s_f = ssd_scan(xs, dt[:, :, 0], a[0], bh, ch, init_f.astype(jnp.float32))
    fl = lambda t: jnp.flip(t, axis=1)
    y_b, s_b = ssd_scan(fl(xs), fl(dt[:, :, 1]), a[1], fl(bh), fl(ch), init_b.astype(jnp.float32))
    y = y_f + fl(y_b) + xs * p["d"].astype(jnp.float32)[:, None]
    y = y.reshape(b, L, SSD_INNER) * jax.nn.silu(z.astype(jnp.float32))
    yg = y.reshape(b, L, SSD_NGROUPS, SSD_INNER // SSD_NGROUPS)
    yg = yg * lax.rsqrt(jnp.mean(yg * yg, axis=-1, keepdims=True) + EPS)
    y = yg.reshape(b, L, SSD_INNER) * p["ssd_norm"].astype(jnp.float32)
    return y.astype(u.dtype), s_f, s_b


def context_attention(q, k, v):
    b, L, H, dh = q.shape
    scale = dh ** -0.5
    qb = jnp.moveaxis(q.reshape(b, L // Q_BLOCK, Q_BLOCK, H, dh), 1, 0)

    def block(q_blk):
        s = jnp.einsum("bqhd,bkhd->bhqk", q_blk, k).astype(jnp.float32) * scale
        pr = jax.nn.softmax(s, axis=-1).astype(v.dtype)
        return jnp.einsum("bhqk,bkhd->bqhd", pr, v)

    out = lax.map(block, qb)
    return jnp.moveaxis(out, 0, 1).reshape(b, L, H * dh)


def neighbourhood_attention(q, k, v, k_ctx, v_ctx, rpb):
    b, L, H, dh = q.shape
    rows = L // GRID_W
    wr = min(NA_WIN_ROWS, rows)
    scale = dh ** -0.5
    qg = q.reshape(b, rows, GRID_W, H, dh)
    kg = k.reshape(b, rows, GRID_W, H, dh)
    vg = v.reshape(b, rows, GRID_W, H, dh)
    cols = jnp.arange(GRID_W)
    cstart = jnp.clip(cols - NA_WIN_COLS // 2, 0, GRID_W - NA_WIN_COLS)
    col_in = (cols[None, :] >= cstart[:, None]) & (cols[None, :] < cstart[:, None] + NA_WIN_COLS)
    dc_idx = jnp.clip(cols[None, :] - cols[:, None] + NA_WIN_COLS - 1, 0, 2 * NA_WIN_COLS - 2)
    rpb32 = rpb.astype(jnp.float32)

    def row_block(r):
        rs = jnp.clip(r - wr // 2, 0, rows - wr)
        q_r = lax.dynamic_index_in_dim(qg, r, axis=1, keepdims=False)
        k_r = lax.dynamic_slice_in_dim(kg, rs, wr, axis=1)
        v_r = lax.dynamic_slice_in_dim(vg, rs, wr, axis=1)
        dr_idx = rs + jnp.arange(wr) - r + NA_WIN_ROWS - 1
        bias = rpb32[:, dr_idx[None, :, None], dc_idx[:, None, :]]
        s_loc = jnp.einsum("bqhd,biwhd->bhqiw", q_r, k_r).astype(jnp.float32) * scale + bias
        s_loc = jnp.where(col_in[:, None, :], s_loc, -jnp.inf)
        s_ctx = jnp.einsum("bqhd,bchd->bhqc", q_r, k_ctx).astype(jnp.float32) * scale
        logits = jnp.concatenate([s_loc.reshape(b, H, GRID_W, wr * GRID_W), s_ctx], axis=-1)
        pr = jax.nn.softmax(logits, axis=-1).astype(v.dtype)
        p_loc = pr[..., :wr * GRID_W].reshape(b, H, GRID_W, wr, GRID_W)
        p_ctx = pr[..., wr * GRID_W:]
        return (jnp.einsum("bhqiw,biwhd->bqhd", p_loc, v_r)
                + jnp.einsum("bhqc,bchd->bqhd", p_ctx, v_ctx))

    out = lax.map(row_block, jnp.arange(rows))
    return jnp.moveaxis(out, 0, 1).reshape(b, L, H * dh)


def split_projection(h, p):
    u = h @ p["w_in"]
    b, L, _ = h.shape
    u_fn = u[..., :FN_WIDTH]
    u_ssd = u[..., FN_WIDTH:FN_WIDTH + SSD_IN]
    q, k, v = [t.reshape(b, L, NA_HEADS, NA_HEAD_DIM)
               for t in jnp.split(u[..., FN_WIDTH + SSD_IN:], 3, axis=-1)]
    return u_fn, u_ssd, q, k, v


def context_mixer(h, p):
    b = h.shape[0]
    u_fn, u_ssd, q, k, v = split_projection(h, p)
    zeros = jnp.zeros((b, SSD_HEADS, SSD_HEAD_DIM, SSD_STATE), jnp.float32)
    y_ssd, s_f, s_b = ssd_mixer(u_ssd, p, zeros, zeros, use_rope=False)
    y = jnp.concatenate([fourier_mix(u_fn), y_ssd, context_attention(q, k, v)], axis=-1) @ p["w_out"]
    return y, (s_f.astype(h.dtype), s_b.astype(h.dtype), k, v)


def latent_mixer(h, p, s_f0, s_b0, k_ctx, v_ctx):
    u_fn, u_ssd, q, k, v = split_projection(h, p)
    y_ssd, _, _ = ssd_mixer(u_ssd, p, s_f0, s_b0, use_rope=True)
    y_na = neighbourhood_attention(q, k, v, k_ctx, v_ctx, p["rpb"])
    y = jnp.concatenate([fourier_mix(u_fn), y_ssd, y_na], axis=-1) @ p["w_out"]
    return y, None


def trunk_layer(x, mods, p, mixer):
    x = half_ffn(x, mods[0], mods[1], mods[2], p["norm_pre"][0], p["norm_post"][0], p["w13"][0], p["w2"][0])
    h = modulated_norm(x, p["norm_pre"][1], mods[3], mods[4])
    y, extra = mixer(h)
    x = x + mods[5] * rmsnorm(y, p["norm_post"][1])
    x = half_ffn(x, mods[6], mods[7], mods[8], p["norm_pre"][2], p["norm_post"][2], p["w13"][1], p["w2"][1])
    return x, extra


def setup_inputs(seed: int = 0) -> dict:
    key = jax.random.key(seed)
    ks = jax.random.split(key, 24)
    f32 = jnp.float32
    nrm = lambda k, shape, s: jax.random.normal(k, shape, f32) * s
    dt0 = jnp.exp(jax.random.uniform(ks[17], (DEPTH, 2, SSD_HEADS), f32, math.log(1e-3), math.log(1e-1)))
    return {
        "x_prompt": nrm(ks[0], (BATCH, SEQ, D_MODEL), 1.0),
        "x_sample": nrm(ks[1], (DEC_BATCH, DEC_SEQ, D_MODEL), 1.0),
        "c": nrm(ks[2], (DEC_BATCH, D_MODEL), 1.0),
        "state_ssd_fwd": nrm(ks[3], (DEC_BATCH, DEPTH, SSD_HEADS, SSD_HEAD_DIM, SSD_STATE), 0.5),
        "state_ssd_bwd": nrm(ks[4], (DEC_BATCH, DEPTH, SSD_HEADS, SSD_HEAD_DIM, SSD_STATE), 0.5),
        "cache_attn_k": nrm(ks[5], (DEC_BATCH, DEPTH, PAST_LEN, NA_HEADS, NA_HEAD_DIM), 1.0),
        "cache_attn_v": nrm(ks[6], (DEC_BATCH, DEPTH, PAST_LEN, NA_HEADS, NA_HEAD_DIM), 1.0),
        "c_ctx": nrm(ks[7], (D_MODEL,), 1.0),
        "mod_w": nrm(ks[8], (DEPTH, D_MODEL, N_MOD * D_MODEL), 0.5 * D_MODEL ** -0.5),
        "mod_b": nrm(ks[9], (DEPTH, N_MOD * D_MODEL), 0.02),
        "norm_pre": 1.0 + nrm(ks[10], (DEPTH, 3, D_MODEL), 0.05),
        "norm_post": 1.0 + nrm(ks[11], (DEPTH, 3, D_MODEL), 0.05),
        "ffn_w13": nrm(ks[12], (DEPTH, 2, D_MODEL, 2 * FF_HIDDEN), D_MODEL ** -0.5),
        "ffn_w2": nrm(ks[13], (DEPTH, 2, FF_HIDDEN, D_MODEL), FF_HIDDEN ** -0.5),
        "w_in": nrm(ks[14], (DEPTH, D_MODEL, IN_COLS), D_MODEL ** -0.5),
        "w_out": nrm(ks[15], (DEPTH, MIX_WIDTH, D_MODEL), MIX_WIDTH ** -0.5),
        "ssd_conv_w": nrm(ks[16], (DEPTH, SSD_CONV, SSD_CONV_DIM), SSD_CONV ** -0.5),
        "ssd_conv_b": nrm(ks[18], (DEPTH, SSD_CONV_DIM), 0.01),
        "ssd_dt_bias": dt0 + jnp.log(-jnp.expm1(-dt0)),
        "ssd_a_log": jnp.log(jax.random.uniform(ks[19], (DEPTH, 2, SSD_HEADS), f32, 1.0, 16.0)),
        "ssd_d": 1.0 + nrm(ks[20], (DEPTH, SSD_HEADS), 0.1),
        "ssd_norm": 1.0 + nrm(ks[21], (DEPTH, SSD_INNER), 0.05),
        "na_rpb": nrm(ks[22], (DEPTH, NA_HEADS, 2 * NA_WIN_ROWS - 1, 2 * NA_WIN_COLS - 1), 0.1),
    }


def reference(x_prompt, x_sample, c, state_ssd_fwd, state_ssd_bwd, cache_attn_k, cache_attn_v,
              c_ctx, mod_w, mod_b, norm_pre, norm_post, ffn_w13, ffn_w2, w_in, w_out,
              ssd_conv_w, ssd_conv_b, ssd_dt_bias, ssd_a_log, ssd_d, ssd_norm, na_rpb):
    xp = x_prompt
    xs = x_sample
    new_sf, new_sb, new_k, new_v = [], [], [], []
    for l in range(DEPTH):
        p = {"w_in": w_in[l], "w_out": w_out[l], "norm_pre": norm_pre[l], "norm_post": norm_post[l],
             "w13": ffn_w13[l], "w2": ffn_w2[l], "conv_w": ssd_conv_w[l], "conv_b": ssd_conv_b[l],
             "dt_bias": ssd_dt_bias[l], "a_log": ssd_a_log[l], "d": ssd_d[l], "ssd_norm": ssd_norm[l],
             "rpb": na_rpb[l]}
        m_ctx = (jax.nn.silu(c_ctx) @ mod_w[l] + mod_b[l]).reshape(N_MOD, D_MODEL)
        mods_ctx = [m_ctx[i] for i in range(N_MOD)]
        xp, (s_f, s_b, k_c, v_c) = trunk_layer(xp, mods_ctx, p, lambda h, p=p: context_mixer(h, p))
        new_sf.append(s_f)
        new_sb.append(s_b)
        new_k.append(k_c)
        new_v.append(v_c)
        m_lat = (jax.nn.silu(c) @ mod_w[l] + mod_b[l]).reshape(c.shape[0], N_MOD, 1, D_MODEL)
        mods_lat = [m_lat[:, i] for i in range(N_MOD)]
        xs, _ = trunk_layer(xs, mods_lat, p,
                            lambda h, p=p, l=l: latent_mixer(h, p, state_ssd_fwd[:, l], state_ssd_bwd[:, l],
                                                             cache_attn_k[:, l], cache_attn_v[:, l]))
    new_state_ssd_fwd = jnp.stack(new_sf, axis=1)
    new_state_ssd_bwd = jnp.stack(new_sb, axis=1)
    new_cache_attn_k = jnp.stack(new_k, axis=1)
    new_cache_attn_v = jnp.stack(new_v, axis=1)
    return (xp, xs, new_state_ssd_fwd, new_state_ssd_bwd, new_cache_attn_k, new_cache_attn_v)
```

```python
import functools
import math

import numpy as np
import jax
import jax.numpy as jnp
from jax import lax
from jax.experimental import pallas as pl
from jax.experimental.pallas import tpu as pltpu

F32 = jnp.float32
BF16 = jnp.bfloat16

D_MODEL = 1024
DEPTH = 2
GRID_W = 64
FF_HIDDEN = 2816
N_MOD = 9
HEAD_DIM = 64
FN_WIDTH = 256
FN_GROUPS = 4
SSD_INNER = 384
SSD_HEADS = 6
SSD_STATE = 64
SSD_NGROUPS = 2
SSD_CONV = 5
SSD_CHUNK = 128
SSD_CONV_DIM = 640
SSD_IN = 1420
NA_WIDTH = 384
NA_HEADS = 6
NA_WIN_ROWS = 8
NA_WIN_COLS = 16
ROPE_BASE = 10000.0
EPS = 1e-6

LANES = 128
VMEM_LIMIT = 56 * 1024 * 1024

SLOTS = 8
SSD_PAD = SLOTS * HEAD_DIM
U_FN = FN_WIDTH
U_SSD = 2 * SSD_PAD + 2 * LANES + LANES
U_QKV = 3 * NA_WIDTH
U_TOTAL = U_FN + U_SSD + U_QKV
NEG_BIG = -1e30


def _slot_of_head(h):
    return 4 * (h // 3) + (h % 3)


def _cparams(sem):
    return pltpu.CompilerParams(dimension_semantics=sem, vmem_limit_bytes=VMEM_LIMIT)


def _rms(x):
    return x * lax.rsqrt(jnp.mean(x * x, axis=-1, keepdims=True) + EPS)


def _silu(x):
    return x * jax.nn.sigmoid(x)


def _bdot(a, b):
    return jnp.dot(a.astype(BF16), b.astype(BF16), preferred_element_type=F32)


def _bdot_nt(a, b):
    return lax.dot_general(a.astype(BF16), b.astype(BF16), (((1,), (1,)), ((), ())),
                           preferred_element_type=F32)


MOD_TN = 1152


def _mods_kernel(c_ref, w_ref, b_ref, o_ref):
    s = _silu(c_ref[...])
    o_ref[0] = _bdot(s, w_ref[0]) + b_ref[0]


def _mods(cvec, mod_w, mod_b):
    ncol = N_MOD * D_MODEL
    return pl.pallas_call(
        _mods_kernel,
        grid=(DEPTH, ncol // MOD_TN),
        in_specs=[pl.BlockSpec((8, D_MODEL), lambda l, j: (0, 0)),
                  pl.BlockSpec((1, D_MODEL, MOD_TN), lambda l, j: (l, 0, j)),
                  pl.BlockSpec((1, 1, MOD_TN), lambda l, j: (l, 0, j))],
        out_specs=pl.BlockSpec((1, 8, MOD_TN), lambda l, j: (l, 0, j)),
        out_shape=jax.ShapeDtypeStruct((DEPTH, 8, ncol), F32),
        compiler_params=_cparams(("parallel", "parallel")),
        name="mods",
    )(cvec, mod_w, mod_b.reshape(DEPTH, 1, ncol))


FFN_TH = 256


def _ffn_kernel(x_ref, m_ref, gpre_ref, gpost_ref, wg_ref, wu_ref, w2_ref, o_ref, h_ref, acc_ref):
    j = pl.program_id(1)

    @pl.when(j == 0)
    def _():
        x = x_ref[...]
        shift = m_ref[0, 0:1, :]
        scale = m_ref[0, 1:2, :]
        h = _rms(x) * gpre_ref[...] * (1.0 + scale) + shift
        h_ref[...] = h.astype(BF16)
        acc_ref[...] = jnp.zeros_like(acc_ref)

    h = h_ref[...]
    g = jnp.dot(h, wg_ref[...], preferred_element_type=F32)
    u = jnp.dot(h, wu_ref[...], preferred_element_type=F32)
    a = (_silu(g) * u).astype(BF16)
    acc_ref[...] += jnp.dot(a, w2_ref[...], preferred_element_type=F32)

    @pl.when(j == pl.num_programs(1) - 1)
    def _():
        gate = m_ref[0, 2:3, :]
        y = _rms(acc_ref[...]) * gpost_ref[...]
        o_ref[...] = x_ref[...] + 0.5 * gate * y


def _ffn(x, gmods, g_pre, g_post, w13, w2, rows_per_group, tm):
    n = x.shape[0]
    nj = FF_HIDDEN // FFN_TH
    tpg = rows_per_group // tm
    return pl.pallas_call(
        _ffn_kernel,
        grid=(n // tm, nj),
        in_specs=[pl.BlockSpec((tm, D_MODEL), lambda i, j: (i, 0)),
                  pl.BlockSpec((1, 3, D_MODEL), lambda i, j: (i // tpg, 0, 0)),
                  pl.BlockSpec((1, D_MODEL), lambda i, j: (0, 0)),
                  pl.BlockSpec((1, D_MODEL), lambda i, j: (0, 0)),
                  pl.BlockSpec((D_MODEL, FFN_TH), lambda i, j: (0, j)),
                  pl.BlockSpec((D_MODEL, FFN_TH), lambda i, j: (0, j + nj)),
                  pl.BlockSpec((FFN_TH, D_MODEL), lambda i, j: (j, 0))],
        out_specs=pl.BlockSpec((tm, D_MODEL), lambda i, j: (i, 0)),
        out_shape=jax.ShapeDtypeStruct((n, D_MODEL), F32),
        scratch_shapes=[pltpu.VMEM((tm, D_MODEL), BF16), pltpu.VMEM((tm, D_MODEL), F32)],
        compiler_params=_cparams(("parallel", "arbitrary")),
        name="ffn",
    )(x, gmods, g_pre.reshape(1, D_MODEL), g_post.reshape(1, D_MODEL), w13, w13, w2)


def _inproj_kernel(x_ref, m_ref, gpre_ref, w_ref, ofn_ref, ossd_ref, oqkv_ref):
    shift = m_ref[0, 0:1, :]
    scale = m_ref[0, 1:2, :]
    h = (_rms(x_ref[...]) * gpre_ref[...] * (1.0 + scale) + shift).astype(BF16)
    u = jnp.dot(h, w_ref[...], preferred_element_type=F32)
    ofn_ref[...] = u[:, :U_FN]
    ossd_ref[...] = u[:, U_FN:U_FN + U_SSD]
    oqkv_ref[...] = u[:, U_FN + U_SSD:]


def _inproj(x, gmods, g_pre, w_in_pad, rows_per_group, tm):
    n = x.shape[0]
    tpg = rows_per_group // tm
    return pl.pallas_call(
        _inproj_kernel,
        grid=(n // tm,),
        in_specs=[pl.BlockSpec((tm, D_MODEL), lambda i: (i, 0)),
                  pl.BlockSpec((1, 3, D_MODEL), lambda i: (i // tpg, 0, 0)),
                  pl.BlockSpec((1, D_MODEL), lambda i: (0, 0)),
                  pl.BlockSpec((D_MODEL, U_TOTAL), lambda i: (0, 0))],
        out_specs=[pl.BlockSpec((tm, U_FN), lambda i: (i, 0)),
                   pl.BlockSpec((tm, U_SSD), lambda i: (i, 0)),
                   pl.BlockSpec((tm, U_QKV), lambda i: (i, 0))],
        out_shape=[jax.ShapeDtypeStruct((n, U_FN), F32),
                   jax.ShapeDtypeStruct((n, U_SSD), F32),
                   jax.ShapeDtypeStruct((n, U_QKV), F32)],
        compiler_params=_cparams(("parallel",)),
        name="inproj",
    )(x, gmods, g_pre.reshape(1, D_MODEL), w_in_pad)


def _outproj_kernel(nhalf, x_ref, m_ref, gpost_ref, fa_ref, sa_ref, aa_ref, fb_ref, sb_ref, ab_ref,
                    wf_ref, ws_ref, wa_ref, o_ref):
    i = pl.program_id(0)

    def finish(f_ref, s_ref, a_ref):
        y = (_bdot(f_ref[...], wf_ref[...]) + _bdot(s_ref[...], ws_ref[...])
             + _bdot(a_ref[...], wa_ref[...]))
        gate = m_ref[0, 0:1, :]
        o_ref[...] = x_ref[...] + gate * (_rms(y) * gpost_ref[...])

    @pl.when(i < nhalf)
    def _():
        finish(fa_ref, sa_ref, aa_ref)

    @pl.when(i >= nhalf)
    def _():
        finish(fb_ref, sb_ref, ab_ref)


def _outproj(x, gmods, g_post, mix_ctx, mix_lat, w_fn, w_ssd, w_att, rows_per_group, tm):
    n = x.shape[0]
    nhalf = (n // 2) // tm
    tpg = rows_per_group // tm
    first = lambda i: (jnp.minimum(i, nhalf - 1), 0)
    second = lambda i: (jnp.maximum(i - nhalf, 0), 0)
    widths = (FN_WIDTH, SSD_PAD, NA_WIDTH)
    return pl.pallas_call(
        functools.partial(_outproj_kernel, nhalf),
        grid=(n // tm,),
        in_specs=[pl.BlockSpec((tm, D_MODEL), lambda i: (i, 0)),
                  pl.BlockSpec((1, 1, D_MODEL), lambda i: (i // tpg, 0, 0)),
                  pl.BlockSpec((1, D_MODEL), lambda i: (0, 0))]
                 + [pl.BlockSpec((tm, w), first) for w in widths]
                 + [pl.BlockSpec((tm, w), second) for w in widths]
                 + [pl.BlockSpec((w, D_MODEL), lambda i: (0, 0)) for w in widths],
        out_specs=pl.BlockSpec((tm, D_MODEL), lambda i: (i, 0)),
        out_shape=jax.ShapeDtypeStruct((n, D_MODEL), F32),
        compiler_params=_cparams(("arbitrary",)),
        name="outproj",
    )(x, gmods, g_post.reshape(1, D_MODEL), *mix_ctx, *mix_lat, w_fn, w_ssd, w_att)


def _dft_tables(L):
    k = np.arange(L, dtype=np.int64)
    ang = 2.0 * np.pi * ((k[:, None] * k[None, :]) % L).astype(np.float64) / L
    sc = 1.0 / math.sqrt(L * HEAD_DIM)
    cl = (np.cos(ang) * sc).astype(np.float32)
    sl = (-np.sin(ang) * sc).astype(np.float32)
    m = np.arange(HEAD_DIM, dtype=np.int64)
    a64 = 2.0 * np.pi * ((m[:, None] * m[None, :]) % HEAD_DIM).astype(np.float64) / HEAD_DIM
    eye = np.eye(FN_GROUPS)
    w1 = np.concatenate([np.kron(eye, np.cos(a64)), np.kron(eye, np.sin(a64))], axis=1).astype(np.float32)
    return cl, sl, w1


def _fourier_kernel(u_ref, w1_ref, cl_ref, sl_ref, o_ref, ab_ref):
    rt = pl.program_id(0)
    b = pl.program_id(1)

    @pl.when(rt == 0)
    def _():
        ab_ref[b] = _bdot(u_ref[...], w1_ref[...]).astype(BF16)

    ab = ab_ref[b]
    o_ref[...] = (jnp.dot(cl_ref[...], ab[:, :FN_WIDTH], preferred_element_type=F32)
                  + jnp.dot(sl_ref[...], ab[:, FN_WIDTH:], preferred_element_type=F32))


def _fourier(u_fn, row0, nb, L):
    cl, sl, w1 = _dft_tables(L)
    cl = jnp.asarray(cl).astype(BF16)
    sl = jnp.asarray(sl).astype(BF16)
    w1 = jnp.asarray(w1).astype(BF16)
    tl = min(L, 512)
    nrt = L // tl
    blk0 = row0 // L
    return pl.pallas_call(
        _fourier_kernel,
        grid=(nrt, nb),
        in_specs=[pl.BlockSpec((L, FN_WIDTH), lambda rt, b: (blk0 + jnp.where(rt == 0, b, nb - 1), 0)),
                  pl.BlockSpec((FN_WIDTH, 2 * FN_WIDTH), lambda rt, b: (0, 0)),
                  pl.BlockSpec((tl, L), lambda rt, b: (rt, 0)),
                  pl.BlockSpec((tl, L), lambda rt, b: (rt, 0))],
        out_specs=pl.BlockSpec((tl, FN_WIDTH), lambda rt, b: (b * nrt + rt, 0)),
        out_shape=jax.ShapeDtypeStruct((nb * L, FN_WIDTH), F32),
        scratch_shapes=[pltpu.VMEM((nb, L, 2 * FN_WIDTH), BF16)],
        compiler_params=_cparams(("arbitrary", "arbitrary")),
        name="fourier",
    )(u_fn, w1, cl, sl)


def _head_masks():
    lane = lax.broadcasted_iota(jnp.int32, (1, LANES), 1)
    return lane < HEAD_DIM


def _ctx_attn_kernel(q_ref, k_ref, v_ref, o_ref):
    low = _head_masks()
    q = q_ref[...]
    k = k_ref[...].astype(BF16)
    v = v_ref[...].astype(BF16)
    outs = []
    for hh in range(2):
        sel = low if hh == 0 else jnp.logical_not(low)
        qm = jnp.where(sel, q, 0.0)
        s = _bdot_nt(qm, k)
        m = jnp.max(s, axis=-1, keepdims=True)
        p = jnp.exp(s - m)
        l = jnp.sum(p, axis=-1, keepdims=True)
        pn = p / l
        outs.append(jnp.dot(pn.astype(BF16), v, preferred_element_type=F32))
    o_ref[...] = jnp.where(low, outs[0], outs[1])


def _ctx_attn(u_qkv, nb, L):
    npair = NA_HEADS // 2
    return pl.pallas_call(
        _ctx_attn_kernel,
        grid=(nb, npair),
        in_specs=[pl.BlockSpec((L, LANES), lambda b, p: (b, p)),
                  pl.BlockSpec((L, LANES), lambda b, p: (b, npair + p)),
                  pl.BlockSpec((L, LANES), lambda b, p: (b, 2 * npair + p))],
        out_specs=pl.BlockSpec((L, LANES), lambda b, p: (b, p)),
        out_shape=jax.ShapeDtypeStruct((nb * L, NA_WIDTH), F32),
        compiler_params=_cparams(("parallel", "parallel")),
        name="ctx_attn",
    )(u_qkv, u_qkv, u_qkv)


NA_QR = 8
NA_QC = 16
NA_KR = 16
NA_KC = 32
NA_TQ = NA_QR * NA_QC
NA_TK = NA_KR * NA_KC


def _na_tile_geometry(rows):
    wr = min(NA_WIN_ROWS, rows)
    n_rb = rows // NA_QR
    n_cb = GRID_W // NA_QC
    rb = [0, 1, n_rb - 1]
    cb = [0, 1, n_cb - 1]
    dr = np.zeros((3, NA_QR, NA_KR), np.int64)
    vr = np.zeros((3, NA_QR, NA_KR), bool)
    for ci, i in enumerate(rb):
        kr0 = int(np.clip(NA_QR * i - NA_WIN_ROWS // 2, 0, rows - NA_KR))
        for rr in range(NA_QR):
            r = NA_QR * i + rr
            rs = int(np.clip(r - wr // 2, 0, rows - wr))
            for kk in range(NA_KR):
                kr = kr0 + kk
                vr[ci, rr, kk] = rs <= kr < rs + wr
                dr[ci, rr, kk] = np.clip(kr - r + NA_WIN_ROWS - 1, 0, 2 * NA_WIN_ROWS - 2)
    dc = np.zeros((3, NA_QC, NA_KC), np.int64)
    vc = np.zeros((3, NA_QC, NA_KC), bool)
    for ci, j in enumerate(cb):
        kc0 = int(np.clip(NA_QC * j - NA_WIN_COLS // 2, 0, GRID_W - NA_KC))
        for cq in range(NA_QC):
            c = NA_QC * j + cq
            cs = int(np.clip(c - NA_WIN_COLS // 2, 0, GRID_W - NA_WIN_COLS))
            for ck in range(NA_KC):
                kc = kc0 + ck
                vc[ci, cq, ck] = cs <= kc < cs + NA_WIN_COLS
                dc[ci, cq, ck] = np.clip(kc - c + NA_WIN_COLS - 1, 0, 2 * NA_WIN_COLS - 2)
    return dr, vr, dc, vc


def _na_bias_tables(rpb, rows):
    dr, vr, dc, vc = _na_tile_geometry(rows)
    oh_r = (dr[..., None] == np.arange(2 * NA_WIN_ROWS - 1)).astype(np.float32)
    oh_c = (dc[..., None] == np.arange(2 * NA_WIN_COLS - 1)).astype(np.float32)
    t = jnp.einsum("xrka,hab,ycqb->hxyrckq", jnp.asarray(oh_r), rpb.astype(F32), jnp.asarray(oh_c),
                   precision=lax.Precision.HIGHEST)
    valid = vr[:, None, :, None, :, None] & vc[None, :, None, :, None, :]
    t = jnp.where(jnp.asarray(valid)[None], t, NEG_BIG)
    return t.reshape(rpb.shape[0], 3, 3, NA_TQ, NA_TK)


def _na_kernel(rows, q_ref, k_ref, v_ref, kc_ref, vc_ref, bias_ref, o_ref):
    n_rb = rows // NA_QR
    n_cb = GRID_W // NA_QC
    low = _head_masks()
    kctx = kc_ref[0, 0].astype(BF16)
    vctx = vc_ref[0, 0].astype(BF16)

    def row_block(i, carry):
        kr0 = jnp.clip(NA_QR * i - NA_WIN_ROWS // 2, 0, rows - NA_KR)
        rcfg = (i > 0).astype(jnp.int32) + (i == n_rb - 1).astype(jnp.int32)
        for j in range(n_cb):
            kc0 = int(np.clip(NA_QC * j - NA_WIN_COLS // 2, 0, GRID_W - NA_KC))
            ccfg = 0 if j == 0 else (2 if j == n_cb - 1 else 1)
            q_parts = []
            for rr in range(NA_QR):
                start = pl.multiple_of((NA_QR * i + rr) * GRID_W + NA_QC * j, NA_QC)
                q_parts.append(q_ref[pl.ds(start, NA_QC), :])
            q = jnp.concatenate(q_parts, axis=0)
            k_parts, v_parts = [], []
            for kk in range(NA_KR):
                start = pl.multiple_of((kr0 + kk) * GRID_W + kc0, 8)
                k_parts.append(k_ref[pl.ds(start, NA_KC), :])
                v_parts.append(v_ref[pl.ds(start, NA_KC), :])
            k = jnp.concatenate(k_parts, axis=0).astype(BF16)
            v = jnp.concatenate(v_parts, axis=0).astype(BF16)
            outs = []
            for hh in range(2):
                sel = low if hh == 0 else jnp.logical_not(low)
                qm = jnp.where(sel, q, 0.0).astype(BF16)
                s_loc = _bdot_nt(qm, k) + bias_ref[hh, rcfg, ccfg]
                s_ctx = _bdot_nt(qm, kctx)
                m = jnp.maximum(jnp.max(s_loc, axis=-1, keepdims=True),
                                jnp.max(s_ctx, axis=-1, keepdims=True))
                p_loc = jnp.exp(s_loc - m)
                p_ctx = jnp.exp(s_ctx - m)
                l = jnp.sum(p_loc, axis=-1, keepdims=True) + jnp.sum(p_ctx, axis=-1, keepdims=True)
                inv = 1.0 / l
                o = (jnp.dot((p_loc * inv).astype(BF16), v, preferred_element_type=F32)
                     + jnp.dot((p_ctx * inv).astype(BF16), vctx, preferred_element_type=F32))
                outs.append(o)
            o = jnp.where(low, outs[0], outs[1])
            for rr in range(NA_QR):
                start = pl.multiple_of((NA_QR * i + rr) * GRID_W + NA_QC * j, NA_QC)
                o_ref[pl.ds(start, NA_QC), :] = o[rr * NA_QC:(rr + 1) * NA_QC, :]
        return carry

    lax.fori_loop(0, n_rb, row_block, 0)


def _na_attn(u_qkv, row0, nb, L, cache_k, cache_v, layer, bias):
    npair = NA_HEADS // 2
    rows = L // GRID_W
    blk0 = row0 // L
    lc = cache_k.shape[2]
    return pl.pallas_call(
        functools.partial(_na_kernel, rows),
        grid=(npair, nb),
        in_specs=[pl.BlockSpec((L, LANES), lambda p, b: (blk0 + b, p)),
                  pl.BlockSpec((L, LANES), lambda p, b: (blk0 + b, npair + p)),
                  pl.BlockSpec((L, LANES), lambda p, b: (blk0 + b, 2 * npair + p)),
                  pl.BlockSpec((1, 1, lc, LANES), lambda p, b: (b, layer, 0, p)),
                  pl.BlockSpec((1, 1, lc, LANES), lambda p, b: (b, layer, 0, p)),
                  pl.BlockSpec((2, 3, 3, NA_TQ, NA_TK), lambda p, b: (p, 0, 0, 0, 0))],
        out_specs=pl.BlockSpec((L, LANES), lambda p, b: (b, p)),
        out_shape=jax.ShapeDtypeStruct((nb * L, NA_WIDTH), F32),
        compiler_params=_cparams(("parallel", "parallel")),
        name="na_attn",
    )(u_qkv, u_qkv, u_qkv, cache_k, cache_v, bias)


def _ssd_constants():
    tril = np.tril(np.ones((SSD_CHUNK, SSD_CHUNK), np.float32))
    expand = np.zeros((2, LANES, SSD_PAD), np.float32)
    colb = np.zeros((2, LANES, SSD_HEADS * LANES), np.float32)
    for d in range(2):
        for h in range(SSD_HEADS):
            s = _slot_of_head(h)
            expand[d, SLOTS * d + s, HEAD_DIM * s:HEAD_DIM * (s + 1)] = 1.0
            colb[d, SLOTS * d + s, LANES * h:LANES * (h + 1)] = 1.0
    two = lambda m: np.concatenate([m, m], axis=-2)
    return np.concatenate([tril, tril], axis=1), two(expand), two(colb)


def _rope_tables(L):
    t = np.arange(L)
    rows = (t // GRID_W).astype(np.float64)
    cols = (t % GRID_W).astype(np.float64)
    quarter = SSD_STATE // 4
    inv = ROPE_BASE ** (-np.arange(quarter, dtype=np.float64) / quarter)
    n = np.arange(SSD_STATE)
    pos = np.where(n[None, :] < SSD_STATE // 2, rows[:, None], cols[:, None])
    ang = pos * inv[n % quarter][None, :]
    first = (n % (SSD_STATE // 2)) < quarter
    cos = np.cos(ang)
    sin = np.where(first[None, :], -np.sin(ang), np.sin(ang))
    tile = lambda a: np.concatenate([a] * SSD_NGROUPS, axis=1).astype(np.float32)
    return tile(cos), tile(sin)


def _split2(x):
    hi = x.astype(BF16)
    lo = (x - hi.astype(F32)).astype(BF16)
    return hi, lo


def _ssd_kernel(L, use_rope, use_init, *refs):
    (u_ref, convw_ref, convb_ref, dtb_ref, a_ref, ax_ref, d_ref, nw_ref,
     tril_ref, exp_ref, colb_ref) = refs[:11]
    pos = 11
    if use_rope:
        cos_ref, sin_ref = refs[pos:pos + 2]
        pos += 2
    if use_init:
        s0_ref = refs[pos]
        pos += 1
    y_ref, sfin_ref = refs[pos:pos + 2]
    act_ref, dt_ref, s_ref = refs[pos + 2:]

    nc = L // SSD_CHUNK
    C = SSD_CHUNK
    GW = SSD_PAD // SSD_NGROUPS
    XC = SSD_PAD
    CONV_W = SSD_PAD + 2 * LANES
    DTC = XC + CONV_W
    HALO = 8

    lane = lax.broadcasted_iota(jnp.int32, (1, LANES), 1)
    first_q = (lane % (SSD_STATE // 2)) < (SSD_STATE // 4)

    def prep(c, carry):
        r0 = pl.multiple_of(c * C, C)
        main = u_ref[pl.ds(r0, C), XC:XC + CONV_W]
        pstart = pl.multiple_of(jnp.maximum(r0 - HALO, 0), HALO)
        nstart = pl.multiple_of(jnp.minimum(r0 + C, L - HALO), HALO)
        prev = u_ref[pl.ds(pstart, HALO), XC:XC + CONV_W] * (c > 0).astype(F32)
        nxt = u_ref[pl.ds(nstart, HALO), XC:XC + CONV_W] * (c < nc - 1).astype(F32)
        win = jnp.concatenate([prev, main, nxt], axis=0)
        acc = jnp.zeros((C, CONV_W), F32) + convb_ref[...]
        for k in range(SSD_CONV):
            off = HALO + k - SSD_CONV // 2
            acc = acc + win[off:off + C, :] * convw_ref[k:k + 1, :]
        act = _silu(acc)
        act_ref[pl.ds(r0, C), 0:SSD_PAD] = act[:, 0:SSD_PAD]
        for t in range(2):
            bc = act[:, SSD_PAD + t * LANES:SSD_PAD + (t + 1) * LANES]
            if use_rope:
                partner = jnp.where(first_q, pltpu.roll(bc, LANES - SSD_STATE // 4, 1),
                                    pltpu.roll(bc, SSD_STATE // 4, 1))
                bc = bc * cos_ref[pl.ds(r0, C), :] + partner * sin_ref[pl.ds(r0, C), :]
            act_ref[pl.ds(r0, C), SSD_PAD + t * LANES:SSD_PAD + (t + 1) * LANES] = bc
        raw = u_ref[pl.ds(r0, C), DTC:DTC + LANES] + dtb_ref[...]
        dt_ref[pl.ds(r0, C), :] = jnp.maximum(raw, 0.0) + jnp.log1p(jnp.exp(-jnp.abs(raw)))
        y_ref[pl.ds(r0, C), :] = jnp.zeros((C, SSD_PAD), F32)
        return carry

    lax.fori_loop(0, nc, prep, 0)

    if use_init:
        s_ref[...] = s0_ref[0]
    else:
        s_ref[...] = jnp.zeros_like(s_ref)

    a_row = a_ref[...]
    li = lax.broadcasted_iota(jnp.int32, (C, C), 0)
    si = lax.broadcasted_iota(jnp.int32, (C, C), 1)
    causal = [li >= si, si >= li]
    low64 = lane < HEAD_DIM
    grp_mask = [low64, jnp.logical_not(low64)]

    def one_direction(d, c):
        r0 = pl.multiple_of(c * C, C)
        x = act_ref[pl.ds(r0, C), 0:SSD_PAD]
        bmat = act_ref[pl.ds(r0, C), SSD_PAD:SSD_PAD + LANES]
        cmat = act_ref[pl.ds(r0, C), SSD_PAD + LANES:SSD_PAD + 2 * LANES]
        dt = dt_ref[pl.ds(r0, C), :]
        dta = dt * a_row
        hi, lo = _split2(dta)
        cs = jnp.dot(tril_ref[...], jnp.concatenate([hi, lo], axis=0),
                     preferred_element_type=F32)
        q = cs if d == 0 else cs - dta
        expand = lambda v: jnp.dot(jnp.concatenate(_split2(v), axis=1), exp_ref[d],
                                   preferred_element_type=F32)
        dt_x = expand(dt)
        cs_x = expand(cs)
        end_x = cs_x[C - 1:C, :]
        if d == 0:
            off_scale = jnp.exp(cs_x)
            w_state = jnp.exp(end_x - cs_x)
        else:
            e_x = cs_x - dt_x * ax_ref[d]
            off_scale = jnp.exp(end_x - e_x)
            w_state = jnp.exp(e_x)
        chunk_decay = jnp.exp(end_x)
        xdt = x * dt_x
        xdt_b = xdt.astype(BF16)
        rhs_state = (xdt * w_state).astype(BF16)
        qcol = jnp.dot(jnp.concatenate(_split2(q), axis=1), colb_ref[d],
                       preferred_element_type=F32)
        q_t = q.T
        b_t = bmat.T.astype(BF16)
        b_b = bmat.astype(BF16)
        for g in range(SSD_NGROUPS):
            cm = jnp.where(grp_mask[g], cmat, 0.0).astype(BF16)
            gmat = lax.dot_general(cm, b_b, (((1,), (1,)), ((), ())), preferred_element_type=F32)
            s_old = s_ref[d, g]
            y_off = (jnp.dot(cm, s_old.astype(BF16), preferred_element_type=F32)
                     * off_scale[:, g * GW:(g + 1) * GW])
            res = []
            for hh in range(3):
                h = 3 * g + hh
                slot = 4 * g + hh
                row = q_t[SLOTS * d + slot:SLOTS * d + slot + 1, :]
                col = qcol[:, h * LANES:(h + 1) * LANES]
                seg = (col - row) if d == 0 else (row - col)
                m = (gmat * jnp.exp(jnp.where(causal[d], seg, NEG_BIG))).astype(BF16)
                pair = xdt_b[:, (slot // 2) * LANES:(slot // 2 + 1) * LANES]
                res.append(jnp.dot(m, pair, preferred_element_type=F32))
            y_g = jnp.concatenate([jnp.where(low64, res[0], res[1]), res[2]], axis=1) + y_off
            y_ref[pl.ds(r0, C), g * GW:(g + 1) * GW] += y_g
            st = jnp.dot(b_t, rhs_state[:, g * GW:(g + 1) * GW], preferred_element_type=F32)
            s_ref[d, g] = s_old * chunk_decay[:, g * GW:(g + 1) * GW] + st

    def scan(i, carry):
        one_direction(0, i)
        one_direction(1, nc - 1 - i)
        return carry

    lax.fori_loop(0, nc, scan, 0)
    sfin_ref[0] = s_ref[...]

    def finish(c, carry):
        r0 = pl.multiple_of(c * C, C)
        y = y_ref[pl.ds(r0, C), :] + act_ref[pl.ds(r0, C), 0:SSD_PAD] * d_ref[...]
        y = y * _silu(u_ref[pl.ds(r0, C), 0:SSD_PAD])
        for g in range(SSD_NGROUPS):
            yg = y[:, g * GW:(g + 1) * GW]
            ms = jnp.sum(yg * yg, axis=-1, keepdims=True) * (1.0 / (SSD_INNER // SSD_NGROUPS))
            y_ref[pl.ds(r0, C), g * GW:(g + 1) * GW] = (yg * lax.rsqrt(ms + EPS)
                                                        * nw_ref[:, g * GW:(g + 1) * GW])
        return carry

    lax.fori_loop(0, nc, finish, 0)


def _ssd(u_ssd, row0, nb, L, prm, use_rope, s0):
    blk0 = row0 // L
    tril2, exp2, colb2 = _ssd_constants()
    consts = [jnp.asarray(tril2, dtype=BF16), jnp.asarray(exp2, dtype=BF16), jnp.asarray(colb2, dtype=BF16)]
    full = lambda a: pl.BlockSpec(a.shape, lambda b, _n=a.ndim: (0,) * _n)
    small = [prm["conv_w"], prm["conv_b"], prm["dt_bias"], prm["a_row"], prm["a_x"], prm["d_row"],
             prm["norm_w"]] + consts
    args = [u_ssd] + small
    in_specs = [pl.BlockSpec((L, U_SSD), lambda b: (blk0 + b, 0))] + [full(a) for a in small]
    if use_rope:
        cos, sin = _rope_tables(L)
        tabs = [jnp.asarray(cos), jnp.asarray(sin)]
        args += tabs
        in_specs += [full(a) for a in tabs]
    if s0 is not None:
        args.append(s0)
        in_specs.append(pl.BlockSpec((1,) + s0.shape[1:], lambda b: (b, 0, 0, 0, 0)))
    sshape = (2, SSD_NGROUPS, LANES, SSD_PAD // SSD_NGROUPS)
    return pl.pallas_call(
        functools.partial(_ssd_kernel, L, use_rope, s0 is not None),
        grid=(nb,),
        in_specs=in_specs,
        out_specs=[pl.BlockSpec((L, SSD_PAD), lambda b: (b, 0)),
                   pl.BlockSpec((1,) + sshape, lambda b: (b, 0, 0, 0, 0))],
        out_shape=[jax.ShapeDtypeStruct((nb * L, SSD_PAD), F32),
                   jax.ShapeDtypeStruct((nb,) + sshape, F32)],
        scratch_shapes=[pltpu.VMEM((L, SSD_PAD + 2 * LANES), F32),
                        pltpu.VMEM((L, LANES), F32),
                        pltpu.VMEM(sshape, F32)],
        compiler_params=_cparams(("parallel",)),
        name="ssd",
    )(*args)


def _pad_heads(a, axis=-1):
    a = jnp.moveaxis(a, axis, -1)
    lead = a.shape[:-1]
    a = a.reshape(lead + (SSD_NGROUPS, 3, HEAD_DIM))
    a = jnp.pad(a, [(0, 0)] * len(lead) + [(0, 0), (0, 1), (0, 0)])
    return jnp.moveaxis(a.reshape(lead + (SSD_PAD,)), -1, axis)


def _pad_dt_lanes(a):
    lead = a.shape[:-2]
    a = a.reshape(lead + (2, SSD_NGROUPS, 3))
    a = jnp.pad(a, [(0, 0)] * len(lead) + [(0, 0), (0, 0), (0, 1)]).reshape(lead + (2 * SLOTS,))
    return jnp.pad(a, [(0, 0)] * len(lead) + [(0, LANES - 2 * SLOTS)])


def _layer_params(l, w_in, w_out, ssd_conv_w, ssd_conv_b, ssd_dt_bias, ssd_a_log, ssd_d, ssd_norm):
    w = w_in[l]
    o = FN_WIDTH
    z = _pad_heads(w[:, o:o + SSD_INNER])
    o += SSD_INNER
    xw = _pad_heads(w[:, o:o + SSD_INNER])
    bc = w[:, o + SSD_INNER:o + SSD_CONV_DIM]
    o += SSD_CONV_DIM
    dtw = _pad_dt_lanes(w[:, o:o + 2 * SSD_HEADS].reshape(D_MODEL, 2, SSD_HEADS))
    o += 2 * SSD_HEADS
    qw = w[:, o:o + NA_WIDTH] * (HEAD_DIM ** -0.5)
    kv = w[:, o + NA_WIDTH:]
    w_in_pad = jnp.concatenate([w[:, :FN_WIDTH], z, xw, bc, dtw, qw, kv], axis=1).astype(BF16)
    wo = w_out[l]
    a = -jnp.exp(ssd_a_log[l].astype(F32))
    a_x = jnp.repeat(_pad_dt_lanes(a)[:2 * SLOTS].reshape(2, SLOTS), HEAD_DIM, axis=1)
    cw = ssd_conv_w[l]
    cb = ssd_conv_b[l]
    ssd = {
        "conv_w": jnp.concatenate([_pad_heads(cw[:, :SSD_INNER]), cw[:, SSD_INNER:]], axis=1),
        "conv_b": jnp.concatenate([_pad_heads(cb[:SSD_INNER]), cb[SSD_INNER:]])[None, :],
        "dt_bias": _pad_dt_lanes(ssd_dt_bias[l])[None, :],
        "a_row": _pad_dt_lanes(a)[None, :],
        "a_x": a_x.reshape(2, 1, SSD_PAD),
        "d_row": _pad_heads(jnp.repeat(ssd_d[l], HEAD_DIM))[None, :],
        "norm_w": _pad_heads(ssd_norm[l])[None, :],
    }
    return {
        "w_in": w_in_pad,
        "w_out_fn": wo[:FN_WIDTH].astype(BF16),
        "w_out_ssd": _pad_heads(wo[FN_WIDTH:FN_WIDTH + SSD_INNER], axis=0).astype(BF16),
        "w_out_att": wo[FN_WIDTH + SSD_INNER:].astype(BF16),
        "ssd": ssd,
    }


def _state_to_kernel(s):
    b = s.shape[0]
    s = s.reshape(b, SSD_NGROUPS, 3, HEAD_DIM, SSD_STATE).transpose(0, 1, 4, 2, 3)
    s = jnp.pad(s, [(0, 0), (0, 0), (0, 0), (0, 1), (0, 0)]).reshape(b, SSD_NGROUPS, SSD_STATE, 4 * HEAD_DIM)
    out = jnp.zeros((b, SSD_NGROUPS, SSD_NGROUPS, SSD_STATE, 4 * HEAD_DIM), F32)
    for g in range(SSD_NGROUPS):
        out = out.at[:, g, g].set(s[:, g])
    return out.reshape(b, SSD_NGROUPS, SSD_NGROUPS * SSD_STATE, 4 * HEAD_DIM)


def _state_from_kernel(s):
    b = s.shape[0]
    s = s.reshape(b, SSD_NGROUPS, SSD_NGROUPS, SSD_STATE, 4, HEAD_DIM)
    s = jnp.stack([s[:, g, g] for g in range(SSD_NGROUPS)], axis=1)
    s = s[:, :, :, :3, :].transpose(0, 1, 3, 4, 2)
    return s.reshape(b, SSD_HEADS, HEAD_DIM, SSD_STATE)


def _pick_tile(rows, want):
    t = min(rows, want)
    while rows % t:
        t //= 2
    return t


def kernel(x_prompt, x_sample, c, state_ssd_fwd, state_ssd_bwd, cache_attn_k, cache_attn_v, c_ctx, mod_w, mod_b, norm_pre, norm_post, ffn_w13, ffn_w2, w_in, w_out, ssd_conv_w, ssd_conv_b, ssd_dt_bias, ssd_a_log, ssd_d, ssd_norm, na_rpb):
    nbp, lp, _ = x_prompt.shape
    nbs, ls, _ = x_sample.shape
    n_ctx, n_lat = nbp * lp, nbs * ls
    assert n_ctx == n_lat and n_ctx % ls == 0 and nbs + 1 <= 8
    rpg = ls
    x = jnp.concatenate([x_prompt.reshape(n_ctx, D_MODEL), x_sample.reshape(n_lat, D_MODEL)], axis=0)

    cvec = jnp.zeros((8, D_MODEL), F32).at[0].set(c_ctx).at[1:1 + nbs].set(c)
    mods = _mods(cvec, mod_w, mod_b).reshape(DEPTH, 8, N_MOD, D_MODEL)
    w13 = ffn_w13.astype(BF16)
    w2 = ffn_w2.astype(BF16)
    cache_k = cache_attn_k.reshape(nbs, DEPTH, cache_attn_k.shape[2], NA_WIDTH)
    cache_v = cache_attn_v.reshape(nbs, DEPTH, cache_attn_v.shape[2], NA_WIDTH)
    tm_ffn = _pick_tile(rpg, 1024)
    tm_proj = _pick_tile(rpg, 512)

    new_sf, new_sb, new_k, new_v = [], [], [], []
    for l in range(DEPTH):
        p = _layer_params(l, w_in, w_out, ssd_conv_w, ssd_conv_b, ssd_dt_bias, ssd_a_log, ssd_d, ssd_norm)
        gm = jnp.concatenate([jnp.broadcast_to(mods[l, 0], (n_ctx // rpg, N_MOD, D_MODEL)),
                              mods[l, 1:1 + nbs]], axis=0)
        x = _ffn(x, gm[:, 0:3], norm_pre[l, 0], norm_post[l, 0], w13[l, 0], w2[l, 0], rpg, tm_ffn)
        u_fn, u_ssd, u_qkv = _inproj(x, gm[:, 3:6], norm_pre[l, 1], p["w_in"], rpg, tm_proj)

        y_ssd_c, s_c = _ssd(u_ssd, 0, nbp, lp, p["ssd"], False, None)
        mix_ctx = (_fourier(u_fn, 0, nbp, lp), y_ssd_c, _ctx_attn(u_qkv, nbp, lp))
        new_sf.append(_state_from_kernel(s_c[:, 0]))
        new_sb.append(_state_from_kernel(s_c[:, 1]))
        new_k.append(u_qkv[:n_ctx, NA_WIDTH:2 * NA_WIDTH].reshape(nbp, lp, NA_HEADS, HEAD_DIM))
        new_v.append(u_qkv[:n_ctx, 2 * NA_WIDTH:].reshape(nbp, lp, NA_HEADS, HEAD_DIM))

        s0 = jnp.stack([_state_to_kernel(state_ssd_fwd[:, l].astype(F32)),
                        _state_to_kernel(state_ssd_bwd[:, l].astype(F32))], axis=1)
        y_ssd_l, _ = _ssd(u_ssd, n_ctx, nbs, ls, p["ssd"], True, s0)
        bias = _na_bias_tables(na_rpb[l], ls // GRID_W)
        mix_lat = (_fourier(u_fn, n_ctx, nbs, ls), y_ssd_l,
                   _na_attn(u_qkv, n_ctx, nbs, ls, cache_k, cache_v, l, bias))

        x = _outproj(x, gm[:, 5:6], norm_post[l, 1], mix_ctx, mix_lat,
                     p["w_out_fn"], p["w_out_ssd"], p["w_out_att"], rpg, tm_proj)
        x = _ffn(x, gm[:, 6:9], norm_pre[l, 2], norm_post[l, 2], w13[l, 1], w2[l, 1], rpg, tm_ffn)

    return (x[:n_ctx].reshape(nbp, lp, D_MODEL), x[n_ctx:].reshape(nbs, ls, D_MODEL),
            jnp.stack(new_sf, axis=1), jnp.stack(new_sb, axis=1),
            jnp.stack(new_k, axis=1), jnp.stack(new_v, axis=1))
```

```python
import functools
import math

import numpy as np
import jax
import jax.numpy as jnp
from jax import lax
from jax.experimental import pallas as pl
from jax.experimental.pallas import tpu as pltpu

F32 = jnp.float32
BF16 = jnp.bfloat16

D_MODEL = 1024
DEPTH = 2
GRID_W = 64
FF_HIDDEN = 2816
N_MOD = 9
HEAD_DIM = 64
FN_WIDTH = 256
FN_GROUPS = 4
SSD_INNER = 384
SSD_HEADS = 6
SSD_STATE = 64
SSD_NGROUPS = 2
SSD_CONV = 5
SSD_CHUNK = 128
SSD_CONV_DIM = 640
SSD_IN = 1420
NA_WIDTH = 384
NA_HEADS = 6
NA_WIN_ROWS = 8
NA_WIN_COLS = 16
ROPE_BASE = 10000.0
EPS = 1e-6

LANES = 128
VMEM_LIMIT = 56 * 1024 * 1024

SLOTS = 8
SSD_PAD = SLOTS * HEAD_DIM
U_FN = FN_WIDTH
U_SSD = 2 * SSD_PAD + 2 * LANES + LANES
U_QKV = 3 * NA_WIDTH
U_TOTAL = U_FN + U_SSD + U_QKV
NEG_BIG = -1e30


def _slot_of_head(h):
    return 4 * (h // 3) + (h % 3)


def _cparams(sem):
    return pltpu.CompilerParams(dimension_semantics=sem, vmem_limit_bytes=VMEM_LIMIT)


def _rms(x):
    return x * lax.rsqrt(jnp.mean(x * x, axis=-1, keepdims=True) + EPS)


def _silu(x):
    return x * jax.nn.sigmoid(x)


def _bdot(a, b):
    return jnp.dot(a.astype(BF16), b.astype(BF16), preferred_element_type=F32)


def _bdot_nt(a, b):
    return lax.dot_general(a.astype(BF16), b.astype(BF16), (((1,), (1,)), ((), ())),
                           preferred_element_type=F32)


MOD_TN = 1152


def _mods_kernel(c_ref, w_ref, b_ref, o_ref):
    s = _silu(c_ref[...])
    o_ref[0] = _bdot(s, w_ref[0]) + b_ref[0]


def _mods(cvec, mod_w, mod_b):
    ncol = N_MOD * D_MODEL
    return pl.pallas_call(
        _mods_kernel,
        grid=(DEPTH, ncol // MOD_TN),
        in_specs=[pl.BlockSpec((8, D_MODEL), lambda l, j: (0, 0)),
                  pl.BlockSpec((1, D_MODEL, MOD_TN), lambda l, j: (l, 0, j)),
                  pl.BlockSpec((1, 1, MOD_TN), lambda l, j: (l, 0, j))],
        out_specs=pl.BlockSpec((1, 8, MOD_TN), lambda l, j: (l, 0, j)),
        out_shape=jax.ShapeDtypeStruct((DEPTH, 8, ncol), F32),
        compiler_params=_cparams(("parallel", "parallel")),
        name="mods",
    )(cvec, mod_w, mod_b.reshape(DEPTH, 1, ncol))


FFN_TH = 256


def _halves(nhalf):
    first = lambda i, *_: (jnp.minimum(i, nhalf - 1), 0)
    second = lambda i, *_: (jnp.maximum(i - nhalf, 0), 0)
    return first, second


def _on_half(nhalf, fn):
    i = pl.program_id(0)
    pl.when(i < nhalf)(functools.partial(fn, 0))
    pl.when(i >= nhalf)(functools.partial(fn, 1))


def _ffn_kernel(nhalf, xa_ref, xb_ref, m_ref, gpre_ref, gpost_ref, wg_ref, wu_ref, w2_ref,
                oa_ref, ob_ref, h_ref, acc_ref):
    j = pl.program_id(1)
    x_refs = (xa_ref, xb_ref)
    o_refs = (oa_ref, ob_ref)

    def prologue(half):
        shift = m_ref[0, 0:1, :]
        scale = m_ref[0, 1:2, :]
        h = _rms(x_refs[half][...]) * (gpre_ref[...] * (1.0 + scale)) + shift
        h_ref[...] = h.astype(BF16)
        acc_ref[...] = jnp.zeros_like(acc_ref)

    pl.when(j == 0)(lambda: _on_half(nhalf, prologue))

    h = h_ref[...]
    g = jnp.dot(h, wg_ref[...], preferred_element_type=F32)
    u = jnp.dot(h, wu_ref[...], preferred_element_type=F32)
    a = (_silu(g) * u).astype(BF16)
    acc_ref[...] += jnp.dot(a, w2_ref[...], preferred_element_type=F32)

    def epilogue(half):
        gate = m_ref[0, 2:3, :]
        y = _rms(acc_ref[...]) * (gpost_ref[...] * (0.5 * gate))
        o_refs[half][...] = x_refs[half][...] + y

    pl.when(j == pl.num_programs(1) - 1)(lambda: _on_half(nhalf, epilogue))


def _ffn(xs, gmods, g_pre, g_post, w13, w2, layer, sub, rows_per_group, tm):
    nh = xs[0].shape[0]
    nhalf = nh // tm
    nj = FF_HIDDEN // FFN_TH
    tpg = rows_per_group // tm
    first, second = _halves(nhalf)
    return pl.pallas_call(
        functools.partial(_ffn_kernel, nhalf),
        grid=(2 * nhalf, nj),
        in_specs=[pl.BlockSpec((tm, D_MODEL), first),
                  pl.BlockSpec((tm, D_MODEL), second),
                  pl.BlockSpec((1, 3, D_MODEL), lambda i, j: (i // tpg, 0, 0)),
                  pl.BlockSpec((1, D_MODEL), lambda i, j: (0, 0)),
                  pl.BlockSpec((1, D_MODEL), lambda i, j: (0, 0)),
                  pl.BlockSpec((None, None, D_MODEL, FFN_TH), lambda i, j: (layer, sub, 0, j)),
                  pl.BlockSpec((None, None, D_MODEL, FFN_TH), lambda i, j: (layer, sub, 0, j + nj)),
                  pl.BlockSpec((None, None, FFN_TH, D_MODEL), lambda i, j: (layer, sub, j, 0))],
        out_specs=[pl.BlockSpec((tm, D_MODEL), first), pl.BlockSpec((tm, D_MODEL), second)],
        out_shape=[jax.ShapeDtypeStruct((nh, D_MODEL), F32)] * 2,
        scratch_shapes=[pltpu.VMEM((tm, D_MODEL), BF16), pltpu.VMEM((tm, D_MODEL), F32)],
        compiler_params=_cparams(("arbitrary", "arbitrary")),
        name="ffn",
    )(xs[0], xs[1], gmods, g_pre.reshape(1, D_MODEL), g_post.reshape(1, D_MODEL), w13, w13, w2)


def _inproj_kernel(nhalf, xa_ref, xb_ref, m_ref, gpre_ref, w_ref, ofn_ref, ossd_ref, oqkv_ref):
    x_refs = (xa_ref, xb_ref)

    def body(half):
        shift = m_ref[0, 0:1, :]
        scale = m_ref[0, 1:2, :]
        h = (_rms(x_refs[half][...]) * (gpre_ref[...] * (1.0 + scale)) + shift).astype(BF16)
        u = jnp.dot(h, w_ref[...], preferred_element_type=F32)
        ofn_ref[...] = u[:, :U_FN]
        ossd_ref[...] = u[:, U_FN:U_FN + U_SSD]
        oqkv_ref[...] = u[:, U_FN + U_SSD:]

    _on_half(nhalf, body)


def _inproj(xs, gmods, g_pre, w_in_pad, layer, rows_per_group, tm):
    nh = xs[0].shape[0]
    n = 2 * nh
    nhalf = nh // tm
    tpg = rows_per_group // tm
    first, second = _halves(nhalf)
    return pl.pallas_call(
        functools.partial(_inproj_kernel, nhalf),
        grid=(n // tm,),
        in_specs=[pl.BlockSpec((tm, D_MODEL), first),
                  pl.BlockSpec((tm, D_MODEL), second),
                  pl.BlockSpec((1, 3, D_MODEL), lambda i: (i // tpg, 0, 0)),
                  pl.BlockSpec((1, D_MODEL), lambda i: (0, 0)),
                  pl.BlockSpec((None, D_MODEL, U_TOTAL), lambda i: (layer, 0, 0))],
        out_specs=[pl.BlockSpec((tm, U_FN), lambda i: (i, 0)),
                   pl.BlockSpec((tm, U_SSD), lambda i: (i, 0)),
                   pl.BlockSpec((tm, U_QKV), lambda i: (i, 0))],
        out_shape=[jax.ShapeDtypeStruct((n, U_FN), F32),
                   jax.ShapeDtypeStruct((n, U_SSD), F32),
                   jax.ShapeDtypeStruct((n, U_QKV), F32)],
        compiler_params=_cparams(("arbitrary",)),
        name="inproj",
    )(xs[0], xs[1], gmods, g_pre.reshape(1, D_MODEL), w_in_pad)


def _outproj_kernel(nhalf, xa_ref, xb_ref, m_ref, gpost_ref, fa_ref, sa_ref, aa_ref, fb_ref, sb_ref, ab_ref,
                    wf_ref, ws_ref, wa_ref, oa_ref, ob_ref):
    ins = ((xa_ref, fa_ref, sa_ref, aa_ref, oa_ref), (xb_ref, fb_ref, sb_ref, ab_ref, ob_ref))

    def finish(half):
        x_ref, f_ref, s_ref, a_ref, o_ref = ins[half]
        y = (_bdot(f_ref[...], wf_ref[...]) + _bdot(s_ref[...], ws_ref[...])
             + _bdot(a_ref[...], wa_ref[...]))
        gate = m_ref[0, 0:1, :]
        o_ref[...] = x_ref[...] + _rms(y) * (gpost_ref[...] * gate)

    _on_half(nhalf, finish)


def _outproj(xs, gmods, g_post, mix_ctx, mix_lat, w_fn, w_ssd, w_att, layer, rows_per_group, tm):
    nh = xs[0].shape[0]
    nhalf = nh // tm
    tpg = rows_per_group // tm
    first, second = _halves(nhalf)
    widths = (FN_WIDTH, SSD_PAD, NA_WIDTH)
    return pl.pallas_call(
        functools.partial(_outproj_kernel, nhalf),
        grid=(2 * nhalf,),
        in_specs=[pl.BlockSpec((tm, D_MODEL), first),
                  pl.BlockSpec((tm, D_MODEL), second),
                  pl.BlockSpec((1, 1, D_MODEL), lambda i: (i // tpg, 0, 0)),
                  pl.BlockSpec((1, D_MODEL), lambda i: (0, 0))]
                 + [pl.BlockSpec((tm, w), first) for w in widths]
                 + [pl.BlockSpec((tm, w), second) for w in widths]
                 + [pl.BlockSpec((None, w, D_MODEL), lambda i: (layer, 0, 0)) for w in widths],
        out_specs=[pl.BlockSpec((tm, D_MODEL), first), pl.BlockSpec((tm, D_MODEL), second)],
        out_shape=[jax.ShapeDtypeStruct((nh, D_MODEL), F32)] * 2,
        compiler_params=_cparams(("arbitrary",)),
        name="outproj",
    )(xs[0], xs[1], gmods, g_post.reshape(1, D_MODEL), *mix_ctx, *mix_lat, w_fn, w_ssd, w_att)


def _dft_tables(L):
    k = np.arange(L, dtype=np.int64)
    ang = 2.0 * np.pi * ((k[:, None] * k[None, :]) % L).astype(np.float64) / L
    sc = 1.0 / math.sqrt(L * HEAD_DIM)
    cl = (np.cos(ang) * sc).astype(np.float32)
    sl = (-np.sin(ang) * sc).astype(np.float32)
    m = np.arange(HEAD_DIM, dtype=np.int64)
    a64 = 2.0 * np.pi * ((m[:, None] * m[None, :]) % HEAD_DIM).astype(np.float64) / HEAD_DIM
    eye = np.eye(FN_GROUPS)
    w1 = np.concatenate([np.kron(eye, np.cos(a64)), np.kron(eye, np.sin(a64))], axis=1).astype(np.float32)
    return cl, sl, w1


def _fourier_kernel(u_ref, w1_ref, cl_ref, sl_ref, o_ref, ab_ref):
    rt = pl.program_id(0)
    b = pl.program_id(1)

    @pl.when(rt == 0)
    def _():
        ab_ref[b] = _bdot(u_ref[...], w1_ref[...]).astype(BF16)

    ab = ab_ref[b]
    o_ref[...] = (jnp.dot(cl_ref[...], ab[:, :FN_WIDTH], preferred_element_type=F32)
                  + jnp.dot(sl_ref[...], ab[:, FN_WIDTH:], preferred_element_type=F32))


def _fourier(u_fn, row0, nb, L):
    cl, sl, w1 = _dft_tables(L)
    cl = jnp.asarray(cl).astype(BF16)
    sl = jnp.asarray(sl).astype(BF16)
    w1 = jnp.asarray(w1).astype(BF16)
    tl = min(L, 512)
    nrt = L // tl
    blk0 = row0 // L
    return pl.pallas_call(
        _fourier_kernel,
        grid=(nrt, nb),
        in_specs=[pl.BlockSpec((L, FN_WIDTH), lambda rt, b: (blk0 + jnp.where(rt == 0, b, nb - 1), 0)),
                  pl.BlockSpec((FN_WIDTH, 2 * FN_WIDTH), lambda rt, b: (0, 0)),
                  pl.BlockSpec((tl, L), lambda rt, b: (rt, 0)),
                  pl.BlockSpec((tl, L), lambda rt, b: (rt, 0))],
        out_specs=pl.BlockSpec((tl, FN_WIDTH), lambda rt, b: (b * nrt + rt, 0)),
        out_shape=jax.ShapeDtypeStruct((nb * L, FN_WIDTH), F32),
        scratch_shapes=[pltpu.VMEM((nb, L, 2 * FN_WIDTH), BF16)],
        compiler_params=_cparams(("arbitrary", "arbitrary")),
        name="fourier",
    )(u_fn, w1, cl, sl)


def _head_masks():
    lane = lax.broadcasted_iota(jnp.int32, (1, LANES), 1)
    return lane < HEAD_DIM


def _ctx_attn_kernel(q_ref, k_ref, v_ref, o_ref):
    low = _head_masks()
    for p in range(NA_HEADS // 2):
        lanes = slice(p * LANES, (p + 1) * LANES)
        q = q_ref[:, lanes]
        k = k_ref[:, lanes].astype(BF16)
        v = v_ref[:, lanes].astype(BF16)
        outs = []
        for hh in range(2):
            sel = low if hh == 0 else jnp.logical_not(low)
            qm = jnp.where(sel, q, 0.0)
            s = _bdot_nt(qm, k)
            m = jnp.max(s, axis=-1, keepdims=True)
            e = jnp.exp(s - m)
            inv = 1.0 / jnp.sum(e, axis=-1, keepdims=True)
            outs.append(jnp.dot((e * inv).astype(BF16), v, preferred_element_type=F32))
        o_ref[:, lanes] = jnp.where(low, outs[0], outs[1])


def _ctx_attn(u_qkv, nb, L):
    return pl.pallas_call(
        _ctx_attn_kernel,
        grid=(nb,),
        in_specs=[pl.BlockSpec((L, NA_WIDTH), lambda b: (b, 0)),
                  pl.BlockSpec((L, NA_WIDTH), lambda b: (b, 1)),
                  pl.BlockSpec((L, NA_WIDTH), lambda b: (b, 2))],
        out_specs=pl.BlockSpec((L, NA_WIDTH), lambda b: (b, 0)),
        out_shape=jax.ShapeDtypeStruct((nb * L, NA_WIDTH), F32),
        compiler_params=_cparams(("parallel",)),
        name="ctx_attn",
    )(u_qkv, u_qkv, u_qkv)


NA_QR = 8
NA_QC = 16
NA_KR = 16
NA_KC = 32
NA_TQ = NA_QR * NA_QC
NA_TK = NA_KR * NA_KC


def _na_tile_geometry(rows):
    wr = min(NA_WIN_ROWS, rows)
    n_rb = rows // NA_QR
    n_cb = GRID_W // NA_QC
    rb = [0, 1, n_rb - 1]
    cb = [0, 1, n_cb - 1]
    dr = np.zeros((3, NA_QR, NA_KR), np.int64)
    vr = np.zeros((3, NA_QR, NA_KR), bool)
    for ci, i in enumerate(rb):
        kr0 = int(np.clip(NA_QR * i - NA_WIN_ROWS // 2, 0, rows - NA_KR))
        for rr in range(NA_QR):
            r = NA_QR * i + rr
            rs = int(np.clip(r - wr // 2, 0, rows - wr))
            for kk in range(NA_KR):
                kr = kr0 + kk
                vr[ci, rr, kk] = rs <= kr < rs + wr
                dr[ci, rr, kk] = np.clip(kr - r + NA_WIN_ROWS - 1, 0, 2 * NA_WIN_ROWS - 2)
    dc = np.zeros((3, NA_QC, NA_KC), np.int64)
    vc = np.zeros((3, NA_QC, NA_KC), bool)
    for ci, j in enumerate(cb):
        kc0 = int(np.clip(NA_QC * j - NA_WIN_COLS // 2, 0, GRID_W - NA_KC))
        for cq in range(NA_QC):
            c = NA_QC * j + cq
            cs = int(np.clip(c - NA_WIN_COLS // 2, 0, GRID_W - NA_WIN_COLS))
            for ck in range(NA_KC):
                kc = kc0 + ck
                vc[ci, cq, ck] = cs <= kc < cs + NA_WIN_COLS
                dc[ci, cq, ck] = np.clip(kc - c + NA_WIN_COLS - 1, 0, 2 * NA_WIN_COLS - 2)
    return dr, vr, dc, vc


def _na_bias_inputs(rpb, rows):
    dr, vr, dc, vc = _na_tile_geometry(rows)
    n_dc = 2 * NA_WIN_COLS - 1
    oh_c = (dc[..., None] == np.arange(n_dc)).astype(np.float32)
    oh_j = np.broadcast_to(oh_c[:, :, None], (3, NA_QC, NA_KR, NA_KC, n_dc)).reshape(3, NA_QC, NA_TK, n_dc)
    vq = jnp.einsum("hab,ycjb->haycj", rpb.astype(F32), jnp.asarray(oh_j), precision=lax.Precision.HIGHEST)
    valid_c = np.broadcast_to(vc[:, :, None], (3, NA_QC, NA_KR, NA_KC)).reshape(3, NA_QC, NA_TK)
    vq = jnp.where(jnp.asarray(valid_c)[None, None], vq, NEG_BIG)
    drt = np.where(vr, dr, -1)[..., None]
    drt = np.broadcast_to(drt, (3, NA_QR, NA_KR, NA_KC)).reshape(3, NA_QR, NA_TK).astype(np.int32)
    offsets = [[sorted(set(dr[x, rr][vr[x, rr]].tolist())) for rr in range(NA_QR)] for x in range(3)]
    return vq, jnp.asarray(drt), offsets


def _na_kernel(rows, offsets, q_ref, k_ref, v_ref, kc_ref, vc_ref, vq_ref, drt_ref, o_ref, bias_ref):
    n_rb = rows // NA_QR
    n_cb = GRID_W // NA_QC
    low = _head_masks()
    kctx = kc_ref[0, 0].astype(BF16)
    vctx = vc_ref[0, 0].astype(BF16)

    @pl.when(pl.program_id(1) == 0)
    def _():
        def build(t, carry):
            hh = t // 3
            cc = t % 3
            for rc in range(3):
                for rr in range(NA_QR):
                    drrow = drt_ref[rc, rr:rr + 1, :]
                    acc = jnp.full((NA_QC, NA_TK), NEG_BIG, F32)
                    for a in offsets[rc][rr]:
                        acc = jnp.where(drrow == a, vq_ref[hh, a, cc], acc)
                    bias_ref[hh, rc, cc, rr * NA_QC:(rr + 1) * NA_QC, :] = acc
            return carry

        lax.fori_loop(0, 6, build, 0)

    def row_block(i, carry):
        kr0 = jnp.clip(NA_QR * i - NA_WIN_ROWS // 2, 0, rows - NA_KR)
        rcfg = (i > 0).astype(jnp.int32) + (i == n_rb - 1).astype(jnp.int32)
        for j in range(n_cb):
            kc0 = int(np.clip(NA_QC * j - NA_WIN_COLS // 2, 0, GRID_W - NA_KC))
            ccfg = 0 if j == 0 else (2 if j == n_cb - 1 else 1)
            q_parts = []
            for rr in range(NA_QR):
                start = pl.multiple_of((NA_QR * i + rr) * GRID_W + NA_QC * j, NA_QC)
                q_parts.append(q_ref[pl.ds(start, NA_QC), :])
            q = jnp.concatenate(q_parts, axis=0)
            k_parts, v_parts = [], []
            for kk in range(NA_KR):
                start = pl.multiple_of((kr0 + kk) * GRID_W + kc0, 8)
                k_parts.append(k_ref[pl.ds(start, NA_KC), :])
                v_parts.append(v_ref[pl.ds(start, NA_KC), :])
            k = jnp.concatenate(k_parts, axis=0).astype(BF16)
            v = jnp.concatenate(v_parts, axis=0).astype(BF16)
            outs = []
            for hh in range(2):
                sel = low if hh == 0 else jnp.logical_not(low)
                qm = jnp.where(sel, q, 0.0).astype(BF16)
                s_loc = _bdot_nt(qm, k) + bias_ref[hh, rcfg, ccfg]
                s_ctx = _bdot_nt(qm, kctx)
                m = jnp.maximum(jnp.max(s_loc, axis=-1, keepdims=True),
                                jnp.max(s_ctx, axis=-1, keepdims=True))
                p_loc = jnp.exp(s_loc - m)
                p_ctx = jnp.exp(s_ctx - m)
                l = jnp.sum(p_loc, axis=-1, keepdims=True) + jnp.sum(p_ctx, axis=-1, keepdims=True)
                inv = 1.0 / l
                o = (jnp.dot((p_loc * inv).astype(BF16), v, preferred_element_type=F32)
                     + jnp.dot((p_ctx * inv).astype(BF16), vctx, preferred_element_type=F32))
                outs.append(o)
            o = jnp.where(low, outs[0], outs[1])
            for rr in range(NA_QR):
                start = pl.multiple_of((NA_QR * i + rr) * GRID_W + NA_QC * j, NA_QC)
                o_ref[pl.ds(start, NA_QC), :] = o[rr * NA_QC:(rr + 1) * NA_QC, :]
        return carry

    lax.fori_loop(0, n_rb, row_block, 0)


def _na_attn(u_qkv, row0, nb, L, cache_k, cache_v, layer, rpb):
    npair = NA_HEADS // 2
    rows = L // GRID_W
    blk0 = row0 // L
    lc = cache_k.shape[2]
    vq, drt, offsets = _na_bias_inputs(rpb, rows)
    return pl.pallas_call(
        functools.partial(_na_kernel, rows, offsets),
        grid=(npair, nb),
        in_specs=[pl.BlockSpec((L, LANES), lambda p, b: (blk0 + b, p)),
                  pl.BlockSpec((L, LANES), lambda p, b: (blk0 + b, npair + p)),
                  pl.BlockSpec((L, LANES), lambda p, b: (blk0 + b, 2 * npair + p)),
                  pl.BlockSpec((1, 1, lc, LANES), lambda p, b: (b, layer, 0, p)),
                  pl.BlockSpec((1, 1, lc, LANES), lambda p, b: (b, layer, 0, p)),
                  pl.BlockSpec((2,) + vq.shape[1:], lambda p, b: (p, 0, 0, 0, 0)),
                  pl.BlockSpec(drt.shape, lambda p, b: (0, 0, 0))],
        out_specs=pl.BlockSpec((L, LANES), lambda p, b: (b, p)),
        out_shape=jax.ShapeDtypeStruct((nb * L, NA_WIDTH), F32),
        scratch_shapes=[pltpu.VMEM((2, 3, 3, NA_TQ, NA_TK), F32)],
        compiler_params=_cparams(("arbitrary", "arbitrary")),
        name="na_attn",
    )(u_qkv, u_qkv, u_qkv, cache_k, cache_v, vq, drt)


def _ssd_constants():
    tril = np.tril(np.ones((SSD_CHUNK, SSD_CHUNK), np.float32))
    expand = np.zeros((2, LANES, SSD_PAD), np.float32)
    colb = np.zeros((2, LANES, SSD_HEADS * LANES), np.float32)
    for d in range(2):
        for h in range(SSD_HEADS):
            s = _slot_of_head(h)
            expand[d, SLOTS * d + s, HEAD_DIM * s:HEAD_DIM * (s + 1)] = 1.0
            colb[d, SLOTS * d + s, LANES * h:LANES * (h + 1)] = 1.0
    two = lambda m: np.concatenate([m, m], axis=-2)
    return np.concatenate([tril, tril], axis=1), two(expand), two(colb)


def _rope_tables(L):
    t = np.arange(L)
    rows = (t // GRID_W).astype(np.float64)
    cols = (t % GRID_W).astype(np.float64)
    quarter = SSD_STATE // 4
    inv = ROPE_BASE ** (-np.arange(quarter, dtype=np.float64) / quarter)
    n = np.arange(SSD_STATE)
    pos = np.where(n[None, :] < SSD_STATE // 2, rows[:, None], cols[:, None])
    ang = pos * inv[n % quarter][None, :]
    first = (n % (SSD_STATE // 2)) < quarter
    cos = np.cos(ang)
    sin = np.where(first[None, :], -np.sin(ang), np.sin(ang))
    tile = lambda a: np.concatenate([a] * SSD_NGROUPS, axis=1).astype(np.float32)
    return tile(cos), tile(sin)


def _split2(x):
    hi = x.astype(BF16)
    lo = (x - hi.astype(F32)).astype(BF16)
    return hi, lo


def _ssd_kernel(L, use_rope, use_init, *refs):
    (u_ref, convw_ref, convb_ref, dtb_ref, a_ref, ax_ref, d_ref, nw_ref,
     tril_ref, exp_ref, colb_ref) = refs[:11]
    pos = 11
    if use_rope:
        cos_ref, sin_ref = refs[pos:pos + 2]
        pos += 2
    if use_init:
        s0_refs = refs[pos:pos + 2]
        pos += 2
    y_ref = refs[pos]
    sfin_refs = refs[pos + 1:pos + 3]
    act_ref, dt_ref, s_ref = refs[pos + 3:]

    nc = L // SSD_CHUNK
    C = SSD_CHUNK
    GW = SSD_PAD // SSD_NGROUPS
    XC = SSD_PAD
    CONV_W = SSD_PAD + 2 * LANES
    DTC = XC + CONV_W
    HALO = 8

    lane = lax.broadcasted_iota(jnp.int32, (1, LANES), 1)
    first_q = (lane % (SSD_STATE // 2)) < (SSD_STATE // 4)

    def prep(c, carry):
        r0 = pl.multiple_of(c * C, C)
        main = u_ref[pl.ds(r0, C), XC:XC + CONV_W]
        pstart = pl.multiple_of(jnp.maximum(r0 - HALO, 0), HALO)
        nstart = pl.multiple_of(jnp.minimum(r0 + C, L - HALO), HALO)
        prev = u_ref[pl.ds(pstart, HALO), XC:XC + CONV_W] * (c > 0).astype(F32)
        nxt = u_ref[pl.ds(nstart, HALO), XC:XC + CONV_W] * (c < nc - 1).astype(F32)
        win = jnp.concatenate([prev, main, nxt], axis=0)
        acc = jnp.zeros((C, CONV_W), F32) + convb_ref[...]
        for k in range(SSD_CONV):
            off = HALO + k - SSD_CONV // 2
            acc = acc + win[off:off + C, :] * convw_ref[k:k + 1, :]
        act = _silu(acc)
        act_ref[pl.ds(r0, C), 0:SSD_PAD] = act[:, 0:SSD_PAD]
        for t in range(2):
            bc = act[:, SSD_PAD + t * LANES:SSD_PAD + (t + 1) * LANES]
            if use_rope:
                partner = jnp.where(first_q, pltpu.roll(bc, LANES - SSD_STATE // 4, 1),
                                    pltpu.roll(bc, SSD_STATE // 4, 1))
                bc = bc * cos_ref[pl.ds(r0, C), :] + partner * sin_ref[pl.ds(r0, C), :]
            act_ref[pl.ds(r0, C), SSD_PAD + t * LANES:SSD_PAD + (t + 1) * LANES] = bc
        raw = u_ref[pl.ds(r0, C), DTC:DTC + LANES] + dtb_ref[...]
        dt_ref[pl.ds(r0, C), :] = jnp.maximum(raw, 0.0) + jnp.log1p(jnp.exp(-jnp.abs(raw)))
        y_ref[pl.ds(r0, C), :] = jnp.zeros((C, SSD_PAD), F32)
        return carry

    lax.fori_loop(0, nc, prep, 0)

    if use_init:
        zero = jnp.zeros((HEAD_DIM, SSD_STATE), F32)
        for d in range(2):
            for g in range(SSD_NGROUPS):
                rows_pn = []
                for hh in range(3):
                    blk = s0_refs[d][0, 0, 3 * g + hh]
                    rows_pn.append(jnp.concatenate([blk, zero] if g == 0 else [zero, blk], axis=1))
                rows_pn.append(jnp.zeros((HEAD_DIM, LANES), F32))
                s_ref[d, g] = jnp.concatenate(rows_pn, axis=0).T
    else:
        s_ref[...] = jnp.zeros_like(s_ref)

    a_row = a_ref[...]
    li = lax.broadcasted_iota(jnp.int32, (C, C), 0)
    si = lax.broadcasted_iota(jnp.int32, (C, C), 1)
    causal = [li >= si, si >= li]
    low64 = lane < HEAD_DIM
    grp_mask = [low64, jnp.logical_not(low64)]

    def one_direction(d, c):
        r0 = pl.multiple_of(c * C, C)
        x = act_ref[pl.ds(r0, C), 0:SSD_PAD]
        bmat = act_ref[pl.ds(r0, C), SSD_PAD:SSD_PAD + LANES]
        cmat = act_ref[pl.ds(r0, C), SSD_PAD + LANES:SSD_PAD + 2 * LANES]
        dt = dt_ref[pl.ds(r0, C), :]
        dta = dt * a_row
        hi, lo = _split2(dta)
        cs = jnp.dot(tril_ref[...], jnp.concatenate([hi, lo], axis=0),
                     preferred_element_type=F32)
        q = cs if d == 0 else cs - dta
        expand = lambda v: jnp.dot(jnp.concatenate(_split2(v), axis=1), exp_ref[d],
                                   preferred_element_type=F32)
        dt_x = expand(dt)
        cs_x = expand(cs)
        end_x = cs_x[C - 1:C, :]
        if d == 0:
            off_scale = jnp.exp(cs_x)
            w_state = jnp.exp(end_x - cs_x)
        else:
            e_x = cs_x - dt_x * ax_ref[d]
            off_scale = jnp.exp(end_x - e_x)
            w_state = jnp.exp(e_x)
        chunk_decay = jnp.exp(end_x)
        xdt = x * dt_x
        xdt_b = xdt.astype(BF16)
        rhs_state = (xdt * w_state).astype(BF16)
        qcol = jnp.dot(jnp.concatenate(_split2(q), axis=1), colb_ref[d],
                       preferred_element_type=F32)
        q_t = q.T
        b_t = bmat.T.astype(BF16)
        b_b = bmat.astype(BF16)
        for g in range(SSD_NGROUPS):
            cm = jnp.where(grp_mask[g], cmat, 0.0).astype(BF16)
            gmat = lax.dot_general(cm, b_b, (((1,), (1,)), ((), ())), preferred_element_type=F32)
            s_old = s_ref[d, g]
            y_off = (jnp.dot(cm, s_old.astype(BF16), preferred_element_type=F32)
                     * off_scale[:, g * GW:(g + 1) * GW])
            res = []
            for hh in range(3):
                h = 3 * g + hh
                slot = 4 * g + hh
                row = q_t[SLOTS * d + slot:SLOTS * d + slot + 1, :]
                col = qcol[:, h * LANES:(h + 1) * LANES]
                seg = (col - row) if d == 0 else (row - col)
                m = (gmat * jnp.exp(jnp.where(causal[d], seg, NEG_BIG))).astype(BF16)
                pair = xdt_b[:, (slot // 2) * LANES:(slot // 2 + 1) * LANES]
                res.append(jnp.dot(m, pair, preferred_element_type=F32))
            y_g = jnp.concatenate([jnp.where(low64, res[0], res[1]), res[2]], axis=1) + y_off
            y_ref[pl.ds(r0, C), g * GW:(g + 1) * GW] += y_g
            st = jnp.dot(b_t, rhs_state[:, g * GW:(g + 1) * GW], preferred_element_type=F32)
            s_ref[d, g] = s_old * chunk_decay[:, g * GW:(g + 1) * GW] + st

    def scan(i, carry):
        one_direction(0, i)
        one_direction(1, nc - 1 - i)
        return carry

    lax.fori_loop(0, nc, scan, 0)
    for d in range(2):
        for g in range(SSD_NGROUPS):
            s_t = s_ref[d, g].T
            for hh in range(3):
                sfin_refs[d][0, 3 * g + hh] = s_t[hh * HEAD_DIM:(hh + 1) * HEAD_DIM,
                                                  g * SSD_STATE:(g + 1) * SSD_STATE]

    def finish(c, carry):
        r0 = pl.multiple_of(c * C, C)
        y = y_ref[pl.ds(r0, C), :] + act_ref[pl.ds(r0, C), 0:SSD_PAD] * d_ref[...]
        y = y * _silu(u_ref[pl.ds(r0, C), 0:SSD_PAD])
        for g in range(SSD_NGROUPS):
            yg = y[:, g * GW:(g + 1) * GW]
            ms = jnp.sum(yg * yg, axis=-1, keepdims=True) * (1.0 / (SSD_INNER // SSD_NGROUPS))
            y_ref[pl.ds(r0, C), g * GW:(g + 1) * GW] = (yg * lax.rsqrt(ms + EPS)
                                                        * nw_ref[:, g * GW:(g + 1) * GW])
        return carry

    lax.fori_loop(0, nc, finish, 0)


def _ssd(u_ssd, row0, nb, L, prm, layer, use_rope, s0):
    blk0 = row0 // L
    tril2, exp2, colb2 = _ssd_constants()
    consts = [jnp.asarray(tril2, dtype=BF16), jnp.asarray(exp2, dtype=BF16), jnp.asarray(colb2, dtype=BF16)]
    full = lambda a: pl.BlockSpec(a.shape, lambda b, _n=a.ndim: (0,) * _n)
    of_layer = lambda a: pl.BlockSpec((None,) + a.shape[1:], lambda b, _n=a.ndim: (layer,) + (0,) * (_n - 1))
    per_layer = [prm[k] for k in ("conv_w", "conv_b", "dt_bias", "a_row", "a_x", "d_row", "norm_w")]
    args = [u_ssd] + per_layer + consts
    in_specs = ([pl.BlockSpec((L, U_SSD), lambda b: (blk0 + b, 0))] + [of_layer(a) for a in per_layer]
                + [full(a) for a in consts])
    if use_rope:
        cos, sin = _rope_tables(L)
        tabs = [jnp.asarray(cos), jnp.asarray(sin)]
        args += tabs
        in_specs += [full(a) for a in tabs]
    state_block = (SSD_HEADS, HEAD_DIM, SSD_STATE)
    if s0 is not None:
        args += list(s0)
        in_specs += [pl.BlockSpec((1, 1) + state_block, lambda b: (b, layer, 0, 0, 0))] * 2
    sshape = (2, SSD_NGROUPS, LANES, SSD_PAD // SSD_NGROUPS)
    return pl.pallas_call(
        functools.partial(_ssd_kernel, L, use_rope, s0 is not None),
        grid=(nb,),
        in_specs=in_specs,
        out_specs=[pl.BlockSpec((L, SSD_PAD), lambda b: (b, 0))]
                  + [pl.BlockSpec((1,) + state_block, lambda b: (b, 0, 0, 0))] * 2,
        out_shape=[jax.ShapeDtypeStruct((nb * L, SSD_PAD), F32)]
                  + [jax.ShapeDtypeStruct((nb,) + state_block, F32)] * 2,
        scratch_shapes=[pltpu.VMEM((L, SSD_PAD + 2 * LANES), F32),
                        pltpu.VMEM((L, LANES), F32),
                        pltpu.VMEM(sshape, F32)],
        compiler_params=_cparams(("parallel",)),
        name="ssd",
    )(*args)


def _pad_heads(a, axis=-1):
    a = jnp.moveaxis(a, axis, -1)
    lead = a.shape[:-1]
    a = a.reshape(lead + (SSD_NGROUPS, 3, HEAD_DIM))
    a = jnp.pad(a, [(0, 0)] * len(lead) + [(0, 0), (0, 1), (0, 0)])
    return jnp.moveaxis(a.reshape(lead + (SSD_PAD,)), -1, axis)


def _pad_dt_lanes(a):
    lead = a.shape[:-2]
    a = a.reshape(lead + (2, SSD_NGROUPS, 3))
    a = jnp.pad(a, [(0, 0)] * len(lead) + [(0, 0), (0, 0), (0, 1)]).reshape(lead + (2 * SLOTS,))
    return jnp.pad(a, [(0, 0)] * len(lead) + [(0, LANES - 2 * SLOTS)])


def _mixer_params(w_in, w_out, ssd_conv_w, ssd_conv_b, ssd_dt_bias, ssd_a_log, ssd_d, ssd_norm):
    o = FN_WIDTH
    z = _pad_heads(w_in[..., o:o + SSD_INNER])
    o += SSD_INNER
    xw = _pad_heads(w_in[..., o:o + SSD_INNER])
    bc = w_in[..., o + SSD_INNER:o + SSD_CONV_DIM]
    o += SSD_CONV_DIM
    dtw = _pad_dt_lanes(w_in[..., o:o + 2 * SSD_HEADS].reshape(DEPTH, D_MODEL, 2, SSD_HEADS))
    o += 2 * SSD_HEADS
    qw = w_in[..., o:o + NA_WIDTH] * (HEAD_DIM ** -0.5)
    kv = w_in[..., o + NA_WIDTH:]
    w_in_pad = jnp.concatenate([w_in[..., :FN_WIDTH], z, xw, bc, dtw, qw, kv], axis=-1).astype(BF16)
    a = -jnp.exp(ssd_a_log.astype(F32))
    a_x = jnp.repeat(_pad_dt_lanes(a)[:, :2 * SLOTS].reshape(DEPTH, 2, SLOTS), HEAD_DIM, axis=-1)
    ssd = {
        "conv_w": jnp.concatenate([_pad_heads(ssd_conv_w[..., :SSD_INNER]), ssd_conv_w[..., SSD_INNER:]], axis=-1),
        "conv_b": jnp.concatenate([_pad_heads(ssd_conv_b[..., :SSD_INNER]),
                                   ssd_conv_b[..., SSD_INNER:]], axis=-1)[:, None, :],
        "dt_bias": _pad_dt_lanes(ssd_dt_bias)[:, None, :],
        "a_row": _pad_dt_lanes(a)[:, None, :],
        "a_x": a_x.reshape(DEPTH, 2, 1, SSD_PAD),
        "d_row": _pad_heads(jnp.repeat(ssd_d, HEAD_DIM, axis=-1))[:, None, :],
        "norm_w": _pad_heads(ssd_norm)[:, None, :],
    }
    return {
        "w_in": w_in_pad,
        "w_out_fn": w_out[:, :FN_WIDTH].astype(BF16),
        "w_out_ssd": _pad_heads(w_out[:, FN_WIDTH:FN_WIDTH + SSD_INNER], axis=1).astype(BF16),
        "w_out_att": w_out[:, FN_WIDTH + SSD_INNER:].astype(BF16),
        "ssd": ssd,
    }


def _pick_tile(rows, want):
    t = min(rows, want)
    while rows % t:
        t //= 2
    return t


def kernel(x_prompt, x_sample, c, state_ssd_fwd, state_ssd_bwd, cache_attn_k, cache_attn_v, c_ctx, mod_w, mod_b, norm_pre, norm_post, ffn_w13, ffn_w2, w_in, w_out, ssd_conv_w, ssd_conv_b, ssd_dt_bias, ssd_a_log, ssd_d, ssd_norm, na_rpb):
    nbp, lp, _ = x_prompt.shape
    nbs, ls, _ = x_sample.shape
    n_ctx, n_lat = nbp * lp, nbs * ls
    assert n_ctx == n_lat and n_ctx % ls == 0 and nbs + 1 <= 8
    rpg = ls
    xs = (x_prompt.reshape(n_ctx, D_MODEL), x_sample.reshape(n_lat, D_MODEL))

    cvec = jnp.zeros((8, D_MODEL), F32).at[0].set(c_ctx).at[1:1 + nbs].set(c)
    mods = _mods(cvec, mod_w, mod_b).reshape(DEPTH, 8, N_MOD, D_MODEL)
    w13 = ffn_w13.astype(BF16)
    w2 = ffn_w2.astype(BF16)
    p = _mixer_params(w_in, w_out, ssd_conv_w, ssd_conv_b, ssd_dt_bias, ssd_a_log, ssd_d, ssd_norm)
    cache_k = cache_attn_k.reshape(nbs, DEPTH, cache_attn_k.shape[2], NA_WIDTH)
    cache_v = cache_attn_v.reshape(nbs, DEPTH, cache_attn_v.shape[2], NA_WIDTH)
    s0 = (state_ssd_fwd.astype(F32), state_ssd_bwd.astype(F32))
    tm_ffn = _pick_tile(rpg, 1024)
    tm_proj = _pick_tile(rpg, 512)

    new_sf, new_sb, new_k, new_v = [], [], [], []
    for l in range(DEPTH):
        gm = jnp.concatenate([jnp.broadcast_to(mods[l, 0], (n_ctx // rpg, N_MOD, D_MODEL)),
                              mods[l, 1:1 + nbs]], axis=0)
        xs = _ffn(xs, gm[:, 0:3], norm_pre[l, 0], norm_post[l, 0], w13, w2, l, 0, rpg, tm_ffn)
        u_fn, u_ssd, u_qkv = _inproj(xs, gm[:, 3:6], norm_pre[l, 1], p["w_in"], l, rpg, tm_proj)

        y_ssd_c, sf_c, sb_c = _ssd(u_ssd, 0, nbp, lp, p["ssd"], l, False, None)
        mix_ctx = (_fourier(u_fn, 0, nbp, lp), y_ssd_c, _ctx_attn(u_qkv, nbp, lp))
        new_sf.append(sf_c)
        new_sb.append(sb_c)
        new_k.append(u_qkv[:n_ctx, NA_WIDTH:2 * NA_WIDTH].reshape(nbp, lp, NA_HEADS, HEAD_DIM))
        new_v.append(u_qkv[:n_ctx, 2 * NA_WIDTH:].reshape(nbp, lp, NA_HEADS, HEAD_DIM))

        y_ssd_l, _, _ = _ssd(u_ssd, n_ctx, nbs, ls, p["ssd"], l, True, s0)
        mix_lat = (_fourier(u_fn, n_ctx, nbs, ls), y_ssd_l,
                   _na_attn(u_qkv, n_ctx, nbs, ls, cache_k, cache_v, l, na_rpb[l]))

        xs = _outproj(xs, gm[:, 5:6], norm_post[l, 1], mix_ctx, mix_lat,
                      p["w_out_fn"], p["w_out_ssd"], p["w_out_att"], l, rpg, tm_proj)
        xs = _ffn(xs, gm[:, 6:9], norm_pre[l, 2], norm_post[l, 2], w13, w2, l, 1, rpg, tm_ffn)

    return (xs[0].reshape(nbp, lp, D_MODEL), xs[1].reshape(nbs, ls, D_MODEL),
            jnp.stack(new_sf, axis=1), jnp.stack(new_sb, axis=1),
            jnp.stack(new_k, axis=1), jnp.stack(new_v, axis=1))
```

```python
import functools
import math

import numpy as np
import jax
import jax.numpy as jnp
from jax import lax
from jax.experimental import pallas as pl
from jax.experimental.pallas import tpu as pltpu

F32 = jnp.float32
BF16 = jnp.bfloat16

D_MODEL = 1024
DEPTH = 2
GRID_W = 64
FF_HIDDEN = 2816
N_MOD = 9
HEAD_DIM = 64
FN_WIDTH = 256
FN_GROUPS = 4
SSD_INNER = 384
SSD_HEADS = 6
SSD_STATE = 64
SSD_NGROUPS = 2
SSD_CONV = 5
SSD_CHUNK = 128
SSD_CONV_DIM = 640
SSD_IN = 1420
NA_WIDTH = 384
NA_HEADS = 6
NA_WIN_ROWS = 8
NA_WIN_COLS = 16
ROPE_BASE = 10000.0
EPS = 1e-6

LANES = 128
VMEM_LIMIT = 56 * 1024 * 1024

SLOTS = 8
SSD_PAD = SLOTS * HEAD_DIM
U_FN = FN_WIDTH
U_SSD = 2 * SSD_PAD + 2 * LANES + LANES
U_QKV = 3 * NA_WIDTH
U_TOTAL = U_FN + U_SSD + U_QKV
NEG_BIG = -1e30


def _slot_of_head(h):
    return 4 * (h // 3) + (h % 3)


def _cparams(sem):
    return pltpu.CompilerParams(dimension_semantics=sem, vmem_limit_bytes=VMEM_LIMIT)


def _rms(x):
    return x * lax.rsqrt(jnp.mean(x * x, axis=-1, keepdims=True) + EPS)


def _silu(x):
    return x * jax.nn.sigmoid(x)


def _bdot(a, b):
    return jnp.dot(a.astype(BF16), b.astype(BF16), preferred_element_type=F32)


def _bdot_nt(a, b):
    return lax.dot_general(a.astype(BF16), b.astype(BF16), (((1,), (1,)), ((), ())),
                           preferred_element_type=F32)


MOD_TN = 1152


def _mods_kernel(c_ref, w_ref, b_ref, o_ref):
    s = _silu(c_ref[...])
    o_ref[0] = _bdot(s, w_ref[0]) + b_ref[0]


def _mods(cvec, mod_w, mod_b):
    ncol = N_MOD * D_MODEL
    return pl.pallas_call(
        _mods_kernel,
        grid=(DEPTH, ncol // MOD_TN),
        in_specs=[pl.BlockSpec((8, D_MODEL), lambda l, j: (0, 0)),
                  pl.BlockSpec((1, D_MODEL, MOD_TN), lambda l, j: (l, 0, j)),
                  pl.BlockSpec((1, 1, MOD_TN), lambda l, j: (l, 0, j))],
        out_specs=pl.BlockSpec((1, 8, MOD_TN), lambda l, j: (l, 0, j)),
        out_shape=jax.ShapeDtypeStruct((DEPTH, 8, ncol), F32),
        compiler_params=_cparams(("parallel", "parallel")),
        name="mods",
    )(cvec, mod_w, mod_b.reshape(DEPTH, 1, ncol))


FFN_TH = 256


def _halves(nhalf):
    first = lambda i, *_: (jnp.minimum(i, nhalf - 1), 0)
    second = lambda i, *_: (jnp.maximum(i - nhalf, 0), 0)
    return first, second


def _on_half(nhalf, fn):
    i = pl.program_id(0)
    pl.when(i < nhalf)(functools.partial(fn, 0))
    pl.when(i >= nhalf)(functools.partial(fn, 1))


def _ffn_kernel(nhalf, xa_ref, xb_ref, m_ref, gpre_ref, gpost_ref, w13_ref, w2_ref,
                oa_ref, ob_ref, h_ref, acc_ref):
    x_refs = (xa_ref, xb_ref)
    o_refs = (oa_ref, ob_ref)

    def prologue(half):
        shift = m_ref[0, 0:1, :]
        scale = m_ref[0, 1:2, :]
        h = _rms(x_refs[half][...]) * (gpre_ref[...] * (1.0 + scale)) + shift
        h_ref[...] = h.astype(BF16)

    _on_half(nhalf, prologue)

    h = h_ref[...]
    for j in range(FF_HIDDEN // FFN_TH):
        cols = slice(j * FFN_TH, (j + 1) * FFN_TH)
        g = jnp.dot(h, w13_ref[:, cols], preferred_element_type=F32)
        u = jnp.dot(h, w13_ref[:, FF_HIDDEN + j * FFN_TH:FF_HIDDEN + (j + 1) * FFN_TH],
                    preferred_element_type=F32)
        a = (_silu(g) * u).astype(BF16)
        part = jnp.dot(a, w2_ref[cols, :], preferred_element_type=F32)
        if j == 0:
            acc_ref[...] = part
        else:
            acc_ref[...] += part

    def epilogue(half):
        gate = m_ref[0, 2:3, :]
        y = _rms(acc_ref[...]) * (gpost_ref[...] * (0.5 * gate))
        o_refs[half][...] = x_refs[half][...] + y

    _on_half(nhalf, epilogue)


def _ffn(xs, gmods, g_pre, g_post, w13, w2, layer, sub, rows_per_group, tm):
    nh = xs[0].shape[0]
    nhalf = nh // tm
    tpg = rows_per_group // tm
    first, second = _halves(nhalf)
    resident = pl.Buffered(1)
    return pl.pallas_call(
        functools.partial(_ffn_kernel, nhalf),
        grid=(2 * nhalf,),
        in_specs=[pl.BlockSpec((tm, D_MODEL), first),
                  pl.BlockSpec((tm, D_MODEL), second),
                  pl.BlockSpec((1, 3, D_MODEL), lambda i: (i // tpg, 0, 0)),
                  pl.BlockSpec((1, D_MODEL), lambda i: (0, 0)),
                  pl.BlockSpec((1, D_MODEL), lambda i: (0, 0)),
                  pl.BlockSpec((None, None, D_MODEL, 2 * FF_HIDDEN), lambda i: (layer, sub, 0, 0),
                               pipeline_mode=resident),
                  pl.BlockSpec((None, None, FF_HIDDEN, D_MODEL), lambda i: (layer, sub, 0, 0),
                               pipeline_mode=resident)],
        out_specs=[pl.BlockSpec((tm, D_MODEL), first), pl.BlockSpec((tm, D_MODEL), second)],
        out_shape=[jax.ShapeDtypeStruct((nh, D_MODEL), F32)] * 2,
        scratch_shapes=[pltpu.VMEM((tm, D_MODEL), BF16), pltpu.VMEM((tm, D_MODEL), F32)],
        compiler_params=_cparams(("arbitrary",)),
        name="ffn",
    )(xs[0], xs[1], gmods, g_pre.reshape(1, D_MODEL), g_post.reshape(1, D_MODEL), w13, w2)


def _inproj_kernel(nhalf, xa_ref, xb_ref, m_ref, gpre_ref, w_ref, ofn_ref, ossd_ref, oqkv_ref):
    x_refs = (xa_ref, xb_ref)

    def body(half):
        shift = m_ref[0, 0:1, :]
        scale = m_ref[0, 1:2, :]
        h = (_rms(x_refs[half][...]) * (gpre_ref[...] * (1.0 + scale)) + shift).astype(BF16)
        u = jnp.dot(h, w_ref[...], preferred_element_type=F32)
        ofn_ref[...] = u[:, :U_FN]
        ossd_ref[...] = u[:, U_FN:U_FN + U_SSD]
        oqkv_ref[...] = u[:, U_FN + U_SSD:]

    _on_half(nhalf, body)


def _inproj(xs, gmods, g_pre, w_in_pad, layer, rows_per_group, tm):
    nh = xs[0].shape[0]
    n = 2 * nh
    nhalf = nh // tm
    tpg = rows_per_group // tm
    first, second = _halves(nhalf)
    return pl.pallas_call(
        functools.partial(_inproj_kernel, nhalf),
        grid=(n // tm,),
        in_specs=[pl.BlockSpec((tm, D_MODEL), first),
                  pl.BlockSpec((tm, D_MODEL), second),
                  pl.BlockSpec((1, 3, D_MODEL), lambda i: (i // tpg, 0, 0)),
                  pl.BlockSpec((1, D_MODEL), lambda i: (0, 0)),
                  pl.BlockSpec((None, D_MODEL, U_TOTAL), lambda i: (layer, 0, 0))],
        out_specs=[pl.BlockSpec((tm, U_FN), lambda i: (i, 0)),
                   pl.BlockSpec((tm, U_SSD), lambda i: (i, 0)),
                   pl.BlockSpec((tm, U_QKV), lambda i: (i, 0))],
        out_shape=[jax.ShapeDtypeStruct((n, U_FN), F32),
                   jax.ShapeDtypeStruct((n, U_SSD), F32),
                   jax.ShapeDtypeStruct((n, U_QKV), F32)],
        compiler_params=_cparams(("arbitrary",)),
        name="inproj",
    )(xs[0], xs[1], gmods, g_pre.reshape(1, D_MODEL), w_in_pad)


def _outproj_kernel(nhalf, xa_ref, xb_ref, m_ref, gpost_ref, fa_ref, sa_ref, aa_ref, fb_ref, sb_ref, ab_ref,
                    wf_ref, ws_ref, wa_ref, oa_ref, ob_ref):
    ins = ((xa_ref, fa_ref, sa_ref, aa_ref, oa_ref), (xb_ref, fb_ref, sb_ref, ab_ref, ob_ref))

    def finish(half):
        x_ref, f_ref, s_ref, a_ref, o_ref = ins[half]
        y = (_bdot(f_ref[...], wf_ref[...]) + _bdot(s_ref[...], ws_ref[...])
             + _bdot(a_ref[...], wa_ref[...]))
        gate = m_ref[0, 0:1, :]
        o_ref[...] = x_ref[...] + _rms(y) * (gpost_ref[...] * gate)

    _on_half(nhalf, finish)


def _outproj(xs, gmods, g_post, mix_ctx, mix_lat, w_fn, w_ssd, w_att, layer, rows_per_group, tm):
    nh = xs[0].shape[0]
    nhalf = nh // tm
    tpg = rows_per_group // tm
    first, second = _halves(nhalf)
    widths = (FN_WIDTH, SSD_PAD, NA_WIDTH)
    return pl.pallas_call(
        functools.partial(_outproj_kernel, nhalf),
        grid=(2 * nhalf,),
        in_specs=[pl.BlockSpec((tm, D_MODEL), first),
                  pl.BlockSpec((tm, D_MODEL), second),
                  pl.BlockSpec((1, 1, D_MODEL), lambda i: (i // tpg, 0, 0)),
                  pl.BlockSpec((1, D_MODEL), lambda i: (0, 0))]
                 + [pl.BlockSpec((tm, w), first) for w in widths]
                 + [pl.BlockSpec((tm, w), second) for w in widths]
                 + [pl.BlockSpec((None, w, D_MODEL), lambda i: (layer, 0, 0)) for w in widths],
        out_specs=[pl.BlockSpec((tm, D_MODEL), first), pl.BlockSpec((tm, D_MODEL), second)],
        out_shape=[jax.ShapeDtypeStruct((nh, D_MODEL), F32)] * 2,
        compiler_params=_cparams(("arbitrary",)),
        name="outproj",
    )(xs[0], xs[1], gmods, g_post.reshape(1, D_MODEL), *mix_ctx, *mix_lat, w_fn, w_ssd, w_att)


def _dft_tables(L):
    k = np.arange(L, dtype=np.int64)
    ang = 2.0 * np.pi * ((k[:, None] * k[None, :]) % L).astype(np.float64) / L
    sc = 1.0 / math.sqrt(L * HEAD_DIM)
    cl = (np.cos(ang) * sc).astype(np.float32)
    sl = (-np.sin(ang) * sc).astype(np.float32)
    m = np.arange(HEAD_DIM, dtype=np.int64)
    a64 = 2.0 * np.pi * ((m[:, None] * m[None, :]) % HEAD_DIM).astype(np.float64) / HEAD_DIM
    eye = np.eye(FN_GROUPS)
    w1 = np.concatenate([np.kron(eye, np.cos(a64)), np.kron(eye, np.sin(a64))], axis=1).astype(np.float32)
    return cl, sl, w1


def _fourier_kernel(u_ref, w1_ref, cl_ref, sl_ref, o_ref, ab_ref):
    rt = pl.program_id(0)
    b = pl.program_id(1)

    @pl.when(rt == 0)
    def _():
        ab_ref[b] = _bdot(u_ref[...], w1_ref[...]).astype(BF16)

    ab = ab_ref[b]
    o_ref[...] = (jnp.dot(cl_ref[...], ab[:, :FN_WIDTH], preferred_element_type=F32)
                  + jnp.dot(sl_ref[...], ab[:, FN_WIDTH:], preferred_element_type=F32))


def _fourier(u_fn, row0, nb, L):
    cl, sl, w1 = _dft_tables(L)
    cl = jnp.asarray(cl).astype(BF16)
    sl = jnp.asarray(sl).astype(BF16)
    w1 = jnp.asarray(w1).astype(BF16)
    tl = min(L, 512)
    nrt = L // tl
    blk0 = row0 // L
    return pl.pallas_call(
        _fourier_kernel,
        grid=(nrt, nb),
        in_specs=[pl.BlockSpec((L, FN_WIDTH), lambda rt, b: (blk0 + jnp.where(rt == 0, b, nb - 1), 0)),
                  pl.BlockSpec((FN_WIDTH, 2 * FN_WIDTH), lambda rt, b: (0, 0)),
                  pl.BlockSpec((tl, L), lambda rt, b: (rt, 0)),
                  pl.BlockSpec((tl, L), lambda rt, b: (rt, 0))],
        out_specs=pl.BlockSpec((tl, FN_WIDTH), lambda rt, b: (b * nrt + rt, 0)),
        out_shape=jax.ShapeDtypeStruct((nb * L, FN_WIDTH), F32),
        scratch_shapes=[pltpu.VMEM((nb, L, 2 * FN_WIDTH), BF16)],
        compiler_params=_cparams(("arbitrary", "arbitrary")),
        name="fourier",
    )(u_fn, w1, cl, sl)


def _head_masks():
    lane = lax.broadcasted_iota(jnp.int32, (1, LANES), 1)
    return lane < HEAD_DIM


def _ctx_attn_kernel(q_ref, k_ref, v_ref, o_ref):
    low = _head_masks()
    for p in range(NA_HEADS // 2):
        lanes = slice(p * LANES, (p + 1) * LANES)
        q = q_ref[:, lanes]
        k = k_ref[:, lanes].astype(BF16)
        v = v_ref[:, lanes].astype(BF16)
        outs = []
        for hh in range(2):
            sel = low if hh == 0 else jnp.logical_not(low)
            qm = jnp.where(sel, q, 0.0)
            s = _bdot_nt(qm, k)
            m = jnp.max(s, axis=-1, keepdims=True)
            e = jnp.exp(s - m)
            inv = 1.0 / jnp.sum(e, axis=-1, keepdims=True)
            outs.append(jnp.dot(e.astype(BF16), v, preferred_element_type=F32) * inv)
        o_ref[:, lanes] = jnp.where(low, outs[0], outs[1])


def _ctx_attn(u_qkv, nb, L):
    return pl.pallas_call(
        _ctx_attn_kernel,
        grid=(nb,),
        in_specs=[pl.BlockSpec((L, NA_WIDTH), lambda b: (b, 0)),
                  pl.BlockSpec((L, NA_WIDTH), lambda b: (b, 1)),
                  pl.BlockSpec((L, NA_WIDTH), lambda b: (b, 2))],
        out_specs=pl.BlockSpec((L, NA_WIDTH), lambda b: (b, 0)),
        out_shape=jax.ShapeDtypeStruct((nb * L, NA_WIDTH), F32),
        compiler_params=_cparams(("parallel",)),
        name="ctx_attn",
    )(u_qkv, u_qkv, u_qkv)


NA_QR = 8
NA_QC = 16
NA_KR = 16
NA_KC = 32
NA_TQ = NA_QR * NA_QC
NA_TK = NA_KR * NA_KC


def _na_tile_geometry(rows):
    wr = min(NA_WIN_ROWS, rows)
    n_rb = rows // NA_QR
    n_cb = GRID_W // NA_QC
    rb = [0, 1, n_rb - 1]
    cb = [0, 1, n_cb - 1]
    dr = np.zeros((3, NA_QR, NA_KR), np.int64)
    vr = np.zeros((3, NA_QR, NA_KR), bool)
    for ci, i in enumerate(rb):
        kr0 = int(np.clip(NA_QR * i - NA_WIN_ROWS // 2, 0, rows - NA_KR))
        for rr in range(NA_QR):
            r = NA_QR * i + rr
            rs = int(np.clip(r - wr // 2, 0, rows - wr))
            for kk in range(NA_KR):
                kr = kr0 + kk
                vr[ci, rr, kk] = rs <= kr < rs + wr
                dr[ci, rr, kk] = np.clip(kr - r + NA_WIN_ROWS - 1, 0, 2 * NA_WIN_ROWS - 2)
    dc = np.zeros((3, NA_QC, NA_KC), np.int64)
    vc = np.zeros((3, NA_QC, NA_KC), bool)
    for ci, j in enumerate(cb):
        kc0 = int(np.clip(NA_QC * j - NA_WIN_COLS // 2, 0, GRID_W - NA_KC))
        for cq in range(NA_QC):
            c = NA_QC * j + cq
            cs = int(np.clip(c - NA_WIN_COLS // 2, 0, GRID_W - NA_WIN_COLS))
            for ck in range(NA_KC):
                kc = kc0 + ck
                vc[ci, cq, ck] = cs <= kc < cs + NA_WIN_COLS
                dc[ci, cq, ck] = np.clip(kc - c + NA_WIN_COLS - 1, 0, 2 * NA_WIN_COLS - 2)
    return dr, vr, dc, vc


def _na_bias_inputs(rpb, rows):
    dr, vr, dc, vc = _na_tile_geometry(rows)
    n_dc = 2 * NA_WIN_COLS - 1
    oh_c = (dc[..., None] == np.arange(n_dc)).astype(np.float32)
    oh_j = np.broadcast_to(oh_c[:, :, None], (3, NA_QC, NA_KR, NA_KC, n_dc)).reshape(3, NA_QC, NA_TK, n_dc)
    vq = jnp.einsum("hab,ycjb->haycj", rpb.astype(F32), jnp.asarray(oh_j), precision=lax.Precision.HIGHEST)
    valid_c = np.broadcast_to(vc[:, :, None], (3, NA_QC, NA_KR, NA_KC)).reshape(3, NA_QC, NA_TK)
    vq = jnp.where(jnp.asarray(valid_c)[None, None], vq, NEG_BIG)
    drt = np.where(vr, dr, -1)[..., None]
    drt = np.broadcast_to(drt, (3, NA_QR, NA_KR, NA_KC)).reshape(3, NA_QR, NA_TK).astype(np.int32)
    offsets = [[sorted(set(dr[x, rr][vr[x, rr]].tolist())) for rr in range(NA_QR)] for x in range(3)]
    return vq, jnp.asarray(drt), offsets


def _na_kernel(rows, offsets, q_ref, k_ref, v_ref, kc_ref, vc_ref, vq_ref, drt_ref, o_ref, bias_ref):
    n_rb = rows // NA_QR
    n_cb = GRID_W // NA_QC
    low = _head_masks()
    kctx = kc_ref[0, 0].astype(BF16)
    vctx = vc_ref[0, 0].astype(BF16)

    @pl.when(pl.program_id(1) == 0)
    def _():
        def build(t, carry):
            hh = t // 3
            cc = t % 3
            for rc in range(3):
                for rr in range(NA_QR):
                    drrow = drt_ref[rc, rr:rr + 1, :]
                    acc = jnp.full((NA_QC, NA_TK), NEG_BIG, F32)
                    for a in offsets[rc][rr]:
                        acc = jnp.where(drrow == a, vq_ref[hh, a, cc], acc)
                    bias_ref[hh, rc, cc, rr * NA_QC:(rr + 1) * NA_QC, :] = acc
            return carry

        lax.fori_loop(0, 6, build, 0)

    def row_block(i, carry):
        kr0 = jnp.clip(NA_QR * i - NA_WIN_ROWS // 2, 0, rows - NA_KR)
        rcfg = jnp.where(i > 0, 1, 0) + jnp.where(i == n_rb - 1, 1, 0)
        for j in range(n_cb):
            kc0 = int(np.clip(NA_QC * j - NA_WIN_COLS // 2, 0, GRID_W - NA_KC))
            ccfg = 0 if j == 0 else (2 if j == n_cb - 1 else 1)
            q_parts = []
            for rr in range(NA_QR):
                start = pl.multiple_of((NA_QR * i + rr) * GRID_W + NA_QC * j, NA_QC)
                q_parts.append(q_ref[pl.ds(start, NA_QC), :])
            q = jnp.concatenate(q_parts, axis=0)
            k_parts, v_parts = [], []
            for kk in range(NA_KR):
                start = pl.multiple_of((kr0 + kk) * GRID_W + kc0, 8)
                k_parts.append(k_ref[pl.ds(start, NA_KC), :])
                v_parts.append(v_ref[pl.ds(start, NA_KC), :])
            k = jnp.concatenate(k_parts, axis=0).astype(BF16)
            v = jnp.concatenate(v_parts, axis=0).astype(BF16)
            outs = []
            for hh in range(2):
                sel = low if hh == 0 else jnp.logical_not(low)
                qm = jnp.where(sel, q, 0.0).astype(BF16)
                s_loc = _bdot_nt(qm, k) + bias_ref[hh, rcfg, ccfg]
                s_ctx = _bdot_nt(qm, kctx)
                m = jnp.maximum(jnp.max(s_loc, axis=-1, keepdims=True),
                                jnp.max(s_ctx, axis=-1, keepdims=True))
                p_loc = jnp.exp(s_loc - m)
                p_ctx = jnp.exp(s_ctx - m)
                l = jnp.sum(p_loc, axis=-1, keepdims=True) + jnp.sum(p_ctx, axis=-1, keepdims=True)
                o = (jnp.dot(p_loc.astype(BF16), v, preferred_element_type=F32)
                     + jnp.dot(p_ctx.astype(BF16), vctx, preferred_element_type=F32))
                outs.append(o * (1.0 / l))
            o = jnp.where(low, outs[0], outs[1])
            for rr in range(NA_QR):
                start = pl.multiple_of((NA_QR * i + rr) * GRID_W + NA_QC * j, NA_QC)
                o_ref[pl.ds(start, NA_QC), :] = o[rr * NA_QC:(rr + 1) * NA_QC, :]
        return carry

    lax.fori_loop(0, n_rb, row_block, 0)


def _na_attn(u_qkv, row0, nb, L, cache_k, cache_v, layer, rpb):
    npair = NA_HEADS // 2
    rows = L // GRID_W
    blk0 = row0 // L
    lc = cache_k.shape[2]
    vq, drt, offsets = _na_bias_inputs(rpb, rows)
    return pl.pallas_call(
        functools.partial(_na_kernel, rows, offsets),
        grid=(npair, nb),
        in_specs=[pl.BlockSpec((L, LANES), lambda p, b: (blk0 + b, p)),
                  pl.BlockSpec((L, LANES), lambda p, b: (blk0 + b, npair + p)),
                  pl.BlockSpec((L, LANES), lambda p, b: (blk0 + b, 2 * npair + p)),
                  pl.BlockSpec((1, 1, lc, LANES), lambda p, b: (b, layer, 0, p)),
                  pl.BlockSpec((1, 1, lc, LANES), lambda p, b: (b, layer, 0, p)),
                  pl.BlockSpec((2,) + vq.shape[1:], lambda p, b: (p, 0, 0, 0, 0)),
                  pl.BlockSpec(drt.shape, lambda p, b: (0, 0, 0))],
        out_specs=pl.BlockSpec((L, LANES), lambda p, b: (b, p)),
        out_shape=jax.ShapeDtypeStruct((nb * L, NA_WIDTH), F32),
        scratch_shapes=[pltpu.VMEM((2, 3, 3, NA_TQ, NA_TK), F32)],
        compiler_params=_cparams(("arbitrary", "arbitrary")),
        name="na_attn",
    )(u_qkv, u_qkv, u_qkv, cache_k, cache_v, vq, drt)


def _ssd_constants():
    tril = np.tril(np.ones((SSD_CHUNK, SSD_CHUNK), np.float32))
    expand = np.zeros((2, LANES, SSD_PAD), np.float32)
    colb = np.zeros((2, LANES, SSD_HEADS * LANES), np.float32)
    for d in range(2):
        for h in range(SSD_HEADS):
            s = _slot_of_head(h)
            expand[d, SLOTS * d + s, HEAD_DIM * s:HEAD_DIM * (s + 1)] = 1.0
            colb[d, SLOTS * d + s, LANES * h:LANES * (h + 1)] = 1.0
    two = lambda m: np.concatenate([m, m], axis=-2)
    return np.concatenate([tril, tril], axis=1), two(expand), two(colb)


def _rope_tables(L):
    t = np.arange(L)
    rows = (t // GRID_W).astype(np.float64)
    cols = (t % GRID_W).astype(np.float64)
    quarter = SSD_STATE // 4
    inv = ROPE_BASE ** (-np.arange(quarter, dtype=np.float64) / quarter)
    n = np.arange(SSD_STATE)
    pos = np.where(n[None, :] < SSD_STATE // 2, rows[:, None], cols[:, None])
    ang = pos * inv[n % quarter][None, :]
    first = (n % (SSD_STATE // 2)) < quarter
    cos = np.cos(ang)
    sin = np.where(first[None, :], -np.sin(ang), np.sin(ang))
    tile = lambda a: np.concatenate([a] * SSD_NGROUPS, axis=1).astype(np.float32)
    return tile(cos), tile(sin)


def _split2(x):
    hi = x.astype(BF16)
    lo = (x - hi.astype(F32)).astype(BF16)
    return hi, lo


def _ssd_kernel(L, use_rope, use_init, *refs):
    (u_ref, convw_ref, convb_ref, dtb_ref, a_ref, ax_ref, d_ref, nw_ref,
     tril_ref, exp_ref, colb_ref) = refs[:11]
    pos = 11
    if use_rope:
        cos_ref, sin_ref = refs[pos:pos + 2]
        pos += 2
    if use_init:
        s0_refs = refs[pos:pos + 2]
        pos += 2
    y_ref = refs[pos]
    sfin_refs = refs[pos + 1:pos + 3]
    act_ref, dt_ref, s_ref = refs[pos + 3:]

    nc = L // SSD_CHUNK
    C = SSD_CHUNK
    GW = SSD_PAD // SSD_NGROUPS
    XC = SSD_PAD
    CONV_W = SSD_PAD + 2 * LANES
    DTC = XC + CONV_W
    HALO = 8

    lane = lax.broadcasted_iota(jnp.int32, (1, LANES), 1)
    first_q = (lane % (SSD_STATE // 2)) < (SSD_STATE // 4)

    def prep(c, carry):
        r0 = pl.multiple_of(c * C, C)
        main = u_ref[pl.ds(r0, C), XC:XC + CONV_W]
        pstart = pl.multiple_of(jnp.maximum(r0 - HALO, 0), HALO)
        nstart = pl.multiple_of(jnp.minimum(r0 + C, L - HALO), HALO)
        prev = u_ref[pl.ds(pstart, HALO), XC:XC + CONV_W] * jnp.where(c > 0, 1.0, 0.0)
        nxt = u_ref[pl.ds(nstart, HALO), XC:XC + CONV_W] * jnp.where(c < nc - 1, 1.0, 0.0)
        win = jnp.concatenate([prev, main, nxt], axis=0)
        acc = jnp.zeros((C, CONV_W), F32) + convb_ref[...]
        for k in range(SSD_CONV):
            off = HALO + k - SSD_CONV // 2
            acc = acc + win[off:off + C, :] * convw_ref[k:k + 1, :]
        act = _silu(acc)
        act_ref[pl.ds(r0, C), 0:SSD_PAD] = act[:, 0:SSD_PAD]
        for t in range(2):
            bc = act[:, SSD_PAD + t * LANES:SSD_PAD + (t + 1) * LANES]
            if use_rope:
                partner = jnp.where(first_q, pltpu.roll(bc, LANES - SSD_STATE // 4, 1),
                                    pltpu.roll(bc, SSD_STATE // 4, 1))
                bc = bc * cos_ref[pl.ds(r0, C), :] + partner * sin_ref[pl.ds(r0, C), :]
            act_ref[pl.ds(r0, C), SSD_PAD + t * LANES:SSD_PAD + (t + 1) * LANES] = bc
        raw = u_ref[pl.ds(r0, C), DTC:DTC + LANES] + dtb_ref[...]
        dt_ref[pl.ds(r0, C), :] = jnp.maximum(raw, 0.0) + jnp.log1p(jnp.exp(-jnp.abs(raw)))
        y_ref[pl.ds(r0, C), :] = jnp.zeros((C, SSD_PAD), F32)
        return carry

    lax.fori_loop(0, nc, prep, 0)

    if use_init:
        zero = jnp.zeros((HEAD_DIM, SSD_STATE), F32)
        for d in range(2):
            for g in range(SSD_NGROUPS):
                rows_pn = []
                for hh in range(3):
                    blk = s0_refs[d][0, 0, 3 * g + hh]
                    rows_pn.append(jnp.concatenate([blk, zero] if g == 0 else [zero, blk], axis=1))
                rows_pn.append(jnp.zeros((HEAD_DIM, LANES), F32))
                s_ref[d, g] = jnp.concatenate(rows_pn, axis=0).T
    else:
        s_ref[...] = jnp.zeros_like(s_ref)

    a_row = a_ref[...]
    li = lax.broadcasted_iota(jnp.int32, (C, C), 0)
    si = lax.broadcasted_iota(jnp.int32, (C, C), 1)
    causal = [li >= si, si >= li]
    low64 = lane < HEAD_DIM
    grp_mask = [low64, jnp.logical_not(low64)]

    def one_direction(d, c):
        r0 = pl.multiple_of(c * C, C)
        x = act_ref[pl.ds(r0, C), 0:SSD_PAD]
        bmat = act_ref[pl.ds(r0, C), SSD_PAD:SSD_PAD + LANES]
        cmat = act_ref[pl.ds(r0, C), SSD_PAD + LANES:SSD_PAD + 2 * LANES]
        dt = dt_ref[pl.ds(r0, C), :]
        dta = dt * a_row
        hi, lo = _split2(dta)
        cs = jnp.dot(tril_ref[...], jnp.concatenate([hi, lo], axis=0),
                     preferred_element_type=F32)
        q = cs if d == 0 else cs - dta
        expand = lambda v: jnp.dot(jnp.concatenate(_split2(v), axis=1), exp_ref[d],
                                   preferred_element_type=F32)
        dt_x = expand(dt)
        cs_x = expand(cs)
        end_x = cs_x[C - 1:C, :]
        if d == 0:
            off_scale = jnp.exp(cs_x)
            w_state = jnp.exp(end_x - cs_x)
        else:
            e_x = cs_x - dt_x * ax_ref[d]
            off_scale = jnp.exp(end_x - e_x)
            w_state = jnp.exp(e_x)
        chunk_decay = jnp.exp(end_x)
        xdt = x * dt_x
        xdt_b = xdt.astype(BF16)
        rhs_state = (xdt * w_state).astype(BF16)
        qcol = jnp.dot(jnp.concatenate(_split2(q), axis=1), colb_ref[d],
                       preferred_element_type=F32)
        q_t = q.T
        b_t = bmat.T.astype(BF16)
        b_b = bmat.astype(BF16)
        for g in range(SSD_NGROUPS):
            cm = jnp.where(grp_mask[g], cmat, 0.0).astype(BF16)
            gmat = lax.dot_general(cm, b_b, (((1,), (1,)), ((), ())), preferred_element_type=F32)
            s_old = s_ref[d, g]
            y_off = (jnp.dot(cm, s_old.astype(BF16), preferred_element_type=F32)
                     * off_scale[:, g * GW:(g + 1) * GW])
            res = []
            for hh in range(3):
                h = 3 * g + hh
                slot = 4 * g + hh
                row = q_t[SLOTS * d + slot:SLOTS * d + slot + 1, :]
                col = qcol[:, h * LANES:(h + 1) * LANES]
                seg = (col - row) if d == 0 else (row - col)
                m = (gmat * jnp.exp(jnp.where(causal[d], seg, NEG_BIG))).astype(BF16)
                pair = xdt_b[:, (slot // 2) * LANES:(slot // 2 + 1) * LANES]
                res.append(jnp.dot(m, pair, preferred_element_type=F32))
            y_g = jnp.concatenate([jnp.where(low64, res[0], res[1]), res[2]], axis=1) + y_off
            y_ref[pl.ds(r0, C), g * GW:(g + 1) * GW] += y_g
            st = jnp.dot(b_t, rhs_state[:, g * GW:(g + 1) * GW], preferred_element_type=F32)
            s_ref[d, g] = s_old * chunk_decay[:, g * GW:(g + 1) * GW] + st

    def scan(i, carry):
        one_direction(0, i)
        one_direction(1, nc - 1 - i)
        return carry

    lax.fori_loop(0, nc, scan, 0)
    for d in range(2):
        for g in range(SSD_NGROUPS):
            s_t = s_ref[d, g].T
            for hh in range(3):
                sfin_refs[d][0, 3 * g + hh] = s_t[hh * HEAD_DIM:(hh + 1) * HEAD_DIM,
                                                  g * SSD_STATE:(g + 1) * SSD_STATE]

    def finish(c, carry):
        r0 = pl.multiple_of(c * C, C)
        y = y_ref[pl.ds(r0, C), :] + act_ref[pl.ds(r0, C), 0:SSD_PAD] * d_ref[...]
        y = y * _silu(u_ref[pl.ds(r0, C), 0:SSD_PAD])
        for g in range(SSD_NGROUPS):
            yg = y[:, g * GW:(g + 1) * GW]
            ms = jnp.sum(yg * yg, axis=-1, keepdims=True) * (1.0 / (SSD_INNER // SSD_NGROUPS))
            y_ref[pl.ds(r0, C), g * GW:(g + 1) * GW] = (yg * lax.rsqrt(ms + EPS)
                                                        * nw_ref[:, g * GW:(g + 1) * GW])
        return carry

    lax.fori_loop(0, nc, finish, 0)


def _ssd(u_ssd, row0, nb, L, prm, layer, use_rope, s0):
    blk0 = row0 // L
    tril2, exp2, colb2 = _ssd_constants()
    consts = [jnp.asarray(tril2, dtype=BF16), jnp.asarray(exp2, dtype=BF16), jnp.asarray(colb2, dtype=BF16)]
    full = lambda a: pl.BlockSpec(a.shape, lambda b, _n=a.ndim: (0,) * _n)
    of_layer = lambda a: pl.BlockSpec((None,) + a.shape[1:], lambda b, _n=a.ndim: (layer,) + (0,) * (_n - 1))
    per_layer = [prm[k] for k in ("conv_w", "conv_b", "dt_bias", "a_row", "a_x", "d_row", "norm_w")]
    args = [u_ssd] + per_layer + consts
    in_specs = ([pl.BlockSpec((L, U_SSD), lambda b: (blk0 + b, 0))] + [of_layer(a) for a in per_layer]
                + [full(a) for a in consts])
    if use_rope:
        cos, sin = _rope_tables(L)
        tabs = [jnp.asarray(cos), jnp.asarray(sin)]
        args += tabs
        in_specs += [full(a) for a in tabs]
    state_block = (SSD_HEADS, HEAD_DIM, SSD_STATE)
    if s0 is not None:
        args += list(s0)
        in_specs += [pl.BlockSpec((1, 1) + state_block, lambda b: (b, layer, 0, 0, 0))] * 2
    sshape = (2, SSD_NGROUPS, LANES, SSD_PAD // SSD_NGROUPS)
    return pl.pallas_call(
        functools.partial(_ssd_kernel, L, use_rope, s0 is not None),
        grid=(nb,),
        in_specs=in_specs,
        out_specs=[pl.BlockSpec((L, SSD_PAD), lambda b: (b, 0))]
                  + [pl.BlockSpec((1,) + state_block, lambda b: (b, 0, 0, 0))] * 2,
        out_shape=[jax.ShapeDtypeStruct((nb * L, SSD_PAD), F32)]
                  + [jax.ShapeDtypeStruct((nb,) + state_block, F32)] * 2,
        scratch_shapes=[pltpu.VMEM((L, SSD_PAD + 2 * LANES), F32),
                        pltpu.VMEM((L, LANES), F32),
                        pltpu.VMEM(sshape, F32)],
        compiler_params=_cparams(("parallel",)),
        name="ssd",
    )(*args)


def _pad_heads(a, axis=-1):
    a = jnp.moveaxis(a, axis, -1)
    lead = a.shape[:-1]
    a = a.reshape(lead + (SSD_NGROUPS, 3, HEAD_DIM))
    a = jnp.pad(a, [(0, 0)] * len(lead) + [(0, 0), (0, 1), (0, 0)])
    return jnp.moveaxis(a.reshape(lead + (SSD_PAD,)), -1, axis)


def _pad_dt_lanes(a):
    lead = a.shape[:-2]
    a = a.reshape(lead + (2, SSD_NGROUPS, 3))
    a = jnp.pad(a, [(0, 0)] * len(lead) + [(0, 0), (0, 0), (0, 1)]).reshape(lead + (2 * SLOTS,))
    return jnp.pad(a, [(0, 0)] * len(lead) + [(0, LANES - 2 * SLOTS)])


def _mixer_params(w_in, w_out, ssd_conv_w, ssd_conv_b, ssd_dt_bias, ssd_a_log, ssd_d, ssd_norm):
    o = FN_WIDTH
    z = _pad_heads(w_in[..., o:o + SSD_INNER])
    o += SSD_INNER
    xw = _pad_heads(w_in[..., o:o + SSD_INNER])
    bc = w_in[..., o + SSD_INNER:o + SSD_CONV_DIM]
    o += SSD_CONV_DIM
    dtw = _pad_dt_lanes(w_in[..., o:o + 2 * SSD_HEADS].reshape(DEPTH, D_MODEL, 2, SSD_HEADS))
    o += 2 * SSD_HEADS
    qw = w_in[..., o:o + NA_WIDTH] * (HEAD_DIM ** -0.5)
    kv = w_in[..., o + NA_WIDTH:]
    w_in_pad = jnp.concatenate([w_in[..., :FN_WIDTH], z, xw, bc, dtw, qw, kv], axis=-1).astype(BF16)
    a = -jnp.exp(ssd_a_log.astype(F32))
    a_x = jnp.repeat(_pad_dt_lanes(a)[:, :2 * SLOTS].reshape(DEPTH, 2, SLOTS), HEAD_DIM, axis=-1)
    ssd = {
        "conv_w": jnp.concatenate([_pad_heads(ssd_conv_w[..., :SSD_INNER]), ssd_conv_w[..., SSD_INNER:]], axis=-1),
        "conv_b": jnp.concatenate([_pad_heads(ssd_conv_b[..., :SSD_INNER]),
                                   ssd_conv_b[..., SSD_INNER:]], axis=-1)[:, None, :],
        "dt_bias": _pad_dt_lanes(ssd_dt_bias)[:, None, :],
        "a_row": _pad_dt_lanes(a)[:, None, :],
        "a_x": a_x.reshape(DEPTH, 2, 1, SSD_PAD),
        "d_row": _pad_heads(jnp.repeat(ssd_d, HEAD_DIM, axis=-1))[:, None, :],
        "norm_w": _pad_heads(ssd_norm)[:, None, :],
    }
    return {
        "w_in": w_in_pad,
        "w_out_fn": w_out[:, :FN_WIDTH].astype(BF16),
        "w_out_ssd": _pad_heads(w_out[:, FN_WIDTH:FN_WIDTH + SSD_INNER], axis=1).astype(BF16),
        "w_out_att": w_out[:, FN_WIDTH + SSD_INNER:].astype(BF16),
        "ssd": ssd,
    }


def _pick_tile(rows, want):
    t = min(rows, want)
    while rows % t:
        t //= 2
    return t


def kernel(x_prompt, x_sample, c, state_ssd_fwd, state_ssd_bwd, cache_attn_k, cache_attn_v, c_ctx, mod_w, mod_b, norm_pre, norm_post, ffn_w13, ffn_w2, w_in, w_out, ssd_conv_w, ssd_conv_b, ssd_dt_bias, ssd_a_log, ssd_d, ssd_norm, na_rpb):
    nbp, lp, _ = x_prompt.shape
    nbs, ls, _ = x_sample.shape
    n_ctx, n_lat = nbp * lp, nbs * ls
    assert n_ctx == n_lat and n_ctx % ls == 0 and nbs + 1 <= 8
    rpg = ls
    xs = (x_prompt.reshape(n_ctx, D_MODEL), x_sample.reshape(n_lat, D_MODEL))

    cvec = jnp.zeros((8, D_MODEL), F32).at[0].set(c_ctx).at[1:1 + nbs].set(c)
    mods = _mods(cvec, mod_w, mod_b).reshape(DEPTH, 8, N_MOD, D_MODEL)
    w13 = ffn_w13.astype(BF16)
    w2 = ffn_w2.astype(BF16)
    p = _mixer_params(w_in, w_out, ssd_conv_w, ssd_conv_b, ssd_dt_bias, ssd_a_log, ssd_d, ssd_norm)
    cache_k = cache_attn_k.reshape(nbs, DEPTH, cache_attn_k.shape[2], NA_WIDTH)
    cache_v = cache_attn_v.reshape(nbs, DEPTH, cache_attn_v.shape[2], NA_WIDTH)
    s0 = (state_ssd_fwd.astype(F32), state_ssd_bwd.astype(F32))
    tm_ffn = _pick_tile(rpg, 512)
    tm_proj = _pick_tile(rpg, 512)

    new_sf, new_sb, new_k, new_v = [], [], [], []
    for l in range(DEPTH):
        gm = jnp.concatenate([jnp.broadcast_to(mods[l, 0], (n_ctx // rpg, N_MOD, D_MODEL)),
                              mods[l, 1:1 + nbs]], axis=0)
        xs = _ffn(xs, gm[:, 0:3], norm_pre[l, 0], norm_post[l, 0], w13, w2, l, 0, rpg, tm_ffn)
        u_fn, u_ssd, u_qkv = _inproj(xs, gm[:, 3:6], norm_pre[l, 1], p["w_in"], l, rpg, tm_proj)

        y_ssd_c, sf_c, sb_c = _ssd(u_ssd, 0, nbp, lp, p["ssd"], l, False, None)
        mix_ctx = (_fourier(u_fn, 0, nbp, lp), y_ssd_c, _ctx_attn(u_qkv, nbp, lp))
        new_sf.append(sf_c)
        new_sb.append(sb_c)
        new_k.append(u_qkv[:n_ctx, NA_WIDTH:2 * NA_WIDTH].reshape(nbp, lp, NA_HEADS, HEAD_DIM))
        new_v.append(u_qkv[:n_ctx, 2 * NA_WIDTH:].reshape(nbp, lp, NA_HEADS, HEAD_DIM))

        y_ssd_l, _, _ = _ssd(u_ssd, n_ctx, nbs, ls, p["ssd"], l, True, s0)
        mix_lat = (_fourier(u_fn, n_ctx, nbs, ls), y_ssd_l,
                   _na_attn(u_qkv, n_ctx, nbs, ls, cache_k, cache_v, l, na_rpb[l]))

        xs = _outproj(xs, gm[:, 5:6], norm_post[l, 1], mix_ctx, mix_lat,
                      p["w_out_fn"], p["w_out_ssd"], p["w_out_att"], l, rpg, tm_proj)
        xs = _ffn(xs, gm[:, 6:9], norm_pre[l, 2], norm_post[l, 2], w13, w2, l, 1, rpg, tm_ffn)

    return (xs[0].reshape(nbp, lp, D_MODEL), xs[1].reshape(nbs, ls, D_MODEL),
            jnp.stack(new_sf, axis=1), jnp.stack(new_sb, axis=1),
            jnp.stack(new_k, axis=1), jnp.stack(new_v, axis=1))
```

```python
import functools
import math

import numpy as np
import jax
import jax.numpy as jnp
from jax import lax
from jax.experimental import pallas as pl
from jax.experimental.pallas import tpu as pltpu

F32 = jnp.float32
BF16 = jnp.bfloat16

D_MODEL = 1024
DEPTH = 2
GRID_W = 64
FF_HIDDEN = 2816
N_MOD = 9
HEAD_DIM = 64
FN_WIDTH = 256
FN_GROUPS = 4
SSD_INNER = 384
SSD_HEADS = 6
SSD_STATE = 64
SSD_NGROUPS = 2
SSD_CONV = 5
SSD_CHUNK = 128
SSD_CONV_DIM = 640
SSD_IN = 1420
NA_WIDTH = 384
NA_HEADS = 6
NA_WIN_ROWS = 8
NA_WIN_COLS = 16
ROPE_BASE = 10000.0
EPS = 1e-6

LANES = 128
VMEM_LIMIT = 56 * 1024 * 1024

SLOTS = 8
SSD_PAD = SLOTS * HEAD_DIM
U_FN = FN_WIDTH
U_SSD = 2 * SSD_PAD + 2 * LANES + LANES
U_QKV = 3 * NA_WIDTH
U_TOTAL = U_FN + U_SSD + U_QKV
NEG_BIG = -1e30


def _slot_of_head(h):
    return 4 * (h // 3) + (h % 3)


def _cparams(sem):
    return pltpu.CompilerParams(dimension_semantics=sem, vmem_limit_bytes=VMEM_LIMIT)


def _rms(x):
    return x * lax.rsqrt(jnp.mean(x * x, axis=-1, keepdims=True) + EPS)


def _silu(x):
    return x * jax.nn.sigmoid(x)


def _bdot(a, b):
    return jnp.dot(a.astype(BF16), b.astype(BF16), preferred_element_type=F32)


def _bdot_nt(a, b):
    return lax.dot_general(a.astype(BF16), b.astype(BF16), (((1,), (1,)), ((), ())),
                           preferred_element_type=F32)


MOD_TN = 1152


def _mods_kernel(c_ref, w_ref, b_ref, o_ref):
    s = _silu(c_ref[...])
    o_ref[0] = _bdot(s, w_ref[0]) + b_ref[0]


def _mods(cvec, mod_w, mod_b):
    ncol = N_MOD * D_MODEL
    return pl.pallas_call(
        _mods_kernel,
        grid=(DEPTH, ncol // MOD_TN),
        in_specs=[pl.BlockSpec((8, D_MODEL), lambda l, j: (0, 0)),
                  pl.BlockSpec((1, D_MODEL, MOD_TN), lambda l, j: (l, 0, j)),
                  pl.BlockSpec((1, 1, MOD_TN), lambda l, j: (l, 0, j))],
        out_specs=pl.BlockSpec((1, 8, MOD_TN), lambda l, j: (l, 0, j)),
        out_shape=jax.ShapeDtypeStruct((DEPTH, 8, ncol), F32),
        compiler_params=_cparams(("parallel", "parallel")),
        name="mods",
    )(cvec, mod_w, mod_b.reshape(DEPTH, 1, ncol))


FFN_TH = 256


def _halves(nhalf):
    first = lambda i, *_: (jnp.minimum(i, nhalf - 1), 0)
    second = lambda i, *_: (jnp.maximum(i - nhalf, 0), 0)
    return first, second


def _on_half(nhalf, fn):
    i = pl.program_id(0)
    pl.when(i < nhalf)(functools.partial(fn, 0))
    pl.when(i >= nhalf)(functools.partial(fn, 1))


def _ffn_kernel(nhalf, mrow, with_mix, *refs):
    xa_ref, xb_ref, m_ref, gpre_ref, gpost_ref, w13_ref, w2_ref = refs[:7]
    pos = 7
    if with_mix:
        mix_refs = (refs[pos:pos + 3], refs[pos + 3:pos + 6])
        wmix_refs = refs[pos + 6:pos + 9]
        gmix_ref = refs[pos + 9]
        pos += 10
    oa_ref, ob_ref, h_ref, acc_ref = refs[pos:pos + 4]
    x_refs = (xa_ref, xb_ref)
    o_refs = (oa_ref, ob_ref)
    if with_mix:
        x1_ref = refs[pos + 4]

    def prologue(half):
        x = x_refs[half][...]
        if with_mix:
            y = None
            for y_ref, w_ref in zip(mix_refs[half], wmix_refs):
                d = jnp.dot(y_ref[...], w_ref[...], preferred_element_type=F32)
                y = d if y is None else y + d
            x = x + _rms(y) * (gmix_ref[...] * m_ref[0, mrow - 1:mrow, :])
            x1_ref[...] = x
        shift = m_ref[0, mrow:mrow + 1, :]
        scale = m_ref[0, mrow + 1:mrow + 2, :]
        h = _rms(x) * (gpre_ref[...] * (1.0 + scale)) + shift
        h_ref[...] = h.astype(BF16)

    _on_half(nhalf, prologue)

    h = h_ref[...]
    for j in range(FF_HIDDEN // FFN_TH):
        cols = slice(j * FFN_TH, (j + 1) * FFN_TH)
        g = jnp.dot(h, w13_ref[:, cols], preferred_element_type=F32)
        u = jnp.dot(h, w13_ref[:, FF_HIDDEN + j * FFN_TH:FF_HIDDEN + (j + 1) * FFN_TH],
                    preferred_element_type=F32)
        a = (_silu(g) * u).astype(BF16)
        part = jnp.dot(a, w2_ref[cols, :], preferred_element_type=F32)
        if j == 0:
            acc_ref[...] = part
        else:
            acc_ref[...] += part

    def epilogue(half):
        gate = m_ref[0, mrow + 2:mrow + 3, :]
        y = _rms(acc_ref[...]) * (gpost_ref[...] * (0.5 * gate))
        x = x1_ref[...] if with_mix else x_refs[half][...]
        o_refs[half][...] = x + y

    _on_half(nhalf, epilogue)


def _ffn(xs, gmods, mrow, g_pre, g_post, w13, w2, layer, sub, rows_per_group, tm, mix=None):
    nh = xs[0].shape[0]
    nhalf = nh // tm
    tpg = rows_per_group // tm
    first, second = _halves(nhalf)
    resident = pl.Buffered(1)
    row = lambda a: a.reshape(1, D_MODEL)
    args = [xs[0], xs[1], gmods, row(g_pre), row(g_post), w13, w2]
    in_specs = [pl.BlockSpec((tm, D_MODEL), first),
                pl.BlockSpec((tm, D_MODEL), second),
                pl.BlockSpec((1, N_MOD, D_MODEL), lambda i: (i // tpg, 0, 0)),
                pl.BlockSpec((1, D_MODEL), lambda i: (0, 0)),
                pl.BlockSpec((1, D_MODEL), lambda i: (0, 0)),
                pl.BlockSpec((None, None, D_MODEL, 2 * FF_HIDDEN), lambda i: (layer, sub, 0, 0),
                             pipeline_mode=resident),
                pl.BlockSpec((None, None, FF_HIDDEN, D_MODEL), lambda i: (layer, sub, 0, 0),
                             pipeline_mode=resident)]
    scratch = [pltpu.VMEM((tm, D_MODEL), BF16), pltpu.VMEM((tm, D_MODEL), F32)]
    if mix is not None:
        mix_ctx, mix_lat, w_mix, g_post_mix = mix
        args += list(mix_ctx) + list(mix_lat) + list(w_mix) + [row(g_post_mix)]
        in_specs += ([pl.BlockSpec((tm, a.shape[1]), first) for a in mix_ctx]
                     + [pl.BlockSpec((tm, a.shape[1]), second) for a in mix_lat]
                     + [pl.BlockSpec((None,) + w.shape[1:], lambda i: (layer, 0, 0), pipeline_mode=resident)
                        for w in w_mix]
                     + [pl.BlockSpec((1, D_MODEL), lambda i: (0, 0))])
        scratch.append(pltpu.VMEM((tm, D_MODEL), F32))
    return pl.pallas_call(
        functools.partial(_ffn_kernel, nhalf, mrow, mix is not None),
        grid=(2 * nhalf,),
        in_specs=in_specs,
        out_specs=[pl.BlockSpec((tm, D_MODEL), first), pl.BlockSpec((tm, D_MODEL), second)],
        out_shape=[jax.ShapeDtypeStruct((nh, D_MODEL), F32)] * 2,
        scratch_shapes=scratch,
        compiler_params=_cparams(("arbitrary",)),
        name="ffn",
    )(*args)


def _inproj_kernel(nhalf, xa_ref, xb_ref, m_ref, gpre_ref, w_ref, ofn_ref, ossd_ref, oqkv_ref):
    x_refs = (xa_ref, xb_ref)

    def body(half):
        shift = m_ref[0, 0:1, :]
        scale = m_ref[0, 1:2, :]
        h = (_rms(x_refs[half][...]) * (gpre_ref[...] * (1.0 + scale)) + shift).astype(BF16)
        u = jnp.dot(h, w_ref[...], preferred_element_type=F32)
        ofn_ref[...] = u[:, :U_FN].astype(ofn_ref.dtype)
        ossd_ref[...] = u[:, U_FN:U_FN + U_SSD]
        oqkv_ref[...] = u[:, U_FN + U_SSD:]

    _on_half(nhalf, body)


def _inproj(xs, gmods, g_pre, w_in_pad, layer, rows_per_group, tm):
    nh = xs[0].shape[0]
    n = 2 * nh
    nhalf = nh // tm
    tpg = rows_per_group // tm
    first, second = _halves(nhalf)
    return pl.pallas_call(
        functools.partial(_inproj_kernel, nhalf),
        grid=(n // tm,),
        in_specs=[pl.BlockSpec((tm, D_MODEL), first),
                  pl.BlockSpec((tm, D_MODEL), second),
                  pl.BlockSpec((1, 3, D_MODEL), lambda i: (i // tpg, 0, 0)),
                  pl.BlockSpec((1, D_MODEL), lambda i: (0, 0)),
                  pl.BlockSpec((None, D_MODEL, U_TOTAL), lambda i: (layer, 0, 0))],
        out_specs=[pl.BlockSpec((tm, U_FN), lambda i: (i, 0)),
                   pl.BlockSpec((tm, U_SSD), lambda i: (i, 0)),
                   pl.BlockSpec((tm, U_QKV), lambda i: (i, 0))],
        out_shape=[jax.ShapeDtypeStruct((n, U_FN), BF16),
                   jax.ShapeDtypeStruct((n, U_SSD), F32),
                   jax.ShapeDtypeStruct((n, U_QKV), F32)],
        compiler_params=_cparams(("arbitrary",)),
        name="inproj",
    )(xs[0], xs[1], gmods, g_pre.reshape(1, D_MODEL), w_in_pad)


def _dft_tables(L):
    k = np.arange(L, dtype=np.int64)
    ang = 2.0 * np.pi * ((k[:, None] * k[None, :]) % L).astype(np.float64) / L
    sc = 1.0 / math.sqrt(L * HEAD_DIM)
    cl = (np.cos(ang) * sc).astype(np.float32)
    sl = (-np.sin(ang) * sc).astype(np.float32)
    m = np.arange(HEAD_DIM, dtype=np.int64)
    a64 = 2.0 * np.pi * ((m[:, None] * m[None, :]) % HEAD_DIM).astype(np.float64) / HEAD_DIM
    eye = np.eye(FN_GROUPS)
    w1 = np.concatenate([np.kron(eye, np.cos(a64)), np.kron(eye, np.sin(a64))], axis=1).astype(np.float32)
    return cl, sl, w1


def _fourier_kernel(u_ref, w1_ref, cl_ref, sl_ref, o_ref, ab_ref):
    rt = pl.program_id(0)
    b = pl.program_id(1)

    @pl.when(rt == 0)
    def _():
        ab_ref[b] = _bdot(u_ref[...], w1_ref[...]).astype(BF16)

    ab = ab_ref[b]
    o_ref[...] = (jnp.dot(cl_ref[...], ab[:, :FN_WIDTH], preferred_element_type=F32)
                  + jnp.dot(sl_ref[...], ab[:, FN_WIDTH:], preferred_element_type=F32)).astype(o_ref.dtype)


def _fourier(u_fn, row0, nb, L):
    cl, sl, w1 = _dft_tables(L)
    cl = jnp.asarray(cl).astype(BF16)
    sl = jnp.asarray(sl).astype(BF16)
    w1 = jnp.asarray(w1).astype(BF16)
    tl = min(L, 512)
    nrt = L // tl
    blk0 = row0 // L
    return pl.pallas_call(
        _fourier_kernel,
        grid=(nrt, nb),
        in_specs=[pl.BlockSpec((L, FN_WIDTH), lambda rt, b: (blk0 + jnp.where(rt == 0, b, nb - 1), 0)),
                  pl.BlockSpec((FN_WIDTH, 2 * FN_WIDTH), lambda rt, b: (0, 0)),
                  pl.BlockSpec((tl, L), lambda rt, b: (rt, 0)),
                  pl.BlockSpec((tl, L), lambda rt, b: (rt, 0))],
        out_specs=pl.BlockSpec((tl, FN_WIDTH), lambda rt, b: (b * nrt + rt, 0)),
        out_shape=jax.ShapeDtypeStruct((nb * L, FN_WIDTH), BF16),
        scratch_shapes=[pltpu.VMEM((nb, L, 2 * FN_WIDTH), BF16)],
        compiler_params=_cparams(("arbitrary", "arbitrary")),
        name="fourier",
    )(u_fn, w1, cl, sl)


def _head_masks():
    lane = lax.broadcasted_iota(jnp.int32, (1, LANES), 1)
    return lane < HEAD_DIM


def _ctx_attn_kernel(q_ref, k_ref, v_ref, o_ref):
    low = _head_masks()
    for p in range(NA_HEADS // 2):
        lanes = slice(p * LANES, (p + 1) * LANES)
        q = q_ref[:, lanes]
        k = k_ref[:, lanes].astype(BF16)
        v = v_ref[:, lanes].astype(BF16)
        outs = []
        for hh in range(2):
            sel = low if hh == 0 else jnp.logical_not(low)
            qm = jnp.where(sel, q, 0.0)
            s = _bdot_nt(qm, k)
            m = jnp.max(s, axis=-1, keepdims=True)
            e = jnp.exp(s - m)
            inv = 1.0 / jnp.sum(e, axis=-1, keepdims=True)
            outs.append(jnp.dot(e.astype(BF16), v, preferred_element_type=F32) * inv)
        o_ref[:, lanes] = jnp.where(low, outs[0], outs[1]).astype(o_ref.dtype)


def _ctx_attn(u_qkv, nb, L):
    return pl.pallas_call(
        _ctx_attn_kernel,
        grid=(nb,),
        in_specs=[pl.BlockSpec((L, NA_WIDTH), lambda b: (b, 0)),
                  pl.BlockSpec((L, NA_WIDTH), lambda b: (b, 1)),
                  pl.BlockSpec((L, NA_WIDTH), lambda b: (b, 2))],
        out_specs=pl.BlockSpec((L, NA_WIDTH), lambda b: (b, 0)),
        out_shape=jax.ShapeDtypeStruct((nb * L, NA_WIDTH), BF16),
        compiler_params=_cparams(("parallel",)),
        name="ctx_attn",
    )(u_qkv, u_qkv, u_qkv)


NA_QR = 8
NA_QC = 16
NA_KR = 16
NA_KC = 32
NA_TQ = NA_QR * NA_QC
NA_TK = NA_KR * NA_KC


def _na_tile_geometry(rows):
    wr = min(NA_WIN_ROWS, rows)
    n_rb = rows // NA_QR
    n_cb = GRID_W // NA_QC
    rb = [0, 1, n_rb - 1]
    cb = [0, 1, n_cb - 1]
    dr = np.zeros((3, NA_QR, NA_KR), np.int64)
    vr = np.zeros((3, NA_QR, NA_KR), bool)
    for ci, i in enumerate(rb):
        kr0 = int(np.clip(NA_QR * i - NA_WIN_ROWS // 2, 0, rows - NA_KR))
        for rr in range(NA_QR):
            r = NA_QR * i + rr
            rs = int(np.clip(r - wr // 2, 0, rows - wr))
            for kk in range(NA_KR):
                kr = kr0 + kk
                vr[ci, rr, kk] = rs <= kr < rs + wr
                dr[ci, rr, kk] = np.clip(kr - r + NA_WIN_ROWS - 1, 0, 2 * NA_WIN_ROWS - 2)
    dc = np.zeros((3, NA_QC, NA_KC), np.int64)
    vc = np.zeros((3, NA_QC, NA_KC), bool)
    for ci, j in enumerate(cb):
        kc0 = int(np.clip(NA_QC * j - NA_WIN_COLS // 2, 0, GRID_W - NA_KC))
        for cq in range(NA_QC):
            c = NA_QC * j + cq
            cs = int(np.clip(c - NA_WIN_COLS // 2, 0, GRID_W - NA_WIN_COLS))
            for ck in range(NA_KC):
                kc = kc0 + ck
                vc[ci, cq, ck] = cs <= kc < cs + NA_WIN_COLS
                dc[ci, cq, ck] = np.clip(kc - c + NA_WIN_COLS - 1, 0, 2 * NA_WIN_COLS - 2)
    return dr, vr, dc, vc


def _na_bias_inputs(rpb, rows):
    dr, vr, dc, vc = _na_tile_geometry(rows)
    n_dc = 2 * NA_WIN_COLS - 1
    oh_c = (dc[..., None] == np.arange(n_dc)).astype(np.float32)
    oh_j = np.broadcast_to(oh_c[:, :, None], (3, NA_QC, NA_KR, NA_KC, n_dc)).reshape(3, NA_QC, NA_TK, n_dc)
    vq = jnp.einsum("hab,ycjb->haycj", rpb.astype(F32), jnp.asarray(oh_j), precision=lax.Precision.HIGHEST)
    valid_c = np.broadcast_to(vc[:, :, None], (3, NA_QC, NA_KR, NA_KC)).reshape(3, NA_QC, NA_TK)
    vq = jnp.where(jnp.asarray(valid_c)[None, None], vq, NEG_BIG)
    drt = np.where(vr, dr, -1)[..., None]
    drt = np.broadcast_to(drt, (3, NA_QR, NA_KR, NA_KC)).reshape(3, NA_QR, NA_TK).astype(np.int32)
    offsets = [[sorted(set(dr[x, rr][vr[x, rr]].tolist())) for rr in range(NA_QR)] for x in range(3)]
    return vq, jnp.asarray(drt), offsets


def _na_kernel(rows, offsets, q_ref, k_ref, v_ref, kc_ref, vc_ref, vq_ref, drt_ref, o_ref, bias_ref):
    n_rb = rows // NA_QR
    n_cb = GRID_W // NA_QC
    low = _head_masks()
    kctx = kc_ref[0, 0].astype(BF16)
    vctx = vc_ref[0, 0].astype(BF16)

    @pl.when(pl.program_id(1) == 0)
    def _():
        def build(t, carry):
            hh = t // 3
            cc = t % 3
            for rc in range(3):
                for rr in range(NA_QR):
                    drrow = drt_ref[rc, rr:rr + 1, :]
                    acc = jnp.full((NA_QC, NA_TK), NEG_BIG, F32)
                    for a in offsets[rc][rr]:
                        acc = jnp.where(drrow == a, vq_ref[hh, a, cc], acc)
                    bias_ref[hh, rc, cc, rr * NA_QC:(rr + 1) * NA_QC, :] = acc
            return carry

        lax.fori_loop(0, 6, build, 0)

    def row_block(i, carry):
        kr0 = jnp.clip(NA_QR * i - NA_WIN_ROWS // 2, 0, rows - NA_KR)
        rcfg = jnp.where(i > 0, 1, 0) + jnp.where(i == n_rb - 1, 1, 0)
        for j in range(n_cb):
            kc0 = int(np.clip(NA_QC * j - NA_WIN_COLS // 2, 0, GRID_W - NA_KC))
            ccfg = 0 if j == 0 else (2 if j == n_cb - 1 else 1)
            q_parts = []
            for rr in range(NA_QR):
                start = pl.multiple_of((NA_QR * i + rr) * GRID_W + NA_QC * j, NA_QC)
                q_parts.append(q_ref[pl.ds(start, NA_QC), :])
            q = jnp.concatenate(q_parts, axis=0)
            k_parts, v_parts = [], []
            for kk in range(NA_KR):
                start = pl.multiple_of((kr0 + kk) * GRID_W + kc0, 8)
                k_parts.append(k_ref[pl.ds(start, NA_KC), :])
                v_parts.append(v_ref[pl.ds(start, NA_KC), :])
            k = jnp.concatenate(k_parts, axis=0).astype(BF16)
            v = jnp.concatenate(v_parts, axis=0).astype(BF16)
            outs = []
            for hh in range(2):
                sel = low if hh == 0 else jnp.logical_not(low)
                qm = jnp.where(sel, q, 0.0).astype(BF16)
                s_loc = _bdot_nt(qm, k) + bias_ref[hh, rcfg, ccfg]
                s_ctx = _bdot_nt(qm, kctx)
                m = jnp.maximum(jnp.max(s_loc, axis=-1, keepdims=True),
                                jnp.max(s_ctx, axis=-1, keepdims=True))
                p_loc = jnp.exp(s_loc - m)
                p_ctx = jnp.exp(s_ctx - m)
                l = jnp.sum(p_loc, axis=-1, keepdims=True) + jnp.sum(p_ctx, axis=-1, keepdims=True)
                o = (jnp.dot(p_loc.astype(BF16), v, preferred_element_type=F32)
                     + jnp.dot(p_ctx.astype(BF16), vctx, preferred_element_type=F32))
                outs.append(o * (1.0 / l))
            o = jnp.where(low, outs[0], outs[1])
            for rr in range(NA_QR):
                start = pl.multiple_of((NA_QR * i + rr) * GRID_W + NA_QC * j, NA_QC)
                o_ref[pl.ds(start, NA_QC), :] = o[rr * NA_QC:(rr + 1) * NA_QC, :].astype(o_ref.dtype)
        return carry

    lax.fori_loop(0, n_rb, row_block, 0)


def _na_attn(u_qkv, row0, nb, L, cache_k, cache_v, layer, rpb):
    npair = NA_HEADS // 2
    rows = L // GRID_W
    blk0 = row0 // L
    lc = cache_k.shape[2]
    vq, drt, offsets = _na_bias_inputs(rpb, rows)
    return pl.pallas_call(
        functools.partial(_na_kernel, rows, offsets),
        grid=(npair, nb),
        in_specs=[pl.BlockSpec((L, LANES), lambda p, b: (blk0 + b, p)),
                  pl.BlockSpec((L, LANES), lambda p, b: (blk0 + b, npair + p)),
                  pl.BlockSpec((L, LANES), lambda p, b: (blk0 + b, 2 * npair + p)),
                  pl.BlockSpec((1, 1, lc, LANES), lambda p, b: (b, layer, 0, p)),
                  pl.BlockSpec((1, 1, lc, LANES), lambda p, b: (b, layer, 0, p)),
                  pl.BlockSpec((2,) + vq.shape[1:], lambda p, b: (p, 0, 0, 0, 0)),
                  pl.BlockSpec(drt.shape, lambda p, b: (0, 0, 0))],
        out_specs=pl.BlockSpec((L, LANES), lambda p, b: (b, p)),
        out_shape=jax.ShapeDtypeStruct((nb * L, NA_WIDTH), BF16),
        scratch_shapes=[pltpu.VMEM((2, 3, 3, NA_TQ, NA_TK), F32)],
        compiler_params=_cparams(("arbitrary", "arbitrary")),
        name="na_attn",
    )(u_qkv, u_qkv, u_qkv, cache_k, cache_v, vq, drt)


def _ssd_constants():
    tril = np.tril(np.ones((SSD_CHUNK, SSD_CHUNK), np.float32))
    expand = np.zeros((2, LANES, SSD_PAD), np.float32)
    colb = np.zeros((2, LANES, SSD_HEADS * LANES), np.float32)
    for d in range(2):
        for h in range(SSD_HEADS):
            s = _slot_of_head(h)
            expand[d, SLOTS * d + s, HEAD_DIM * s:HEAD_DIM * (s + 1)] = 1.0
            colb[d, SLOTS * d + s, LANES * h:LANES * (h + 1)] = 1.0
    two = lambda m: np.concatenate([m, m], axis=-2)
    return np.concatenate([tril, tril], axis=1), two(expand), two(colb)


def _rope_tables(L):
    t = np.arange(L)
    rows = (t // GRID_W).astype(np.float64)
    cols = (t % GRID_W).astype(np.float64)
    quarter = SSD_STATE // 4
    inv = ROPE_BASE ** (-np.arange(quarter, dtype=np.float64) / quarter)
    n = np.arange(SSD_STATE)
    pos = np.where(n[None, :] < SSD_STATE // 2, rows[:, None], cols[:, None])
    ang = pos * inv[n % quarter][None, :]
    first = (n % (SSD_STATE // 2)) < quarter
    cos = np.cos(ang)
    sin = np.where(first[None, :], -np.sin(ang), np.sin(ang))
    tile = lambda a: np.concatenate([a] * SSD_NGROUPS, axis=1).astype(np.float32)
    return tile(cos), tile(sin)


def _split2(x):
    hi = x.astype(BF16)
    lo = (x - hi.astype(F32)).astype(BF16)
    return hi, lo


def _ssd_kernel(L, use_rope, use_init, *refs):
    (u_ref, convw_ref, convb_ref, dtb_ref, a_ref, ax_ref, d_ref, nw_ref,
     tril_ref, exp_ref, colb_ref) = refs[:11]
    pos = 11
    if use_rope:
        cos_ref, sin_ref = refs[pos:pos + 2]
        pos += 2
    if use_init:
        s0_refs = refs[pos:pos + 2]
        pos += 2
    y_ref = refs[pos]
    sfin_refs = refs[pos + 1:pos + 3]
    act_ref, dt_ref, s_ref, yacc_ref = refs[pos + 3:]

    nc = L // SSD_CHUNK
    C = SSD_CHUNK
    GW = SSD_PAD // SSD_NGROUPS
    XC = SSD_PAD
    CONV_W = SSD_PAD + 2 * LANES
    DTC = XC + CONV_W
    HALO = 8

    lane = lax.broadcasted_iota(jnp.int32, (1, LANES), 1)
    first_q = (lane % (SSD_STATE // 2)) < (SSD_STATE // 4)

    def prep(c, carry):
        r0 = pl.multiple_of(c * C, C)
        main = u_ref[pl.ds(r0, C), XC:XC + CONV_W]
        pstart = pl.multiple_of(jnp.maximum(r0 - HALO, 0), HALO)
        nstart = pl.multiple_of(jnp.minimum(r0 + C, L - HALO), HALO)
        prev = u_ref[pl.ds(pstart, HALO), XC:XC + CONV_W] * jnp.where(c > 0, 1.0, 0.0)
        nxt = u_ref[pl.ds(nstart, HALO), XC:XC + CONV_W] * jnp.where(c < nc - 1, 1.0, 0.0)
        win = jnp.concatenate([prev, main, nxt], axis=0)
        acc = jnp.zeros((C, CONV_W), F32) + convb_ref[...]
        for k in range(SSD_CONV):
            off = HALO + k - SSD_CONV // 2
            acc = acc + win[off:off + C, :] * convw_ref[k:k + 1, :]
        act = _silu(acc)
        act_ref[pl.ds(r0, C), 0:SSD_PAD] = act[:, 0:SSD_PAD]
        for t in range(2):
            bc = act[:, SSD_PAD + t * LANES:SSD_PAD + (t + 1) * LANES]
            if use_rope:
                partner = jnp.where(first_q, pltpu.roll(bc, LANES - SSD_STATE // 4, 1),
                                    pltpu.roll(bc, SSD_STATE // 4, 1))
                bc = bc * cos_ref[pl.ds(r0, C), :] + partner * sin_ref[pl.ds(r0, C), :]
            act_ref[pl.ds(r0, C), SSD_PAD + t * LANES:SSD_PAD + (t + 1) * LANES] = bc
        raw = u_ref[pl.ds(r0, C), DTC:DTC + LANES] + dtb_ref[...]
        dt_ref[pl.ds(r0, C), :] = jnp.maximum(raw, 0.0) + jnp.log1p(jnp.exp(-jnp.abs(raw)))
        yacc_ref[pl.ds(r0, C), :] = jnp.zeros((C, SSD_PAD), F32)
        return carry

    lax.fori_loop(0, nc, prep, 0)

    if use_init:
        zero = jnp.zeros((HEAD_DIM, SSD_STATE), F32)
        for d in range(2):
            for g in range(SSD_NGROUPS):
                rows_pn = []
                for hh in range(3):
                    blk = s0_refs[d][0, 0, 3 * g + hh]
                    rows_pn.append(jnp.concatenate([blk, zero] if g == 0 else [zero, blk], axis=1))
                rows_pn.append(jnp.zeros((HEAD_DIM, LANES), F32))
                s_ref[d, g] = jnp.concatenate(rows_pn, axis=0).T
    else:
        s_ref[...] = jnp.zeros_like(s_ref)

    a_row = a_ref[...]
    li = lax.broadcasted_iota(jnp.int32, (C, C), 0)
    si = lax.broadcasted_iota(jnp.int32, (C, C), 1)
    causal = [li >= si, si >= li]
    low64 = lane < HEAD_DIM
    grp_mask = [low64, jnp.logical_not(low64)]

    def one_direction(d, c):
        r0 = pl.multiple_of(c * C, C)
        x = act_ref[pl.ds(r0, C), 0:SSD_PAD]
        bmat = act_ref[pl.ds(r0, C), SSD_PAD:SSD_PAD + LANES]
        cmat = act_ref[pl.ds(r0, C), SSD_PAD + LANES:SSD_PAD + 2 * LANES]
        dt = dt_ref[pl.ds(r0, C), :]
        dta = dt * a_row
        hi, lo = _split2(dta)
        cs = jnp.dot(tril_ref[...], jnp.concatenate([hi, lo], axis=0),
                     preferred_element_type=F32)
        q = cs if d == 0 else cs - dta
        expand = lambda v: jnp.dot(jnp.concatenate(_split2(v), axis=1), exp_ref[d],
                                   preferred_element_type=F32)
        dt_x = expand(dt)
        cs_x = expand(cs)
        end_x = cs_x[C - 1:C, :]
        if d == 0:
            off_scale = jnp.exp(cs_x)
            w_state = jnp.exp(end_x - cs_x)
        else:
            e_x = cs_x - dt_x * ax_ref[d]
            off_scale = jnp.exp(end_x - e_x)
            w_state = jnp.exp(e_x)
        chunk_decay = jnp.exp(end_x)
        xdt = x * dt_x
        xdt_b = xdt.astype(BF16)
        rhs_state = (xdt * w_state).astype(BF16)
        qcol = jnp.dot(jnp.concatenate(_split2(q), axis=1), colb_ref[d],
                       preferred_element_type=F32)
        q_t = q.T
        b_t = bmat.T.astype(BF16)
        b_b = bmat.astype(BF16)
        for g in range(SSD_NGROUPS):
            cm = jnp.where(grp_mask[g], cmat, 0.0).astype(BF16)
            gmat = lax.dot_general(cm, b_b, (((1,), (1,)), ((), ())), preferred_element_type=F32)
            s_old = s_ref[d, g]
            y_off = (jnp.dot(cm, s_old.astype(BF16), preferred_element_type=F32)
                     * off_scale[:, g * GW:(g + 1) * GW])
            res = []
            for hh in range(3):
                h = 3 * g + hh
                slot = 4 * g + hh
                row = q_t[SLOTS * d + slot:SLOTS * d + slot + 1, :]
                col = qcol[:, h * LANES:(h + 1) * LANES]
                seg = (col - row) if d == 0 else (row - col)
                m = (gmat * jnp.exp(jnp.where(causal[d], seg, NEG_BIG))).astype(BF16)
                pair = xdt_b[:, (slot // 2) * LANES:(slot // 2 + 1) * LANES]
                res.append(jnp.dot(m, pair, preferred_element_type=F32))
            y_g = jnp.concatenate([jnp.where(low64, res[0], res[1]), res[2]], axis=1) + y_off
            yacc_ref[pl.ds(r0, C), g * GW:(g + 1) * GW] += y_g
            st = jnp.dot(b_t, rhs_state[:, g * GW:(g + 1) * GW], preferred_element_type=F32)
            s_ref[d, g] = s_old * chunk_decay[:, g * GW:(g + 1) * GW] + st

    def scan(i, carry):
        one_direction(0, i)
        one_direction(1, nc - 1 - i)
        return carry

    lax.fori_loop(0, nc, scan, 0)
    for d in range(2):
        for g in range(SSD_NGROUPS):
            s_t = s_ref[d, g].T
            for hh in range(3):
                sfin_refs[d][0, 3 * g + hh] = s_t[hh * HEAD_DIM:(hh + 1) * HEAD_DIM,
                                                  g * SSD_STATE:(g + 1) * SSD_STATE]

    def finish(c, carry):
        r0 = pl.multiple_of(c * C, C)
        y = yacc_ref[pl.ds(r0, C), :] + act_ref[pl.ds(r0, C), 0:SSD_PAD] * d_ref[...]
        y = y * _silu(u_ref[pl.ds(r0, C), 0:SSD_PAD])
        for g in range(SSD_NGROUPS):
            yg = y[:, g * GW:(g + 1) * GW]
            ms = jnp.sum(yg * yg, axis=-1, keepdims=True) * (1.0 / (SSD_INNER // SSD_NGROUPS))
            yn = yg * lax.rsqrt(ms + EPS) * nw_ref[:, g * GW:(g + 1) * GW]
            y_ref[pl.ds(r0, C), g * GW:(g + 1) * GW] = yn.astype(y_ref.dtype)
        return carry

    lax.fori_loop(0, nc, finish, 0)


def _ssd(u_ssd, row0, nb, L, prm, layer, use_rope, s0):
    blk0 = row0 // L
    tril2, exp2, colb2 = _ssd_constants()
    consts = [jnp.asarray(tril2, dtype=BF16), jnp.asarray(exp2, dtype=BF16), jnp.asarray(colb2, dtype=BF16)]
    full = lambda a: pl.BlockSpec(a.shape, lambda b, _n=a.ndim: (0,) * _n)
    of_layer = lambda a: pl.BlockSpec((None,) + a.shape[1:], lambda b, _n=a.ndim: (layer,) + (0,) * (_n - 1))
    per_layer = [prm[k] for k in ("conv_w", "conv_b", "dt_bias", "a_row", "a_x", "d_row", "norm_w")]
    args = [u_ssd] + per_layer + consts
    in_specs = ([pl.BlockSpec((L, U_SSD), lambda b: (blk0 + b, 0))] + [of_layer(a) for a in per_layer]
                + [full(a) for a in consts])
    if use_rope:
        cos, sin = _rope_tables(L)
        tabs = [jnp.asarray(cos), jnp.asarray(sin)]
        args += tabs
        in_specs += [full(a) for a in tabs]
    state_block = (SSD_HEADS, HEAD_DIM, SSD_STATE)
    if s0 is not None:
        args += list(s0)
        in_specs += [pl.BlockSpec((1, 1) + state_block, lambda b: (b, layer, 0, 0, 0))] * 2
    sshape = (2, SSD_NGROUPS, LANES, SSD_PAD // SSD_NGROUPS)
    return pl.pallas_call(
        functools.partial(_ssd_kernel, L, use_rope, s0 is not None),
        grid=(nb,),
        in_specs=in_specs,
        out_specs=[pl.BlockSpec((L, SSD_PAD), lambda b: (b, 0))]
                  + [pl.BlockSpec((1,) + state_block, lambda b: (b, 0, 0, 0))] * 2,
        out_shape=[jax.ShapeDtypeStruct((nb * L, SSD_PAD), BF16)]
                  + [jax.ShapeDtypeStruct((nb,) + state_block, F32)] * 2,
        scratch_shapes=[pltpu.VMEM((L, SSD_PAD + 2 * LANES), F32),
                        pltpu.VMEM((L, LANES), F32),
                        pltpu.VMEM(sshape, F32),
                        pltpu.VMEM((L, SSD_PAD), F32)],
        compiler_params=_cparams(("parallel",)),
        name="ssd",
    )(*args)


def _pad_heads(a, axis=-1):
    a = jnp.moveaxis(a, axis, -1)
    lead = a.shape[:-1]
    a = a.reshape(lead + (SSD_NGROUPS, 3, HEAD_DIM))
    a = jnp.pad(a, [(0, 0)] * len(lead) + [(0, 0), (0, 1), (0, 0)])
    return jnp.moveaxis(a.reshape(lead + (SSD_PAD,)), -1, axis)


def _pad_dt_lanes(a):
    lead = a.shape[:-2]
    a = a.reshape(lead + (2, SSD_NGROUPS, 3))
    a = jnp.pad(a, [(0, 0)] * len(lead) + [(0, 0), (0, 0), (0, 1)]).reshape(lead + (2 * SLOTS,))
    return jnp.pad(a, [(0, 0)] * len(lead) + [(0, LANES - 2 * SLOTS)])


def _mixer_params(w_in, w_out, ssd_conv_w, ssd_conv_b, ssd_dt_bias, ssd_a_log, ssd_d, ssd_norm):
    o = FN_WIDTH
    z = _pad_heads(w_in[..., o:o + SSD_INNER])
    o += SSD_INNER
    xw = _pad_heads(w_in[..., o:o + SSD_INNER])
    bc = w_in[..., o + SSD_INNER:o + SSD_CONV_DIM]
    o += SSD_CONV_DIM
    dtw = _pad_dt_lanes(w_in[..., o:o + 2 * SSD_HEADS].reshape(DEPTH, D_MODEL, 2, SSD_HEADS))
    o += 2 * SSD_HEADS
    qw = w_in[..., o:o + NA_WIDTH] * (HEAD_DIM ** -0.5)
    kv = w_in[..., o + NA_WIDTH:]
    w_in_pad = jnp.concatenate([w_in[..., :FN_WIDTH], z, xw, bc, dtw, qw, kv], axis=-1).astype(BF16)
    a = -jnp.exp(ssd_a_log.astype(F32))
    a_x = jnp.repeat(_pad_dt_lanes(a)[:, :2 * SLOTS].reshape(DEPTH, 2, SLOTS), HEAD_DIM, axis=-1)
    ssd = {
        "conv_w": jnp.concatenate([_pad_heads(ssd_conv_w[..., :SSD_INNER]), ssd_conv_w[..., SSD_INNER:]], axis=-1),
        "conv_b": jnp.concatenate([_pad_heads(ssd_conv_b[..., :SSD_INNER]),
                                   ssd_conv_b[..., SSD_INNER:]], axis=-1)[:, None, :],
        "dt_bias": _pad_dt_lanes(ssd_dt_bias)[:, None, :],
        "a_row": _pad_dt_lanes(a)[:, None, :],
        "a_x": a_x.reshape(DEPTH, 2, 1, SSD_PAD),
        "d_row": _pad_heads(jnp.repeat(ssd_d, HEAD_DIM, axis=-1))[:, None, :],
        "norm_w": _pad_heads(ssd_norm)[:, None, :],
    }
    return {
        "w_in": w_in_pad,
        "w_out_fn": w_out[:, :FN_WIDTH].astype(BF16),
        "w_out_ssd": _pad_heads(w_out[:, FN_WIDTH:FN_WIDTH + SSD_INNER], axis=1).astype(BF16),
        "w_out_att": w_out[:, FN_WIDTH + SSD_INNER:].astype(BF16),
        "ssd": ssd,
    }


def _pick_tile(rows, want):
    t = min(rows, want)
    while rows % t:
        t //= 2
    return t


def kernel(x_prompt, x_sample, c, state_ssd_fwd, state_ssd_bwd, cache_attn_k, cache_attn_v, c_ctx, mod_w, mod_b, norm_pre, norm_post, ffn_w13, ffn_w2, w_in, w_out, ssd_conv_w, ssd_conv_b, ssd_dt_bias, ssd_a_log, ssd_d, ssd_norm, na_rpb):
    nbp, lp, _ = x_prompt.shape
    nbs, ls, _ = x_sample.shape
    n_ctx, n_lat = nbp * lp, nbs * ls
    assert n_ctx == n_lat and n_ctx % ls == 0 and nbs + 1 <= 8
    rpg = ls
    xs = (x_prompt.reshape(n_ctx, D_MODEL), x_sample.reshape(n_lat, D_MODEL))

    cvec = jnp.zeros((8, D_MODEL), F32).at[0].set(c_ctx).at[1:1 + nbs].set(c)
    mods = _mods(cvec, mod_w, mod_b).reshape(DEPTH, 8, N_MOD, D_MODEL)
    w13 = ffn_w13.astype(BF16)
    w2 = ffn_w2.astype(BF16)
    p = _mixer_params(w_in, w_out, ssd_conv_w, ssd_conv_b, ssd_dt_bias, ssd_a_log, ssd_d, ssd_norm)
    cache_k = cache_attn_k.reshape(nbs, DEPTH, cache_attn_k.shape[2], NA_WIDTH)
    cache_v = cache_attn_v.reshape(nbs, DEPTH, cache_attn_v.shape[2], NA_WIDTH)
    s0 = (state_ssd_fwd.astype(F32), state_ssd_bwd.astype(F32))
    tm_ffn = _pick_tile(rpg, 512)
    tm_proj = _pick_tile(rpg, 512)

    new_sf, new_sb, new_k, new_v = [], [], [], []
    for l in range(DEPTH):
        gm = jnp.concatenate([jnp.broadcast_to(mods[l, 0], (n_ctx // rpg, N_MOD, D_MODEL)),
                              mods[l, 1:1 + nbs]], axis=0)
        xs = _ffn(xs, gm, 0, norm_pre[l, 0], norm_post[l, 0], w13, w2, l, 0, rpg, tm_ffn)
        u_fn, u_ssd, u_qkv = _inproj(xs, gm[:, 3:6], norm_pre[l, 1], p["w_in"], l, rpg, tm_proj)

        y_ssd_c, sf_c, sb_c = _ssd(u_ssd, 0, nbp, lp, p["ssd"], l, False, None)
        mix_ctx = (_fourier(u_fn, 0, nbp, lp), y_ssd_c, _ctx_attn(u_qkv, nbp, lp))
        new_sf.append(sf_c)
        new_sb.append(sb_c)
        new_k.append(u_qkv[:n_ctx, NA_WIDTH:2 * NA_WIDTH].reshape(nbp, lp, NA_HEADS, HEAD_DIM))
        new_v.append(u_qkv[:n_ctx, 2 * NA_WIDTH:].reshape(nbp, lp, NA_HEADS, HEAD_DIM))

        y_ssd_l, _, _ = _ssd(u_ssd, n_ctx, nbs, ls, p["ssd"], l, True, s0)
        mix_lat = (_fourier(u_fn, n_ctx, nbs, ls), y_ssd_l,
                   _na_attn(u_qkv, n_ctx, nbs, ls, cache_k, cache_v, l, na_rpb[l]))

        mix = (mix_ctx, mix_lat, (p["w_out_fn"], p["w_out_ssd"], p["w_out_att"]), norm_post[l, 1])
        xs = _ffn(xs, gm, 6, norm_pre[l, 2], norm_post[l, 2], w13, w2, l, 1, rpg, tm_ffn, mix=mix)

    return (xs[0].reshape(nbp, lp, D_MODEL), xs[1].reshape(nbs, ls, D_MODEL),
            jnp.stack(new_sf, axis=1), jnp.stack(new_sb, axis=1),
            jnp.stack(new_k, axis=1), jnp.stack(new_v, axis=1))
```

```python
import functools
import math

import numpy as np
import jax
import jax.numpy as jnp
from jax import lax
from jax.experimental import pallas as pl
from jax.experimental.pallas import tpu as pltpu

F32 = jnp.float32
BF16 = jnp.bfloat16

D_MODEL = 1024
DEPTH = 2
GRID_W = 64
FF_HIDDEN = 2816
N_MOD = 9
HEAD_DIM = 64
FN_WIDTH = 256
FN_GROUPS = 4
SSD_INNER = 384
SSD_HEADS = 6
SSD_STATE = 64
SSD_NGROUPS = 2
SSD_CONV = 5
SSD_CHUNK = 128
SSD_CONV_DIM = 640
SSD_IN = 1420
NA_WIDTH = 384
NA_HEADS = 6
NA_WIN_ROWS = 8
NA_WIN_COLS = 16
ROPE_BASE = 10000.0
EPS = 1e-6

LANES = 128
VMEM_LIMIT = 56 * 1024 * 1024

SLOTS = 8
SSD_PAD = SLOTS * HEAD_DIM
U_FN = FN_WIDTH
U_SSD = 2 * SSD_PAD + 2 * LANES + LANES
U_QKV = 3 * NA_WIDTH
U_TOTAL = U_FN + U_SSD + U_QKV
NEG_BIG = -1e30


def _slot_of_head(h):
    return 4 * (h // 3) + (h % 3)


def _cparams(sem):
    return pltpu.CompilerParams(dimension_semantics=sem, vmem_limit_bytes=VMEM_LIMIT)


def _rms(x):
    return x * lax.rsqrt(jnp.mean(x * x, axis=-1, keepdims=True) + EPS)


def _silu(x):
    return x * jax.nn.sigmoid(x)


def _bdot(a, b):
    return jnp.dot(a.astype(BF16), b.astype(BF16), preferred_element_type=F32)


def _bdot_nt(a, b):
    return lax.dot_general(a.astype(BF16), b.astype(BF16), (((1,), (1,)), ((), ())),
                           preferred_element_type=F32)


MOD_TN = 1152


def _mods_kernel(c_ref, w_ref, b_ref, o_ref):
    s = _silu(c_ref[...])
    o_ref[0] = _bdot(s, w_ref[0]) + b_ref[0]


def _mods(cvec, mod_w, mod_b):
    ncol = N_MOD * D_MODEL
    return pl.pallas_call(
        _mods_kernel,
        grid=(DEPTH, ncol // MOD_TN),
        in_specs=[pl.BlockSpec((8, D_MODEL), lambda l, j: (0, 0)),
                  pl.BlockSpec((1, D_MODEL, MOD_TN), lambda l, j: (l, 0, j)),
                  pl.BlockSpec((1, 1, MOD_TN), lambda l, j: (l, 0, j))],
        out_specs=pl.BlockSpec((1, 8, MOD_TN), lambda l, j: (l, 0, j)),
        out_shape=jax.ShapeDtypeStruct((DEPTH, 8, ncol), F32),
        compiler_params=_cparams(("parallel", "parallel")),
        name="mods",
    )(cvec, mod_w, mod_b.reshape(DEPTH, 1, ncol))


FFN_TH = 256


def _halves(nhalf):
    first = lambda i, *_: (jnp.minimum(i, nhalf - 1), 0)
    second = lambda i, *_: (jnp.maximum(i - nhalf, 0), 0)
    return first, second


def _on_half(nhalf, fn):
    i = pl.program_id(0)
    pl.when(i < nhalf)(functools.partial(fn, 0))
    pl.when(i >= nhalf)(functools.partial(fn, 1))


def _ffn_kernel(nhalf, mrow, with_mix, *refs):
    xa_ref, xb_ref, m_ref, gpre_ref, gpost_ref, w13_ref, w2_ref = refs[:7]
    pos = 7
    if with_mix:
        mix_refs = (refs[pos:pos + 3], refs[pos + 3:pos + 6])
        wmix_refs = refs[pos + 6:pos + 9]
        gmix_ref = refs[pos + 9]
        pos += 10
    oa_ref, ob_ref, h_ref, acc_ref = refs[pos:pos + 4]
    x_refs = (xa_ref, xb_ref)
    o_refs = (oa_ref, ob_ref)
    if with_mix:
        x1_ref = refs[pos + 4]

    def prologue(half):
        x = x_refs[half][...]
        if with_mix:
            y = None
            for y_ref, w_ref in zip(mix_refs[half], wmix_refs):
                d = jnp.dot(y_ref[...], w_ref[...], preferred_element_type=F32)
                y = d if y is None else y + d
            x = x + _rms(y) * (gmix_ref[...] * m_ref[0, mrow - 1:mrow, :])
            x1_ref[...] = x
        shift = m_ref[0, mrow:mrow + 1, :]
        scale = m_ref[0, mrow + 1:mrow + 2, :]
        h = _rms(x) * (gpre_ref[...] * (1.0 + scale)) + shift
        h_ref[...] = h.astype(BF16)

    _on_half(nhalf, prologue)

    h = h_ref[...]
    for j in range(FF_HIDDEN // FFN_TH):
        cols = slice(j * FFN_TH, (j + 1) * FFN_TH)
        g = jnp.dot(h, w13_ref[:, cols], preferred_element_type=F32)
        u = jnp.dot(h, w13_ref[:, FF_HIDDEN + j * FFN_TH:FF_HIDDEN + (j + 1) * FFN_TH],
                    preferred_element_type=F32)
        a = (_silu(g) * u).astype(BF16)
        part = jnp.dot(a, w2_ref[cols, :], preferred_element_type=F32)
        if j == 0:
            acc_ref[...] = part
        else:
            acc_ref[...] += part

    def epilogue(half):
        gate = m_ref[0, mrow + 2:mrow + 3, :]
        y = _rms(acc_ref[...]) * (gpost_ref[...] * (0.5 * gate))
        x = x1_ref[...] if with_mix else x_refs[half][...]
        o_refs[half][...] = x + y

    _on_half(nhalf, epilogue)


def _ffn(xs, gmods, mrow, g_pre, g_post, w13, w2, layer, sub, rows_per_group, tm, mix=None):
    nh = xs[0].shape[0]
    nhalf = nh // tm
    tpg = rows_per_group // tm
    first, second = _halves(nhalf)
    resident = pl.Buffered(1)
    row = lambda a: a.reshape(1, D_MODEL)
    args = [xs[0], xs[1], gmods, row(g_pre), row(g_post), w13, w2]
    in_specs = [pl.BlockSpec((tm, D_MODEL), first),
                pl.BlockSpec((tm, D_MODEL), second),
                pl.BlockSpec((1, N_MOD, D_MODEL), lambda i: (i // tpg, 0, 0)),
                pl.BlockSpec((1, D_MODEL), lambda i: (0, 0)),
                pl.BlockSpec((1, D_MODEL), lambda i: (0, 0)),
                pl.BlockSpec((None, None, D_MODEL, 2 * FF_HIDDEN), lambda i: (layer, sub, 0, 0),
                             pipeline_mode=resident),
                pl.BlockSpec((None, None, FF_HIDDEN, D_MODEL), lambda i: (layer, sub, 0, 0),
                             pipeline_mode=resident)]
    scratch = [pltpu.VMEM((tm, D_MODEL), BF16), pltpu.VMEM((tm, D_MODEL), F32)]
    if mix is not None:
        mix_ctx, mix_lat, w_mix, g_post_mix = mix
        args += list(mix_ctx) + list(mix_lat) + list(w_mix) + [row(g_post_mix)]
        in_specs += ([pl.BlockSpec((tm, a.shape[1]), first) for a in mix_ctx]
                     + [pl.BlockSpec((tm, a.shape[1]), second) for a in mix_lat]
                     + [pl.BlockSpec((None,) + w.shape[1:], lambda i: (layer, 0, 0), pipeline_mode=resident)
                        for w in w_mix]
                     + [pl.BlockSpec((1, D_MODEL), lambda i: (0, 0))])
        scratch.append(pltpu.VMEM((tm, D_MODEL), F32))
    return pl.pallas_call(
        functools.partial(_ffn_kernel, nhalf, mrow, mix is not None),
        grid=(2 * nhalf,),
        in_specs=in_specs,
        out_specs=[pl.BlockSpec((tm, D_MODEL), first), pl.BlockSpec((tm, D_MODEL), second)],
        out_shape=[jax.ShapeDtypeStruct((nh, D_MODEL), F32)] * 2,
        scratch_shapes=scratch,
        compiler_params=_cparams(("arbitrary",)),
        name="ffn",
    )(*args)


def _inproj_kernel(nhalf, xa_ref, xb_ref, m_ref, gpre_ref, w_ref, ofn_ref, ossd_ref, oqkv_ref):
    x_refs = (xa_ref, xb_ref)

    def body(half):
        shift = m_ref[0, 0:1, :]
        scale = m_ref[0, 1:2, :]
        h = (_rms(x_refs[half][...]) * (gpre_ref[...] * (1.0 + scale)) + shift).astype(BF16)
        u = jnp.dot(h, w_ref[...], preferred_element_type=F32)
        ofn_ref[...] = u[:, :U_FN].astype(ofn_ref.dtype)
        ossd_ref[...] = u[:, U_FN:U_FN + U_SSD]
        oqkv_ref[...] = u[:, U_FN + U_SSD:]

    _on_half(nhalf, body)


def _inproj(xs, gmods, g_pre, w_in_pad, layer, rows_per_group, tm):
    nh = xs[0].shape[0]
    n = 2 * nh
    nhalf = nh // tm
    tpg = rows_per_group // tm
    first, second = _halves(nhalf)
    return pl.pallas_call(
        functools.partial(_inproj_kernel, nhalf),
        grid=(n // tm,),
        in_specs=[pl.BlockSpec((tm, D_MODEL), first),
                  pl.BlockSpec((tm, D_MODEL), second),
                  pl.BlockSpec((1, 3, D_MODEL), lambda i: (i // tpg, 0, 0)),
                  pl.BlockSpec((1, D_MODEL), lambda i: (0, 0)),
                  pl.BlockSpec((None, D_MODEL, U_TOTAL), lambda i: (layer, 0, 0))],
        out_specs=[pl.BlockSpec((tm, U_FN), lambda i: (i, 0)),
                   pl.BlockSpec((tm, U_SSD), lambda i: (i, 0)),
                   pl.BlockSpec((tm, U_QKV), lambda i: (i, 0))],
        out_shape=[jax.ShapeDtypeStruct((n, U_FN), BF16),
                   jax.ShapeDtypeStruct((n, U_SSD), F32),
                   jax.ShapeDtypeStruct((n, U_QKV), F32)],
        compiler_params=_cparams(("arbitrary",)),
        name="inproj",
    )(xs[0], xs[1], gmods, g_pre.reshape(1, D_MODEL), w_in_pad)


def _dft_tables(L):
    k = np.arange(L, dtype=np.int64)
    ang = 2.0 * np.pi * ((k[:, None] * k[None, :]) % L).astype(np.float64) / L
    sc = 1.0 / math.sqrt(L * HEAD_DIM)
    cl = (np.cos(ang) * sc).astype(np.float32)
    sl = (-np.sin(ang) * sc).astype(np.float32)
    m = np.arange(HEAD_DIM, dtype=np.int64)
    a64 = 2.0 * np.pi * ((m[:, None] * m[None, :]) % HEAD_DIM).astype(np.float64) / HEAD_DIM
    eye = np.eye(FN_GROUPS)
    w1 = np.concatenate([np.kron(eye, np.cos(a64)), np.kron(eye, np.sin(a64))], axis=1).astype(np.float32)
    return cl, sl, w1


def _fourier_kernel(nseq, L, tl, u_ref, w1_ref, cl_ref, sl_ref, o_ref, ab_ref):
    rt = pl.program_id(0)
    g = pl.program_id(1)

    @pl.when(rt == 0)
    def _():
        ab_ref[g] = _bdot(u_ref[...], w1_ref[...]).astype(BF16)

    for s in range(nseq):
        ab = ab_ref[g, s * L:(s + 1) * L, :]
        y = (jnp.dot(cl_ref[...], ab[:, :FN_WIDTH], preferred_element_type=F32)
             + jnp.dot(sl_ref[...], ab[:, FN_WIDTH:], preferred_element_type=F32))
        o_ref[s * tl:(s + 1) * tl, :] = y.astype(o_ref.dtype)


def _fourier(u_fn, row0, nb, L):
    cl, sl, w1 = _dft_tables(L)
    cl = jnp.asarray(cl).astype(BF16)
    sl = jnp.asarray(sl).astype(BF16)
    w1 = jnp.asarray(w1).astype(BF16)
    tl = min(L, 512)
    nrt = L // tl
    nseq = max(1, min(nb, 2048 // L)) if nrt == 1 else 1
    while nb % nseq:
        nseq -= 1
    ng = nb // nseq
    blk0 = row0 // (nseq * L)
    assert row0 % (nseq * L) == 0
    return pl.pallas_call(
        functools.partial(_fourier_kernel, nseq, L, tl),
        grid=(nrt, ng),
        in_specs=[pl.BlockSpec((nseq * L, FN_WIDTH), lambda rt, g: (blk0 + jnp.where(rt == 0, g, ng - 1), 0)),
                  pl.BlockSpec((FN_WIDTH, 2 * FN_WIDTH), lambda rt, g: (0, 0)),
                  pl.BlockSpec((tl, L), lambda rt, g: (rt, 0)),
                  pl.BlockSpec((tl, L), lambda rt, g: (rt, 0))],
        out_specs=pl.BlockSpec((nseq * tl, FN_WIDTH), lambda rt, g: (g * nrt + rt, 0)),
        out_shape=jax.ShapeDtypeStruct((nb * L, FN_WIDTH), BF16),
        scratch_shapes=[pltpu.VMEM((ng, nseq * L, 2 * FN_WIDTH), BF16)],
        compiler_params=_cparams(("arbitrary", "arbitrary")),
        name="fourier",
    )(u_fn, w1, cl, sl)


def _head_masks():
    lane = lax.broadcasted_iota(jnp.int32, (1, LANES), 1)
    return lane < HEAD_DIM


def _ctx_attn_kernel(nseq, L, q_ref, k_ref, v_ref, o_ref):
    low = _head_masks()
    for t in range(nseq):
        rows = slice(t * L, (t + 1) * L)
        for p in range(NA_HEADS // 2):
            lanes = slice(p * LANES, (p + 1) * LANES)
            q = q_ref[rows, lanes]
            k = k_ref[rows, lanes].astype(BF16)
            v = v_ref[rows, lanes].astype(BF16)
            q2 = jnp.concatenate([jnp.where(low, q, 0.0), jnp.where(low, 0.0, q)], axis=0)
            s = _bdot_nt(q2, k)
            m = jnp.max(s, axis=-1, keepdims=True)
            e = jnp.exp(s - m)
            inv = 1.0 / jnp.sum(e, axis=-1, keepdims=True)
            o2 = jnp.dot(e.astype(BF16), v, preferred_element_type=F32) * inv
            o_ref[rows, lanes] = jnp.where(low, o2[:L], o2[L:]).astype(o_ref.dtype)


def _ctx_attn(u_qkv, nb, L):
    nseq = 4 if nb % 4 == 0 else 1
    return pl.pallas_call(
        functools.partial(_ctx_attn_kernel, nseq, L),
        grid=(nb // nseq,),
        in_specs=[pl.BlockSpec((nseq * L, NA_WIDTH), lambda b: (b, 0)),
                  pl.BlockSpec((nseq * L, NA_WIDTH), lambda b: (b, 1)),
                  pl.BlockSpec((nseq * L, NA_WIDTH), lambda b: (b, 2))],
        out_specs=pl.BlockSpec((nseq * L, NA_WIDTH), lambda b: (b, 0)),
        out_shape=jax.ShapeDtypeStruct((nb * L, NA_WIDTH), BF16),
        compiler_params=_cparams(("parallel",)),
        name="ctx_attn",
    )(u_qkv, u_qkv, u_qkv)


NA_QR = 8
NA_QC = 16
NA_KR = 16
NA_KC = 32
NA_TQ = NA_QR * NA_QC
NA_TK = NA_KR * NA_KC


def _na_tile_geometry(rows):
    wr = min(NA_WIN_ROWS, rows)
    n_rb = rows // NA_QR
    n_cb = GRID_W // NA_QC
    rb = [0, 1, n_rb - 1]
    cb = [0, 1, n_cb - 1]
    dr = np.zeros((3, NA_QR, NA_KR), np.int64)
    vr = np.zeros((3, NA_QR, NA_KR), bool)
    for ci, i in enumerate(rb):
        kr0 = int(np.clip(NA_QR * i - NA_WIN_ROWS // 2, 0, rows - NA_KR))
        for rr in range(NA_QR):
            r = NA_QR * i + rr
            rs = int(np.clip(r - wr // 2, 0, rows - wr))
            for kk in range(NA_KR):
                kr = kr0 + kk
                vr[ci, rr, kk] = rs <= kr < rs + wr
                dr[ci, rr, kk] = np.clip(kr - r + NA_WIN_ROWS - 1, 0, 2 * NA_WIN_ROWS - 2)
    dc = np.zeros((3, NA_QC, NA_KC), np.int64)
    vc = np.zeros((3, NA_QC, NA_KC), bool)
    for ci, j in enumerate(cb):
        kc0 = int(np.clip(NA_QC * j - NA_WIN_COLS // 2, 0, GRID_W - NA_KC))
        for cq in range(NA_QC):
            c = NA_QC * j + cq
            cs = int(np.clip(c - NA_WIN_COLS // 2, 0, GRID_W - NA_WIN_COLS))
            for ck in range(NA_KC):
                kc = kc0 + ck
                vc[ci, cq, ck] = cs <= kc < cs + NA_WIN_COLS
                dc[ci, cq, ck] = np.clip(kc - c + NA_WIN_COLS - 1, 0, 2 * NA_WIN_COLS - 2)
    return dr, vr, dc, vc


def _na_bias_inputs(rpb, rows):
    dr, vr, dc, vc = _na_tile_geometry(rows)
    n_dc = 2 * NA_WIN_COLS - 1
    oh_c = (dc[..., None] == np.arange(n_dc)).astype(np.float32)
    oh_j = np.broadcast_to(oh_c[:, :, None], (3, NA_QC, NA_KR, NA_KC, n_dc)).reshape(3, NA_QC, NA_TK, n_dc)
    vq = jnp.einsum("lhab,ycjb->lhaycj", rpb.astype(F32), jnp.asarray(oh_j), precision=lax.Precision.HIGHEST)
    valid_c = np.broadcast_to(vc[:, :, None], (3, NA_QC, NA_KR, NA_KC)).reshape(3, NA_QC, NA_TK)
    vq = jnp.where(jnp.asarray(valid_c)[None, None, None], vq, NEG_BIG)
    drt = np.where(vr, dr, -1)[..., None]
    drt = np.broadcast_to(drt, (3, NA_QR, NA_KR, NA_KC)).reshape(3, NA_QR, NA_TK).astype(np.int32)
    offsets = [[sorted(set(dr[x, rr][vr[x, rr]].tolist())) for rr in range(NA_QR)] for x in range(3)]
    return vq, jnp.asarray(drt), offsets


def _na_kernel(rows, offsets, q_ref, k_ref, v_ref, kc_ref, vc_ref, vq_ref, drt_ref, o_ref, bias_ref):
    n_rb = rows // NA_QR
    n_cb = GRID_W // NA_QC
    low = _head_masks()
    kctx = kc_ref[0, 0].astype(BF16)
    vctx = vc_ref[0, 0].astype(BF16)

    @pl.when(pl.program_id(1) == 0)
    def _():
        def build(t, carry):
            hh = t // 3
            cc = t % 3
            for rc in range(3):
                for rr in range(NA_QR):
                    drrow = drt_ref[rc, rr:rr + 1, :]
                    acc = jnp.full((NA_QC, NA_TK), NEG_BIG, F32)
                    for a in offsets[rc][rr]:
                        acc = jnp.where(drrow == a, vq_ref[hh, a, cc], acc)
                    row0 = pl.multiple_of(hh * NA_TQ + rr * NA_QC, NA_QC)
                    bias_ref[rc, cc, pl.ds(row0, NA_QC), :] = acc
            return carry

        lax.fori_loop(0, 6, build, 0)

    def row_block(i, carry):
        kr0 = jnp.clip(NA_QR * i - NA_WIN_ROWS // 2, 0, rows - NA_KR)
        rcfg = jnp.where(i > 0, 1, 0) + jnp.where(i == n_rb - 1, 1, 0)
        for j in range(n_cb):
            kc0 = int(np.clip(NA_QC * j - NA_WIN_COLS // 2, 0, GRID_W - NA_KC))
            ccfg = 0 if j == 0 else (2 if j == n_cb - 1 else 1)
            q_parts = []
            for rr in range(NA_QR):
                start = pl.multiple_of((NA_QR * i + rr) * GRID_W + NA_QC * j, NA_QC)
                q_parts.append(q_ref[pl.ds(start, NA_QC), :])
            q = jnp.concatenate(q_parts, axis=0)
            k_parts, v_parts = [], []
            for kk in range(NA_KR):
                start = pl.multiple_of((kr0 + kk) * GRID_W + kc0, 8)
                k_parts.append(k_ref[pl.ds(start, NA_KC), :])
                v_parts.append(v_ref[pl.ds(start, NA_KC), :])
            k = jnp.concatenate(k_parts, axis=0).astype(BF16)
            v = jnp.concatenate(v_parts, axis=0).astype(BF16)
            q2 = jnp.concatenate([jnp.where(low, q, 0.0), jnp.where(low, 0.0, q)], axis=0).astype(BF16)
            s_loc = _bdot_nt(q2, k) + bias_ref[rcfg, ccfg]
            s_ctx = _bdot_nt(q2, kctx)
            m = jnp.maximum(jnp.max(s_loc, axis=-1, keepdims=True),
                            jnp.max(s_ctx, axis=-1, keepdims=True))
            p_loc = jnp.exp(s_loc - m)
            p_ctx = jnp.exp(s_ctx - m)
            l = jnp.sum(p_loc, axis=-1, keepdims=True) + jnp.sum(p_ctx, axis=-1, keepdims=True)
            o2 = (jnp.dot(p_loc.astype(BF16), v, preferred_element_type=F32)
                  + jnp.dot(p_ctx.astype(BF16), vctx, preferred_element_type=F32)) * (1.0 / l)
            o = jnp.where(low, o2[:NA_TQ], o2[NA_TQ:])
            for rr in range(NA_QR):
                start = pl.multiple_of((NA_QR * i + rr) * GRID_W + NA_QC * j, NA_QC)
                o_ref[pl.ds(start, NA_QC), :] = o[rr * NA_QC:(rr + 1) * NA_QC, :].astype(o_ref.dtype)
        return carry

    lax.fori_loop(0, n_rb, row_block, 0)


def _na_attn(u_qkv, row0, nb, L, cache_k, cache_v, layer, bias_inputs):
    npair = NA_HEADS // 2
    rows = L // GRID_W
    blk0 = row0 // L
    lc = cache_k.shape[2]
    vq, drt, offsets = bias_inputs
    return pl.pallas_call(
        functools.partial(_na_kernel, rows, offsets),
        grid=(npair, nb),
        in_specs=[pl.BlockSpec((L, LANES), lambda p, b: (blk0 + b, p)),
                  pl.BlockSpec((L, LANES), lambda p, b: (blk0 + b, npair + p)),
                  pl.BlockSpec((L, LANES), lambda p, b: (blk0 + b, 2 * npair + p)),
                  pl.BlockSpec((1, 1, lc, LANES), lambda p, b: (b, layer, 0, p)),
                  pl.BlockSpec((1, 1, lc, LANES), lambda p, b: (b, layer, 0, p)),
                  pl.BlockSpec((None, 2) + vq.shape[2:], lambda p, b: (layer, p, 0, 0, 0, 0)),
                  pl.BlockSpec(drt.shape, lambda p, b: (0, 0, 0))],
        out_specs=pl.BlockSpec((L, LANES), lambda p, b: (b, p)),
        out_shape=jax.ShapeDtypeStruct((nb * L, NA_WIDTH), BF16),
        scratch_shapes=[pltpu.VMEM((3, 3, 2 * NA_TQ, NA_TK), F32)],
        compiler_params=_cparams(("arbitrary", "arbitrary")),
        name="na_attn",
    )(u_qkv, u_qkv, u_qkv, cache_k, cache_v, vq, drt)


def _ssd_constants():
    tril = np.tril(np.ones((SSD_CHUNK, SSD_CHUNK), np.float32))
    expand = np.zeros((2, LANES, SSD_PAD), np.float32)
    colb = np.zeros((2, LANES, SSD_HEADS * LANES), np.float32)
    for d in range(2):
        for h in range(SSD_HEADS):
            s = _slot_of_head(h)
            expand[d, SLOTS * d + s, HEAD_DIM * s:HEAD_DIM * (s + 1)] = 1.0
            colb[d, SLOTS * d + s, LANES * h:LANES * (h + 1)] = 1.0
    two = lambda m: np.concatenate([m, m], axis=-2)
    return np.concatenate([tril, tril], axis=1), two(expand), two(colb)


def _rope_tables(L):
    t = np.arange(L)
    rows = (t // GRID_W).astype(np.float64)
    cols = (t % GRID_W).astype(np.float64)
    quarter = SSD_STATE // 4
    inv = ROPE_BASE ** (-np.arange(quarter, dtype=np.float64) / quarter)
    n = np.arange(SSD_STATE)
    pos = np.where(n[None, :] < SSD_STATE // 2, rows[:, None], cols[:, None])
    ang = pos * inv[n % quarter][None, :]
    first = (n % (SSD_STATE // 2)) < quarter
    cos = np.cos(ang)
    sin = np.where(first[None, :], -np.sin(ang), np.sin(ang))
    tile = lambda a: np.concatenate([a] * SSD_NGROUPS, axis=1).astype(np.float32)
    return tile(cos), tile(sin)


def _split2(x):
    hi = x.astype(BF16)
    lo = (x - hi.astype(F32)).astype(BF16)
    return hi, lo


def _ssd_kernel(L, use_rope, use_init, *refs):
    (u_ref, convw_ref, convb_ref, dtb_ref, a_ref, ax_ref, d_ref, nw_ref,
     tril_ref, exp_ref, colb_ref) = refs[:11]
    pos = 11
    if use_rope:
        cos_ref, sin_ref = refs[pos:pos + 2]
        pos += 2
    if use_init:
        s0_refs = refs[pos:pos + 2]
        pos += 2
    y_ref = refs[pos]
    sfin_refs = refs[pos + 1:pos + 3]
    act_ref, dt_ref, s_ref, yacc_ref = refs[pos + 3:]

    nc = L // SSD_CHUNK
    C = SSD_CHUNK
    GW = SSD_PAD // SSD_NGROUPS
    XC = SSD_PAD
    CONV_W = SSD_PAD + 2 * LANES
    DTC = XC + CONV_W
    HALO = 8

    lane = lax.broadcasted_iota(jnp.int32, (1, LANES), 1)
    first_q = (lane % (SSD_STATE // 2)) < (SSD_STATE // 4)

    def prep(c, carry):
        r0 = pl.multiple_of(c * C, C)
        main = u_ref[pl.ds(r0, C), XC:XC + CONV_W]
        pstart = pl.multiple_of(jnp.maximum(r0 - HALO, 0), HALO)
        nstart = pl.multiple_of(jnp.minimum(r0 + C, L - HALO), HALO)
        prev = u_ref[pl.ds(pstart, HALO), XC:XC + CONV_W] * jnp.where(c > 0, 1.0, 0.0)
        nxt = u_ref[pl.ds(nstart, HALO), XC:XC + CONV_W] * jnp.where(c < nc - 1, 1.0, 0.0)
        win = jnp.concatenate([prev, main, nxt], axis=0)
        acc = jnp.zeros((C, CONV_W), F32) + convb_ref[...]
        for k in range(SSD_CONV):
            off = HALO + k - SSD_CONV // 2
            acc = acc + win[off:off + C, :] * convw_ref[k:k + 1, :]
        act = _silu(acc)
        act_ref[pl.ds(r0, C), 0:SSD_PAD] = act[:, 0:SSD_PAD]
        for t in range(2):
            bc = act[:, SSD_PAD + t * LANES:SSD_PAD + (t + 1) * LANES]
            if use_rope:
                partner = jnp.where(first_q, pltpu.roll(bc, LANES - SSD_STATE // 4, 1),
                                    pltpu.roll(bc, SSD_STATE // 4, 1))
                bc = bc * cos_ref[pl.ds(r0, C), :] + partner * sin_ref[pl.ds(r0, C), :]
            act_ref[pl.ds(r0, C), SSD_PAD + t * LANES:SSD_PAD + (t + 1) * LANES] = bc
        raw = u_ref[pl.ds(r0, C), DTC:DTC + LANES] + dtb_ref[...]
        dt_ref[pl.ds(r0, C), :] = jnp.maximum(raw, 0.0) + jnp.log1p(jnp.exp(-jnp.abs(raw)))
        yacc_ref[pl.ds(r0, C), :] = jnp.zeros((C, SSD_PAD), F32)
        return carry

    lax.fori_loop(0, nc, prep, 0)

    if use_init:
        zero = jnp.zeros((HEAD_DIM, SSD_STATE), F32)
        for d in range(2):
            for g in range(SSD_NGROUPS):
                rows_pn = []
                for hh in range(3):
                    blk = s0_refs[d][0, 0, 3 * g + hh]
                    rows_pn.append(jnp.concatenate([blk, zero] if g == 0 else [zero, blk], axis=1))
                rows_pn.append(jnp.zeros((HEAD_DIM, LANES), F32))
                s_ref[d, g] = jnp.concatenate(rows_pn, axis=0).T
    else:
        s_ref[...] = jnp.zeros_like(s_ref)

    a_row = a_ref[...]
    li = lax.broadcasted_iota(jnp.int32, (C, C), 0)
    si = lax.broadcasted_iota(jnp.int32, (C, C), 1)
    causal = [li >= si, si >= li]
    low64 = lane < HEAD_DIM
    grp_mask = [low64, jnp.logical_not(low64)]

    def one_direction(d, c):
        r0 = pl.multiple_of(c * C, C)
        x = act_ref[pl.ds(r0, C), 0:SSD_PAD]
        bmat = act_ref[pl.ds(r0, C), SSD_PAD:SSD_PAD + LANES]
        cmat = act_ref[pl.ds(r0, C), SSD_PAD + LANES:SSD_PAD + 2 * LANES]
        dt = dt_ref[pl.ds(r0, C), :]
        dta = dt * a_row
        hi, lo = _split2(dta)
        cs = jnp.dot(tril_ref[...], jnp.concatenate([hi, lo], axis=0),
                     preferred_element_type=F32)
        q = cs if d == 0 else cs - dta
        expand = lambda v: jnp.dot(jnp.concatenate(_split2(v), axis=1), exp_ref[d],
                                   preferred_element_type=F32)
        dt_x = expand(dt)
        cs_x = expand(cs)
        end_x = cs_x[C - 1:C, :]
        if d == 0:
            off_scale = jnp.exp(cs_x)
            w_state = jnp.exp(end_x - cs_x)
        else:
            e_x = cs_x - dt_x * ax_ref[d]
            off_scale = jnp.exp(end_x - e_x)
            w_state = jnp.exp(e_x)
        chunk_decay = jnp.exp(end_x)
        xdt = x * dt_x
        xdt_b = xdt.astype(BF16)
        rhs_state = (xdt * w_state).astype(BF16)
        qcol = jnp.dot(jnp.concatenate(_split2(q), axis=1), colb_ref[d],
                       preferred_element_type=F32)
        q_t = q.T
        b_t = bmat.T.astype(BF16)
        b_b = bmat.astype(BF16)
        cms = [jnp.where(grp_mask[g], cmat, 0.0).astype(BF16) for g in range(SSD_NGROUPS)]
        gmats = lax.dot_general(jnp.concatenate(cms, axis=0), b_b, (((1,), (1,)), ((), ())),
                                preferred_element_type=F32)
        st_all = jnp.dot(b_t, rhs_state, preferred_element_type=F32)
        for g in range(SSD_NGROUPS):
            gmat = gmats[g * C:(g + 1) * C]
            s_old = s_ref[d, g]
            y_off = (jnp.dot(cms[g], s_old.astype(BF16), preferred_element_type=F32)
                     * off_scale[:, g * GW:(g + 1) * GW])
            ms = []
            for hh in range(3):
                h = 3 * g + hh
                slot = 4 * g + hh
                row = q_t[SLOTS * d + slot:SLOTS * d + slot + 1, :]
                col = qcol[:, h * LANES:(h + 1) * LANES]
                seg = (col - row) if d == 0 else (row - col)
                ms.append((gmat * jnp.exp(jnp.where(causal[d], seg, NEG_BIG))).astype(BF16))
            r01 = jnp.dot(jnp.concatenate(ms[:2], axis=0), xdt_b[:, 2 * g * LANES:(2 * g + 1) * LANES],
                          preferred_element_type=F32)
            r2 = jnp.dot(ms[2], xdt_b[:, (2 * g + 1) * LANES:(2 * g + 2) * LANES],
                         preferred_element_type=F32)
            y_g = jnp.concatenate([jnp.where(low64, r01[:C], r01[C:]), r2], axis=1) + y_off
            yacc_ref[pl.ds(r0, C), g * GW:(g + 1) * GW] += y_g
            s_ref[d, g] = (s_old * chunk_decay[:, g * GW:(g + 1) * GW]
                           + st_all[:, g * GW:(g + 1) * GW])

    def scan(i, carry):
        one_direction(0, i)
        one_direction(1, nc - 1 - i)
        return carry

    lax.fori_loop(0, nc, scan, 0)
    for d in range(2):
        for g in range(SSD_NGROUPS):
            s_t = s_ref[d, g].T
            for hh in range(3):
                sfin_refs[d][0, 3 * g + hh] = s_t[hh * HEAD_DIM:(hh + 1) * HEAD_DIM,
                                                  g * SSD_STATE:(g + 1) * SSD_STATE]

    def finish(c, carry):
        r0 = pl.multiple_of(c * C, C)
        y = yacc_ref[pl.ds(r0, C), :] + act_ref[pl.ds(r0, C), 0:SSD_PAD] * d_ref[...]
        y = y * _silu(u_ref[pl.ds(r0, C), 0:SSD_PAD])
        for g in range(SSD_NGROUPS):
            yg = y[:, g * GW:(g + 1) * GW]
            ms = jnp.sum(yg * yg, axis=-1, keepdims=True) * (1.0 / (SSD_INNER // SSD_NGROUPS))
            yn = yg * lax.rsqrt(ms + EPS) * nw_ref[:, g * GW:(g + 1) * GW]
            y_ref[pl.ds(r0, C), g * GW:(g + 1) * GW] = yn.astype(y_ref.dtype)
        return carry

    lax.fori_loop(0, nc, finish, 0)


def _ssd(u_ssd, row0, nb, L, prm, layer, use_rope, s0):
    blk0 = row0 // L
    tril2, exp2, colb2 = _ssd_constants()
    consts = [jnp.asarray(tril2, dtype=BF16), jnp.asarray(exp2, dtype=BF16), jnp.asarray(colb2, dtype=BF16)]
    full = lambda a: pl.BlockSpec(a.shape, lambda b, _n=a.ndim: (0,) * _n)
    of_layer = lambda a: pl.BlockSpec((None,) + a.shape[1:], lambda b, _n=a.ndim: (layer,) + (0,) * (_n - 1))
    per_layer = [prm[k] for k in ("conv_w", "conv_b", "dt_bias", "a_row", "a_x", "d_row", "norm_w")]
    args = [u_ssd] + per_layer + consts
    in_specs = ([pl.BlockSpec((L, U_SSD), lambda b: (blk0 + b, 0))] + [of_layer(a) for a in per_layer]
                + [full(a) for a in consts])
    if use_rope:
        cos, sin = _rope_tables(L)
        tabs = [jnp.asarray(cos), jnp.asarray(sin)]
        args += tabs
        in_specs += [full(a) for a in tabs]
    state_block = (SSD_HEADS, HEAD_DIM, SSD_STATE)
    if s0 is not None:
        args += list(s0)
        in_specs += [pl.BlockSpec((1, 1) + state_block, lambda b: (b, layer, 0, 0, 0))] * 2
    sshape = (2, SSD_NGROUPS, LANES, SSD_PAD // SSD_NGROUPS)
    return pl.pallas_call(
        functools.partial(_ssd_kernel, L, use_rope, s0 is not None),
        grid=(nb,),
        in_specs=in_specs,
        out_specs=[pl.BlockSpec((L, SSD_PAD), lambda b: (b, 0))]
                  + [pl.BlockSpec((1,) + state_block, lambda b: (b, 0, 0, 0))] * 2,
        out_shape=[jax.ShapeDtypeStruct((nb * L, SSD_PAD), BF16)]
                  + [jax.ShapeDtypeStruct((nb,) + state_block, F32)] * 2,
        scratch_shapes=[pltpu.VMEM((L, SSD_PAD + 2 * LANES), F32),
                        pltpu.VMEM((L, LANES), F32),
                        pltpu.VMEM(sshape, F32),
                        pltpu.VMEM((L, SSD_PAD), F32)],
        compiler_params=_cparams(("parallel",)),
        name="ssd",
    )(*args)


def _pad_heads(a, axis=-1):
    a = jnp.moveaxis(a, axis, -1)
    lead = a.shape[:-1]
    a = a.reshape(lead + (SSD_NGROUPS, 3, HEAD_DIM))
    a = jnp.pad(a, [(0, 0)] * len(lead) + [(0, 0), (0, 1), (0, 0)])
    return jnp.moveaxis(a.reshape(lead + (SSD_PAD,)), -1, axis)


def _pad_dt_lanes(a):
    lead = a.shape[:-2]
    a = a.reshape(lead + (2, SSD_NGROUPS, 3))
    a = jnp.pad(a, [(0, 0)] * len(lead) + [(0, 0), (0, 0), (0, 1)]).reshape(lead + (2 * SLOTS,))
    return jnp.pad(a, [(0, 0)] * len(lead) + [(0, LANES - 2 * SLOTS)])


def _mixer_params(w_in, w_out, ssd_conv_w, ssd_conv_b, ssd_dt_bias, ssd_a_log, ssd_d, ssd_norm):
    w_in = w_in.astype(BF16)
    o = FN_WIDTH
    z = _pad_heads(w_in[..., o:o + SSD_INNER])
    o += SSD_INNER
    xw = _pad_heads(w_in[..., o:o + SSD_INNER])
    bc = w_in[..., o + SSD_INNER:o + SSD_CONV_DIM]
    o += SSD_CONV_DIM
    dtw = _pad_dt_lanes(w_in[..., o:o + 2 * SSD_HEADS].reshape(DEPTH, D_MODEL, 2, SSD_HEADS))
    o += 2 * SSD_HEADS
    qw = w_in[..., o:o + NA_WIDTH] * (HEAD_DIM ** -0.5)
    kv = w_in[..., o + NA_WIDTH:]
    w_in_pad = jnp.concatenate([w_in[..., :FN_WIDTH], z, xw, bc, dtw, qw, kv], axis=-1).astype(BF16)
    a = -jnp.exp(ssd_a_log.astype(F32))
    a_x = jnp.repeat(_pad_dt_lanes(a)[:, :2 * SLOTS].reshape(DEPTH, 2, SLOTS), HEAD_DIM, axis=-1)
    ssd = {
        "conv_w": jnp.concatenate([_pad_heads(ssd_conv_w[..., :SSD_INNER]), ssd_conv_w[..., SSD_INNER:]], axis=-1),
        "conv_b": jnp.concatenate([_pad_heads(ssd_conv_b[..., :SSD_INNER]),
                                   ssd_conv_b[..., SSD_INNER:]], axis=-1)[:, None, :],
        "dt_bias": _pad_dt_lanes(ssd_dt_bias)[:, None, :],
        "a_row": _pad_dt_lanes(a)[:, None, :],
        "a_x": a_x.reshape(DEPTH, 2, 1, SSD_PAD),
        "d_row": _pad_heads(jnp.repeat(ssd_d, HEAD_DIM, axis=-1))[:, None, :],
        "norm_w": _pad_heads(ssd_norm)[:, None, :],
    }
    return {
        "w_in": w_in_pad,
        "w_out_fn": w_out[:, :FN_WIDTH].astype(BF16),
        "w_out_ssd": _pad_heads(w_out[:, FN_WIDTH:FN_WIDTH + SSD_INNER], axis=1).astype(BF16),
        "w_out_att": w_out[:, FN_WIDTH + SSD_INNER:].astype(BF16),
        "ssd": ssd,
    }


def _pick_tile(rows, want):
    t = min(rows, want)
    while rows % t:
        t //= 2
    return t


def kernel(x_prompt, x_sample, c, state_ssd_fwd, state_ssd_bwd, cache_attn_k, cache_attn_v, c_ctx, mod_w, mod_b, norm_pre, norm_post, ffn_w13, ffn_w2, w_in, w_out, ssd_conv_w, ssd_conv_b, ssd_dt_bias, ssd_a_log, ssd_d, ssd_norm, na_rpb):
    nbp, lp, _ = x_prompt.shape
    nbs, ls, _ = x_sample.shape
    n_ctx, n_lat = nbp * lp, nbs * ls
    assert n_ctx == n_lat and n_ctx % ls == 0 and nbs + 1 <= 8
    rpg = ls
    xs = (x_prompt.reshape(n_ctx, D_MODEL), x_sample.reshape(n_lat, D_MODEL))

    cvec = jnp.zeros((8, D_MODEL), F32).at[0].set(c_ctx).at[1:1 + nbs].set(c)
    mods = _mods(cvec, mod_w, mod_b).reshape(DEPTH, 8, N_MOD, D_MODEL)
    w13 = ffn_w13.astype(BF16)
    w2 = ffn_w2.astype(BF16)
    p = _mixer_params(w_in, w_out, ssd_conv_w, ssd_conv_b, ssd_dt_bias, ssd_a_log, ssd_d, ssd_norm)
    cache_k = cache_attn_k.reshape(nbs, DEPTH, cache_attn_k.shape[2], NA_WIDTH)
    cache_v = cache_attn_v.reshape(nbs, DEPTH, cache_attn_v.shape[2], NA_WIDTH)
    s0 = (state_ssd_fwd.astype(F32), state_ssd_bwd.astype(F32))
    na_bias = _na_bias_inputs(na_rpb, ls // GRID_W)
    tm_ffn = _pick_tile(rpg, 512)
    tm_proj = _pick_tile(rpg, 512)

    new_sf, new_sb, new_k, new_v = [], [], [], []
    for l in range(DEPTH):
        gm = jnp.concatenate([jnp.broadcast_to(mods[l, 0], (n_ctx // rpg, N_MOD, D_MODEL)),
                              mods[l, 1:1 + nbs]], axis=0)
        xs = _ffn(xs, gm, 0, norm_pre[l, 0], norm_post[l, 0], w13, w2, l, 0, rpg, tm_ffn)
        u_fn, u_ssd, u_qkv = _inproj(xs, gm[:, 3:6], norm_pre[l, 1], p["w_in"], l, rpg, tm_proj)

        y_ssd_c, sf_c, sb_c = _ssd(u_ssd, 0, nbp, lp, p["ssd"], l, False, None)
        mix_ctx = (_fourier(u_fn, 0, nbp, lp), y_ssd_c, _ctx_attn(u_qkv, nbp, lp))
        new_sf.append(sf_c)
        new_sb.append(sb_c)
        new_k.append(u_qkv[:n_ctx, NA_WIDTH:2 * NA_WIDTH].reshape(nbp, lp, NA_HEADS, HEAD_DIM))
        new_v.append(u_qkv[:n_ctx, 2 * NA_WIDTH:].reshape(nbp, lp, NA_HEADS, HEAD_DIM))

        y_ssd_l, _, _ = _ssd(u_ssd, n_ctx, nbs, ls, p["ssd"], l, True, s0)
        mix_lat = (_fourier(u_fn, n_ctx, nbs, ls), y_ssd_l,
                   _na_attn(u_qkv, n_ctx, nbs, ls, cache_k, cache_v, l, na_bias))

        mix = (mix_ctx, mix_lat, (p["w_out_fn"], p["w_out_ssd"], p["w_out_att"]), norm_post[l, 1])
        xs = _ffn(xs, gm, 6, norm_pre[l, 2], norm_post[l, 2], w13, w2, l, 1, rpg, tm_ffn, mix=mix)

    return (xs[0].reshape(nbp, lp, D_MODEL), xs[1].reshape(nbs, ls, D_MODEL),
            jnp.stack(new_sf, axis=1), jnp.stack(new_sb, axis=1),
            jnp.stack(new_k, axis=1), jnp.stack(new_v, axis=1))
```

```python
import functools
import math

import numpy as np
import jax
import jax.numpy as jnp
from jax import lax
from jax.experimental import pallas as pl
from jax.experimental.pallas import tpu as pltpu

F32 = jnp.float32
BF16 = jnp.bfloat16

D_MODEL = 1024
DEPTH = 2
GRID_W = 64
FF_HIDDEN = 2816
N_MOD = 9
HEAD_DIM = 64
FN_WIDTH = 256
FN_GROUPS = 4
SSD_INNER = 384
SSD_HEADS = 6
SSD_STATE = 64
SSD_NGROUPS = 2
SSD_CONV = 5
SSD_CHUNK = 128
SSD_CONV_DIM = 640
SSD_IN = 1420
NA_WIDTH = 384
NA_HEADS = 6
NA_WIN_ROWS = 8
NA_WIN_COLS = 16
ROPE_BASE = 10000.0
EPS = 1e-6

LANES = 128
VMEM_LIMIT = 56 * 1024 * 1024

SLOTS = 8
SSD_PAD = SLOTS * HEAD_DIM
U_FN = FN_WIDTH
U_SSD = 2 * SSD_PAD + 2 * LANES + LANES
U_QKV = 3 * NA_WIDTH
U_TOTAL = U_FN + U_SSD + U_QKV
NEG_BIG = -1e30


def _slot_of_head(h):
    return 4 * (h // 3) + (h % 3)


def _cparams(sem):
    return pltpu.CompilerParams(dimension_semantics=sem, vmem_limit_bytes=VMEM_LIMIT)


def _rms(x):
    return x * lax.rsqrt(jnp.mean(x * x, axis=-1, keepdims=True) + EPS)


def _silu(x):
    return x * jax.nn.sigmoid(x)


def _bdot(a, b):
    return jnp.dot(a.astype(BF16), b.astype(BF16), preferred_element_type=F32)


def _bdot_nt(a, b):
    return lax.dot_general(a.astype(BF16), b.astype(BF16), (((1,), (1,)), ((), ())),
                           preferred_element_type=F32)


MOD_TN = 1152


def _mods_kernel(c_ref, w_ref, b_ref, o_ref):
    s = _silu(c_ref[...])
    o_ref[0] = _bdot(s, w_ref[0]) + b_ref[0]


def _mods(cvec, mod_w, mod_b):
    ncol = N_MOD * D_MODEL
    return pl.pallas_call(
        _mods_kernel,
        grid=(DEPTH, ncol // MOD_TN),
        in_specs=[pl.BlockSpec((8, D_MODEL), lambda l, j: (0, 0)),
                  pl.BlockSpec((1, D_MODEL, MOD_TN), lambda l, j: (l, 0, j)),
                  pl.BlockSpec((1, 1, MOD_TN), lambda l, j: (l, 0, j))],
        out_specs=pl.BlockSpec((1, 8, MOD_TN), lambda l, j: (l, 0, j)),
        out_shape=jax.ShapeDtypeStruct((DEPTH, 8, ncol), F32),
        compiler_params=_cparams(("parallel", "parallel")),
        name="mods",
    )(cvec, mod_w, mod_b.reshape(DEPTH, 1, ncol))


FFN_TH = 256


def _halves(nhalf):
    first = lambda i, *_: (jnp.minimum(i, nhalf - 1), 0)
    second = lambda i, *_: (jnp.maximum(i - nhalf, 0), 0)
    return first, second


def _on_half(nhalf, fn):
    i = pl.program_id(0)
    pl.when(i < nhalf)(functools.partial(fn, 0))
    pl.when(i >= nhalf)(functools.partial(fn, 1))


def _ffn_kernel(nhalf, mrow, with_mix, *refs):
    xa_ref, xb_ref, m_ref, gpre_ref, gpost_ref, w13_ref, w2_ref = refs[:7]
    pos = 7
    if with_mix:
        mix_refs = (refs[pos:pos + 3], refs[pos + 3:pos + 6])
        wmix_refs = refs[pos + 6:pos + 9]
        gmix_ref = refs[pos + 9]
        pos += 10
    oa_ref, ob_ref, h_ref, acc_ref = refs[pos:pos + 4]
    x_refs = (xa_ref, xb_ref)
    o_refs = (oa_ref, ob_ref)
    if with_mix:
        x1_ref = refs[pos + 4]

    def body(half):
        x = x_refs[half][...]
        if with_mix:
            y = None
            for y_ref, w_ref in zip(mix_refs[half], wmix_refs):
                d = jnp.dot(y_ref[...], w_ref[...], preferred_element_type=F32)
                y = d if y is None else y + d
            x = x + _rms(y) * (gmix_ref[...] * m_ref[0, mrow - 1:mrow, :])
            x1_ref[...] = x
        shift = m_ref[0, mrow:mrow + 1, :]
        scale = m_ref[0, mrow + 1:mrow + 2, :]
        h_ref[...] = (_rms(x) * (gpre_ref[...] * (1.0 + scale)) + shift).astype(BF16)

        h = h_ref[...]
        for j in range(FF_HIDDEN // FFN_TH):
            cols = slice(j * FFN_TH, (j + 1) * FFN_TH)
            g = jnp.dot(h, w13_ref[:, cols], preferred_element_type=F32)
            u = jnp.dot(h, w13_ref[:, FF_HIDDEN + j * FFN_TH:FF_HIDDEN + (j + 1) * FFN_TH],
                        preferred_element_type=F32)
            a = (_silu(g) * u).astype(BF16)
            part = jnp.dot(a, w2_ref[cols, :], preferred_element_type=F32)
            if j == 0:
                acc_ref[...] = part
            else:
                acc_ref[...] += part

        gate = m_ref[0, mrow + 2:mrow + 3, :]
        y = _rms(acc_ref[...]) * (gpost_ref[...] * (0.5 * gate))
        x = x1_ref[...] if with_mix else x_refs[half][...]
        o_refs[half][...] = x + y

    _on_half(nhalf, body)


def _ffn(xs, gmods, mrow, g_pre, g_post, w13, w2, layer, sub, rows_per_group, tm, mix=None):
    nh = xs[0].shape[0]
    nhalf = nh // tm
    tpg = rows_per_group // tm
    first, second = _halves(nhalf)
    resident = pl.Buffered(1)
    row = lambda a: a.reshape(1, D_MODEL)
    args = [xs[0], xs[1], gmods, row(g_pre), row(g_post), w13, w2]
    in_specs = [pl.BlockSpec((tm, D_MODEL), first),
                pl.BlockSpec((tm, D_MODEL), second),
                pl.BlockSpec((1, N_MOD, D_MODEL), lambda i: (i // tpg, 0, 0)),
                pl.BlockSpec((1, D_MODEL), lambda i: (0, 0)),
                pl.BlockSpec((1, D_MODEL), lambda i: (0, 0)),
                pl.BlockSpec((None, None, D_MODEL, 2 * FF_HIDDEN), lambda i: (layer, sub, 0, 0),
                             pipeline_mode=resident),
                pl.BlockSpec((None, None, FF_HIDDEN, D_MODEL), lambda i: (layer, sub, 0, 0),
                             pipeline_mode=resident)]
    scratch = [pltpu.VMEM((tm, D_MODEL), BF16), pltpu.VMEM((tm, D_MODEL), F32)]
    if mix is not None:
        mix_ctx, mix_lat, w_mix, g_post_mix = mix
        args += list(mix_ctx) + list(mix_lat) + list(w_mix) + [row(g_post_mix)]
        in_specs += ([pl.BlockSpec((tm, a.shape[1]), first) for a in mix_ctx]
                     + [pl.BlockSpec((tm, a.shape[1]), second) for a in mix_lat]
                     + [pl.BlockSpec((None,) + w.shape[1:], lambda i: (layer, 0, 0), pipeline_mode=resident)
                        for w in w_mix]
                     + [pl.BlockSpec((1, D_MODEL), lambda i: (0, 0))])
        scratch.append(pltpu.VMEM((tm, D_MODEL), F32))
    return pl.pallas_call(
        functools.partial(_ffn_kernel, nhalf, mrow, mix is not None),
        grid=(2 * nhalf,),
        in_specs=in_specs,
        out_specs=[pl.BlockSpec((tm, D_MODEL), first), pl.BlockSpec((tm, D_MODEL), second)],
        out_shape=[jax.ShapeDtypeStruct((nh, D_MODEL), F32)] * 2,
        scratch_shapes=scratch,
        compiler_params=_cparams(("arbitrary",)),
        name="ffn",
    )(*args)


def _inproj_kernel(nhalf, xa_ref, xb_ref, m_ref, gpre_ref, w_ref, ofn_ref, ossd_ref, oqkv_ref, ok_ref, ov_ref):
    x_refs = (xa_ref, xb_ref)

    def body(half):
        shift = m_ref[0, 0:1, :]
        scale = m_ref[0, 1:2, :]
        h = (_rms(x_refs[half][...]) * (gpre_ref[...] * (1.0 + scale)) + shift).astype(BF16)
        u = jnp.dot(h, w_ref[...], preferred_element_type=F32)
        ofn_ref[...] = u[:, :U_FN].astype(ofn_ref.dtype)
        ossd_ref[...] = u[:, U_FN:U_FN + U_SSD]
        oqkv_ref[...] = u[:, U_FN + U_SSD:]
        if half == 0:
            ok_ref[...] = u[:, U_FN + U_SSD + NA_WIDTH:U_FN + U_SSD + 2 * NA_WIDTH]
            ov_ref[...] = u[:, U_FN + U_SSD + 2 * NA_WIDTH:]

    _on_half(nhalf, body)


def _inproj(xs, gmods, g_pre, w_in_pad, layer, rows_per_group, tm):
    nh = xs[0].shape[0]
    n = 2 * nh
    nhalf = nh // tm
    tpg = rows_per_group // tm
    first, second = _halves(nhalf)
    return pl.pallas_call(
        functools.partial(_inproj_kernel, nhalf),
        grid=(n // tm,),
        in_specs=[pl.BlockSpec((tm, D_MODEL), first),
                  pl.BlockSpec((tm, D_MODEL), second),
                  pl.BlockSpec((1, 3, D_MODEL), lambda i: (i // tpg, 0, 0)),
                  pl.BlockSpec((1, D_MODEL), lambda i: (0, 0)),
                  pl.BlockSpec((None, D_MODEL, U_TOTAL), lambda i: (layer, 0, 0))],
        out_specs=[pl.BlockSpec((tm, U_FN), lambda i: (i, 0)),
                   pl.BlockSpec((tm, U_SSD), lambda i: (i, 0)),
                   pl.BlockSpec((tm, U_QKV), lambda i: (i, 0)),
                   pl.BlockSpec((tm, NA_WIDTH), first),
                   pl.BlockSpec((tm, NA_WIDTH), first)],
        out_shape=[jax.ShapeDtypeStruct((n, U_FN), BF16),
                   jax.ShapeDtypeStruct((n, U_SSD), F32),
                   jax.ShapeDtypeStruct((n, U_QKV), F32),
                   jax.ShapeDtypeStruct((nh, NA_WIDTH), F32),
                   jax.ShapeDtypeStruct((nh, NA_WIDTH), F32)],
        compiler_params=_cparams(("arbitrary",)),
        name="inproj",
    )(xs[0], xs[1], gmods, g_pre.reshape(1, D_MODEL), w_in_pad)


def _dft_tables(L):
    k = np.arange(L, dtype=np.int64)
    ang = 2.0 * np.pi * ((k[:, None] * k[None, :]) % L).astype(np.float64) / L
    sc = 1.0 / math.sqrt(L * HEAD_DIM)
    cl = (np.cos(ang) * sc).astype(np.float32)
    sl = (-np.sin(ang) * sc).astype(np.float32)
    m = np.arange(HEAD_DIM, dtype=np.int64)
    a64 = 2.0 * np.pi * ((m[:, None] * m[None, :]) % HEAD_DIM).astype(np.float64) / HEAD_DIM
    eye = np.eye(FN_GROUPS)
    w1 = np.concatenate([np.kron(eye, np.cos(a64)), np.kron(eye, np.sin(a64))], axis=1).astype(np.float32)
    return cl, sl, w1


def _fourier_kernel(nseq, L, tl, u_ref, w1_ref, cl_ref, sl_ref, o_ref, ab_ref):
    rt = pl.program_id(0)
    g = pl.program_id(1)

    @pl.when(rt == 0)
    def _():
        ab_ref[g] = _bdot(u_ref[...], w1_ref[...]).astype(BF16)

    for s in range(nseq):
        ab = ab_ref[g, s * L:(s + 1) * L, :]
        y = (jnp.dot(cl_ref[...], ab[:, :FN_WIDTH], preferred_element_type=F32)
             + jnp.dot(sl_ref[...], ab[:, FN_WIDTH:], preferred_element_type=F32))
        o_ref[s * tl:(s + 1) * tl, :] = y.astype(o_ref.dtype)


def _fourier(u_fn, row0, nb, L):
    cl, sl, w1 = _dft_tables(L)
    cl = jnp.asarray(cl).astype(BF16)
    sl = jnp.asarray(sl).astype(BF16)
    w1 = jnp.asarray(w1).astype(BF16)
    tl = min(L, 512)
    nrt = L // tl
    nseq = max(1, min(nb, 2048 // L)) if nrt == 1 else 1
    while nb % nseq:
        nseq -= 1
    ng = nb // nseq
    blk0 = row0 // (nseq * L)
    assert row0 % (nseq * L) == 0
    return pl.pallas_call(
        functools.partial(_fourier_kernel, nseq, L, tl),
        grid=(nrt, ng),
        in_specs=[pl.BlockSpec((nseq * L, FN_WIDTH), lambda rt, g: (blk0 + jnp.where(rt == 0, g, ng - 1), 0)),
                  pl.BlockSpec((FN_WIDTH, 2 * FN_WIDTH), lambda rt, g: (0, 0)),
                  pl.BlockSpec((tl, L), lambda rt, g: (rt, 0)),
                  pl.BlockSpec((tl, L), lambda rt, g: (rt, 0))],
        out_specs=pl.BlockSpec((nseq * tl, FN_WIDTH), lambda rt, g: (g * nrt + rt, 0)),
        out_shape=jax.ShapeDtypeStruct((nb * L, FN_WIDTH), BF16),
        scratch_shapes=[pltpu.VMEM((ng, nseq * L, 2 * FN_WIDTH), BF16)],
        compiler_params=_cparams(("arbitrary", "arbitrary")),
        name="fourier",
    )(u_fn, w1, cl, sl)


def _head_masks():
    lane = lax.broadcasted_iota(jnp.int32, (1, LANES), 1)
    return lane < HEAD_DIM


def _ctx_attn_kernel(nseq, L, q_ref, k_ref, v_ref, o_ref):
    low = _head_masks()
    for t in range(nseq):
        rows = slice(t * L, (t + 1) * L)
        for p in range(NA_HEADS // 2):
            lanes = slice(p * LANES, (p + 1) * LANES)
            q = q_ref[rows, lanes]
            k = k_ref[rows, lanes].astype(BF16)
            v = v_ref[rows, lanes].astype(BF16)
            q2 = jnp.concatenate([jnp.where(low, q, 0.0), jnp.where(low, 0.0, q)], axis=0)
            s = _bdot_nt(q2, k)
            m = jnp.max(s, axis=-1, keepdims=True)
            e = jnp.exp(s - m)
            inv = 1.0 / jnp.sum(e, axis=-1, keepdims=True)
            o2 = jnp.dot(e.astype(BF16), v, preferred_element_type=F32) * inv
            o_ref[rows, lanes] = jnp.where(low, o2[:L], o2[L:]).astype(o_ref.dtype)


def _ctx_attn(u_qkv, nb, L):
    nseq = 4 if nb % 4 == 0 else 1
    return pl.pallas_call(
        functools.partial(_ctx_attn_kernel, nseq, L),
        grid=(nb // nseq,),
        in_specs=[pl.BlockSpec((nseq * L, NA_WIDTH), lambda b: (b, 0)),
                  pl.BlockSpec((nseq * L, NA_WIDTH), lambda b: (b, 1)),
                  pl.BlockSpec((nseq * L, NA_WIDTH), lambda b: (b, 2))],
        out_specs=pl.BlockSpec((nseq * L, NA_WIDTH), lambda b: (b, 0)),
        out_shape=jax.ShapeDtypeStruct((nb * L, NA_WIDTH), BF16),
        compiler_params=_cparams(("parallel",)),
        name="ctx_attn",
    )(u_qkv, u_qkv, u_qkv)


NA_QR = 8
NA_QC = 16
NA_KR = 16
NA_KC = 32
NA_TQ = NA_QR * NA_QC
NA_TK = NA_KR * NA_KC


def _na_tile_geometry(rows):
    wr = min(NA_WIN_ROWS, rows)
    n_rb = rows // NA_QR
    n_cb = GRID_W // NA_QC
    rb = [0, 1, n_rb - 1]
    cb = [0, 1, n_cb - 1]
    dr = np.zeros((3, NA_QR, NA_KR), np.int64)
    vr = np.zeros((3, NA_QR, NA_KR), bool)
    for ci, i in enumerate(rb):
        kr0 = int(np.clip(NA_QR * i - NA_WIN_ROWS // 2, 0, rows - NA_KR))
        for rr in range(NA_QR):
            r = NA_QR * i + rr
            rs = int(np.clip(r - wr // 2, 0, rows - wr))
            for kk in range(NA_KR):
                kr = kr0 + kk
                vr[ci, rr, kk] = rs <= kr < rs + wr
                dr[ci, rr, kk] = np.clip(kr - r + NA_WIN_ROWS - 1, 0, 2 * NA_WIN_ROWS - 2)
    dc = np.zeros((3, NA_QC, NA_KC), np.int64)
    vc = np.zeros((3, NA_QC, NA_KC), bool)
    for ci, j in enumerate(cb):
        kc0 = int(np.clip(NA_QC * j - NA_WIN_COLS // 2, 0, GRID_W - NA_KC))
        for cq in range(NA_QC):
            c = NA_QC * j + cq
            cs = int(np.clip(c - NA_WIN_COLS // 2, 0, GRID_W - NA_WIN_COLS))
            for ck in range(NA_KC):
                kc = kc0 + ck
                vc[ci, cq, ck] = cs <= kc < cs + NA_WIN_COLS
                dc[ci, cq, ck] = np.clip(kc - c + NA_WIN_COLS - 1, 0, 2 * NA_WIN_COLS - 2)
    return dr, vr, dc, vc


def _na_bias_inputs(rpb, rows):
    dr, vr, dc, vc = _na_tile_geometry(rows)
    n_dc = 2 * NA_WIN_COLS - 1
    oh_c = (dc[..., None] == np.arange(n_dc)).astype(np.float32)
    oh_j = np.broadcast_to(oh_c[:, :, None], (3, NA_QC, NA_KR, NA_KC, n_dc)).reshape(3, NA_QC, NA_TK, n_dc)
    vq = jnp.einsum("lhab,ycjb->lhaycj", rpb.astype(F32), jnp.asarray(oh_j), precision=lax.Precision.HIGHEST)
    valid_c = np.broadcast_to(vc[:, :, None], (3, NA_QC, NA_KR, NA_KC)).reshape(3, NA_QC, NA_TK)
    vq = jnp.where(jnp.asarray(valid_c)[None, None, None], vq, NEG_BIG)
    drt = np.where(vr, dr, -1)[..., None]
    drt = np.broadcast_to(drt, (3, NA_QR, NA_KR, NA_KC)).reshape(3, NA_QR, NA_TK).astype(np.int32)
    offsets = [[sorted(set(dr[x, rr][vr[x, rr]].tolist())) for rr in range(NA_QR)] for x in range(3)]
    return vq, jnp.asarray(drt), offsets


def _na_kernel(rows, offsets, q_ref, k_ref, v_ref, kc_ref, vc_ref, vq_ref, drt_ref, o_ref, bias_ref):
    n_rb = rows // NA_QR
    n_cb = GRID_W // NA_QC
    low = _head_masks()
    kctx = kc_ref[0, 0].astype(BF16)
    vctx = vc_ref[0, 0].astype(BF16)

    @pl.when(pl.program_id(1) == 0)
    def _():
        def build(t, carry):
            hh = t // 3
            cc = t % 3
            for rc in range(3):
                for rr in range(NA_QR):
                    drrow = drt_ref[rc, rr:rr + 1, :]
                    acc = jnp.full((NA_QC, NA_TK), NEG_BIG, F32)
                    for a in offsets[rc][rr]:
                        acc = jnp.where(drrow == a, vq_ref[hh, a, cc], acc)
                    row0 = pl.multiple_of(hh * NA_TQ + rr * NA_QC, NA_QC)
                    bias_ref[rc, cc, pl.ds(row0, NA_QC), :] = acc
            return carry

        lax.fori_loop(0, 6, build, 0)

    def row_block(i, carry):
        kr0 = jnp.clip(NA_QR * i - NA_WIN_ROWS // 2, 0, rows - NA_KR)
        rcfg = jnp.where(i > 0, 1, 0) + jnp.where(i == n_rb - 1, 1, 0)
        for j in range(n_cb):
            kc0 = int(np.clip(NA_QC * j - NA_WIN_COLS // 2, 0, GRID_W - NA_KC))
            ccfg = 0 if j == 0 else (2 if j == n_cb - 1 else 1)
            q_parts = []
            for rr in range(NA_QR):
                start = pl.multiple_of((NA_QR * i + rr) * GRID_W + NA_QC * j, NA_QC)
                q_parts.append(q_ref[pl.ds(start, NA_QC), :])
            q = jnp.concatenate(q_parts, axis=0)
            k_parts, v_parts = [], []
            for kk in range(NA_KR):
                start = pl.multiple_of((kr0 + kk) * GRID_W + kc0, 8)
                k_parts.append(k_ref[pl.ds(start, NA_KC), :])
                v_parts.append(v_ref[pl.ds(start, NA_KC), :])
            k = jnp.concatenate(k_parts, axis=0).astype(BF16)
            v = jnp.concatenate(v_parts, axis=0).astype(BF16)
            q2 = jnp.concatenate([jnp.where(low, q, 0.0), jnp.where(low, 0.0, q)], axis=0).astype(BF16)
            s_loc = _bdot_nt(q2, k) + bias_ref[rcfg, ccfg]
            s_ctx = _bdot_nt(q2, kctx)
            m = jnp.maximum(jnp.max(s_loc, axis=-1, keepdims=True),
                            jnp.max(s_ctx, axis=-1, keepdims=True))
            p_loc = jnp.exp(s_loc - m)
            p_ctx = jnp.exp(s_ctx - m)
            l = jnp.sum(p_loc, axis=-1, keepdims=True) + jnp.sum(p_ctx, axis=-1, keepdims=True)
            o2 = (jnp.dot(p_loc.astype(BF16), v, preferred_element_type=F32)
                  + jnp.dot(p_ctx.astype(BF16), vctx, preferred_element_type=F32)) * (1.0 / l)
            o = jnp.where(low, o2[:NA_TQ], o2[NA_TQ:])
            for rr in range(NA_QR):
                start = pl.multiple_of((NA_QR * i + rr) * GRID_W + NA_QC * j, NA_QC)
                o_ref[pl.ds(start, NA_QC), :] = o[rr * NA_QC:(rr + 1) * NA_QC, :].astype(o_ref.dtype)
        return carry

    lax.fori_loop(0, n_rb, row_block, 0)


def _na_attn(u_qkv, row0, nb, L, cache_k, cache_v, layer, bias_inputs):
    npair = NA_HEADS // 2
    rows = L // GRID_W
    blk0 = row0 // L
    lc = cache_k.shape[2]
    vq, drt, offsets = bias_inputs
    return pl.pallas_call(
        functools.partial(_na_kernel, rows, offsets),
        grid=(npair, nb),
        in_specs=[pl.BlockSpec((L, LANES), lambda p, b: (blk0 + b, p)),
                  pl.BlockSpec((L, LANES), lambda p, b: (blk0 + b, npair + p)),
                  pl.BlockSpec((L, LANES), lambda p, b: (blk0 + b, 2 * npair + p)),
                  pl.BlockSpec((1, 1, lc, LANES), lambda p, b: (b, layer, 0, p)),
                  pl.BlockSpec((1, 1, lc, LANES), lambda p, b: (b, layer, 0, p)),
                  pl.BlockSpec((None, 2) + vq.shape[2:], lambda p, b: (layer, p, 0, 0, 0, 0)),
                  pl.BlockSpec(drt.shape, lambda p, b: (0, 0, 0))],
        out_specs=pl.BlockSpec((L, LANES), lambda p, b: (b, p)),
        out_shape=jax.ShapeDtypeStruct((nb * L, NA_WIDTH), BF16),
        scratch_shapes=[pltpu.VMEM((3, 3, 2 * NA_TQ, NA_TK), F32)],
        compiler_params=_cparams(("arbitrary", "arbitrary")),
        name="na_attn",
    )(u_qkv, u_qkv, u_qkv, cache_k, cache_v, vq, drt)


def _ssd_constants():
    tril = np.tril(np.ones((SSD_CHUNK, SSD_CHUNK), np.float32))
    expand = np.zeros((2, LANES, SSD_PAD), np.float32)
    colb = np.zeros((2, LANES, SSD_HEADS * LANES), np.float32)
    for d in range(2):
        for h in range(SSD_HEADS):
            s = _slot_of_head(h)
            expand[d, SLOTS * d + s, HEAD_DIM * s:HEAD_DIM * (s + 1)] = 1.0
            colb[d, SLOTS * d + s, LANES * h:LANES * (h + 1)] = 1.0
    two = lambda m: np.concatenate([m, m], axis=-2)
    return np.concatenate([tril, tril], axis=1), two(expand), two(colb)


def _rope_tables(L):
    t = np.arange(L)
    rows = (t // GRID_W).astype(np.float64)
    cols = (t % GRID_W).astype(np.float64)
    quarter = SSD_STATE // 4
    inv = ROPE_BASE ** (-np.arange(quarter, dtype=np.float64) / quarter)
    n = np.arange(SSD_STATE)
    pos = np.where(n[None, :] < SSD_STATE // 2, rows[:, None], cols[:, None])
    ang = pos * inv[n % quarter][None, :]
    first = (n % (SSD_STATE // 2)) < quarter
    cos = np.cos(ang)
    sin = np.where(first[None, :], -np.sin(ang), np.sin(ang))
    tile = lambda a: np.concatenate([a] * SSD_NGROUPS, axis=1).astype(np.float32)
    return tile(cos), tile(sin)


def _split2(x):
    hi = x.astype(BF16)
    lo = (x - hi.astype(F32)).astype(BF16)
    return hi, lo


def _ssd_kernel(L, use_rope, use_init, *refs):
    (u_ref, convw_ref, convb_ref, dtb_ref, a_ref, ax_ref, d_ref, nw_ref,
     tril_ref, exp_ref, colb_ref) = refs[:11]
    pos = 11
    if use_rope:
        cos_ref, sin_ref = refs[pos:pos + 2]
        pos += 2
    if use_init:
        s0_refs = refs[pos:pos + 2]
        pos += 2
    y_ref = refs[pos]
    sfin_refs = refs[pos + 1:pos + 3]
    act_ref, dt_ref, s_ref, yacc_ref = refs[pos + 3:]

    nc = L // SSD_CHUNK
    C = SSD_CHUNK
    GW = SSD_PAD // SSD_NGROUPS
    XC = SSD_PAD
    CONV_W = SSD_PAD + 2 * LANES
    DTC = XC + CONV_W
    HALO = 8

    lane = lax.broadcasted_iota(jnp.int32, (1, LANES), 1)
    first_q = (lane % (SSD_STATE // 2)) < (SSD_STATE // 4)

    def prep(c, carry):
        r0 = pl.multiple_of(c * C, C)
        main = u_ref[pl.ds(r0, C), XC:XC + CONV_W]
        pstart = pl.multiple_of(jnp.maximum(r0 - HALO, 0), HALO)
        nstart = pl.multiple_of(jnp.minimum(r0 + C, L - HALO), HALO)
        prev = u_ref[pl.ds(pstart, HALO), XC:XC + CONV_W] * jnp.where(c > 0, 1.0, 0.0)
        nxt = u_ref[pl.ds(nstart, HALO), XC:XC + CONV_W] * jnp.where(c < nc - 1, 1.0, 0.0)
        win = jnp.concatenate([prev, main, nxt], axis=0)
        acc = jnp.zeros((C, CONV_W), F32) + convb_ref[...]
        for k in range(SSD_CONV):
            off = HALO + k - SSD_CONV // 2
            acc = acc + win[off:off + C, :] * convw_ref[k:k + 1, :]
        act = _silu(acc)
        act_ref[pl.ds(r0, C), 0:SSD_PAD] = act[:, 0:SSD_PAD]
        for t in range(2):
            bc = act[:, SSD_PAD + t * LANES:SSD_PAD + (t + 1) * LANES]
            if use_rope:
                partner = jnp.where(first_q, pltpu.roll(bc, LANES - SSD_STATE // 4, 1),
                                    pltpu.roll(bc, SSD_STATE // 4, 1))
                bc = bc * cos_ref[pl.ds(r0, C), :] + partner * sin_ref[pl.ds(r0, C), :]
            act_ref[pl.ds(r0, C), SSD_PAD + t * LANES:SSD_PAD + (t + 1) * LANES] = bc
        raw = u_ref[pl.ds(r0, C), DTC:DTC + LANES] + dtb_ref[...]
        dt_ref[pl.ds(r0, C), :] = jnp.maximum(raw, 0.0) + jnp.log1p(jnp.exp(-jnp.abs(raw)))
        yacc_ref[pl.ds(r0, C), :] = jnp.zeros((C, SSD_PAD), F32)
        return carry

    lax.fori_loop(0, nc, prep, 0)

    if use_init:
        zero = jnp.zeros((HEAD_DIM, SSD_STATE), F32)
        for d in range(2):
            for g in range(SSD_NGROUPS):
                rows_pn = []
                for hh in range(3):
                    blk = s0_refs[d][0, 0, 3 * g + hh]
                    rows_pn.append(jnp.concatenate([blk, zero] if g == 0 else [zero, blk], axis=1))
                rows_pn.append(jnp.zeros((HEAD_DIM, LANES), F32))
                s_ref[d, g] = jnp.concatenate(rows_pn, axis=0).T
    else:
        s_ref[...] = jnp.zeros_like(s_ref)

    a_row = a_ref[...]
    li = lax.broadcasted_iota(jnp.int32, (C, C), 0)
    si = lax.broadcasted_iota(jnp.int32, (C, C), 1)
    causal = [li >= si, si >= li]
    low64 = lane < HEAD_DIM
    grp_mask = [low64, jnp.logical_not(low64)]

    def one_direction(d, c):
        r0 = pl.multiple_of(c * C, C)
        x = act_ref[pl.ds(r0, C), 0:SSD_PAD]
        bmat = act_ref[pl.ds(r0, C), SSD_PAD:SSD_PAD + LANES]
        cmat = act_ref[pl.ds(r0, C), SSD_PAD + LANES:SSD_PAD + 2 * LANES]
        dt = dt_ref[pl.ds(r0, C), :]
        dta = dt * a_row
        hi, lo = _split2(dta)
        cs = jnp.dot(tril_ref[...], jnp.concatenate([hi, lo], axis=0),
                     preferred_element_type=F32)
        q = cs if d == 0 else cs - dta
        expand = lambda v: jnp.dot(jnp.concatenate(_split2(v), axis=1), exp_ref[d],
                                   preferred_element_type=F32)
        dt_x = expand(dt)
        cs_x = expand(cs)
        end_x = cs_x[C - 1:C, :]
        if d == 0:
            off_scale = jnp.exp(cs_x)
            w_state = jnp.exp(end_x - cs_x)
        else:
            e_x = cs_x - dt_x * ax_ref[d]
            off_scale = jnp.exp(end_x - e_x)
            w_state = jnp.exp(e_x)
        chunk_decay = jnp.exp(end_x)
        xdt = x * dt_x
        xdt_b = xdt.astype(BF16)
        rhs_state = (xdt * w_state).astype(BF16)
        qcol = jnp.dot(jnp.concatenate(_split2(q), axis=1), colb_ref[d],
                       preferred_element_type=F32)
        q_t = q.T
        b_t = bmat.T.astype(BF16)
        b_b = bmat.astype(BF16)
        cms = [jnp.where(grp_mask[g], cmat, 0.0).astype(BF16) for g in range(SSD_NGROUPS)]
        gmats = lax.dot_general(jnp.concatenate(cms, axis=0), b_b, (((1,), (1,)), ((), ())),
                                preferred_element_type=F32)
        st_all = jnp.dot(b_t, rhs_state, preferred_element_type=F32)
        for g in range(SSD_NGROUPS):
            gmat = gmats[g * C:(g + 1) * C]
            s_old = s_ref[d, g]
            y_off = (jnp.dot(cms[g], s_old.astype(BF16), preferred_element_type=F32)
                     * off_scale[:, g * GW:(g + 1) * GW])
            ms = []
            for hh in range(3):
                h = 3 * g + hh
                slot = 4 * g + hh
                row = q_t[SLOTS * d + slot:SLOTS * d + slot + 1, :]
                col = qcol[:, h * LANES:(h + 1) * LANES]
                seg = (col - row) if d == 0 else (row - col)
                ms.append((gmat * jnp.exp(jnp.where(causal[d], seg, NEG_BIG))).astype(BF16))
            r01 = jnp.dot(jnp.concatenate(ms[:2], axis=0), xdt_b[:, 2 * g * LANES:(2 * g + 1) * LANES],
                          preferred_element_type=F32)
            r2 = jnp.dot(ms[2], xdt_b[:, (2 * g + 1) * LANES:(2 * g + 2) * LANES],
                         preferred_element_type=F32)
            y_g = jnp.concatenate([jnp.where(low64, r01[:C], r01[C:]), r2], axis=1) + y_off
            yacc_ref[pl.ds(r0, C), g * GW:(g + 1) * GW] += y_g
            s_ref[d, g] = (s_old * chunk_decay[:, g * GW:(g + 1) * GW]
                           + st_all[:, g * GW:(g + 1) * GW])

    def scan(i, carry):
        one_direction(0, i)
        one_direction(1, nc - 1 - i)
        return carry

    lax.fori_loop(0, nc, scan, 0)
    for d in range(2):
        for g in range(SSD_NGROUPS):
            s_t = s_ref[d, g].T
            for hh in range(3):
                sfin_refs[d][0, 3 * g + hh] = s_t[hh * HEAD_DIM:(hh + 1) * HEAD_DIM,
                                                  g * SSD_STATE:(g + 1) * SSD_STATE]

    def finish(c, carry):
        r0 = pl.multiple_of(c * C, C)
        y = yacc_ref[pl.ds(r0, C), :] + act_ref[pl.ds(r0, C), 0:SSD_PAD] * d_ref[...]
        y = y * _silu(u_ref[pl.ds(r0, C), 0:SSD_PAD])
        for g in range(SSD_NGROUPS):
            yg = y[:, g * GW:(g + 1) * GW]
            ms = jnp.sum(yg * yg, axis=-1, keepdims=True) * (1.0 / (SSD_INNER // SSD_NGROUPS))
            yn = yg * lax.rsqrt(ms + EPS) * nw_ref[:, g * GW:(g + 1) * GW]
            y_ref[pl.ds(r0, C), g * GW:(g + 1) * GW] = yn.astype(y_ref.dtype)
        return carry

    lax.fori_loop(0, nc, finish, 0)


def _ssd(u_ssd, row0, nb, L, prm, layer, use_rope, s0):
    blk0 = row0 // L
    tril2, exp2, colb2 = _ssd_constants()
    consts = [jnp.asarray(tril2, dtype=BF16), jnp.asarray(exp2, dtype=BF16), jnp.asarray(colb2, dtype=BF16)]
    full = lambda a: pl.BlockSpec(a.shape, lambda b, _n=a.ndim: (0,) * _n)
    of_layer = lambda a: pl.BlockSpec((None,) + a.shape[1:], lambda b, _n=a.ndim: (layer,) + (0,) * (_n - 1))
    per_layer = [prm[k] for k in ("conv_w", "conv_b", "dt_bias", "a_row", "a_x", "d_row", "norm_w")]
    args = [u_ssd] + per_layer + consts
    in_specs = ([pl.BlockSpec((L, U_SSD), lambda b: (blk0 + b, 0))] + [of_layer(a) for a in per_layer]
                + [full(a) for a in consts])
    if use_rope:
        cos, sin = _rope_tables(L)
        tabs = [jnp.asarray(cos), jnp.asarray(sin)]
        args += tabs
        in_specs += [full(a) for a in tabs]
    state_block = (SSD_HEADS, HEAD_DIM, SSD_STATE)
    if s0 is not None:
        args += list(s0)
        in_specs += [pl.BlockSpec((1, 1) + state_block, lambda b: (b, layer, 0, 0, 0))] * 2
    sshape = (2, SSD_NGROUPS, LANES, SSD_PAD // SSD_NGROUPS)
    return pl.pallas_call(
        functools.partial(_ssd_kernel, L, use_rope, s0 is not None),
        grid=(nb,),
        in_specs=in_specs,
        out_specs=[pl.BlockSpec((L, SSD_PAD), lambda b: (b, 0))]
                  + [pl.BlockSpec((1,) + state_block, lambda b: (b, 0, 0, 0))] * 2,
        out_shape=[jax.ShapeDtypeStruct((nb * L, SSD_PAD), BF16)]
                  + [jax.ShapeDtypeStruct((nb,) + state_block, F32)] * 2,
        scratch_shapes=[pltpu.VMEM((L, SSD_PAD + 2 * LANES), F32),
                        pltpu.VMEM((L, LANES), F32),
                        pltpu.VMEM(sshape, F32),
                        pltpu.VMEM((L, SSD_PAD), F32)],
        compiler_params=_cparams(("parallel",)),
        name="ssd",
    )(*args)


def _pad_heads(a, axis=-1):
    a = jnp.moveaxis(a, axis, -1)
    lead = a.shape[:-1]
    a = a.reshape(lead + (SSD_NGROUPS, 3, HEAD_DIM))
    a = jnp.pad(a, [(0, 0)] * len(lead) + [(0, 0), (0, 1), (0, 0)])
    return jnp.moveaxis(a.reshape(lead + (SSD_PAD,)), -1, axis)


def _pad_dt_lanes(a):
    lead = a.shape[:-2]
    a = a.reshape(lead + (2, SSD_NGROUPS, 3))
    a = jnp.pad(a, [(0, 0)] * len(lead) + [(0, 0), (0, 0), (0, 1)]).reshape(lead + (2 * SLOTS,))
    return jnp.pad(a, [(0, 0)] * len(lead) + [(0, LANES - 2 * SLOTS)])


def _mixer_params(w_in, w_out, ssd_conv_w, ssd_conv_b, ssd_dt_bias, ssd_a_log, ssd_d, ssd_norm):
    w_in = w_in.astype(BF16)
    o = FN_WIDTH
    z = _pad_heads(w_in[..., o:o + SSD_INNER])
    o += SSD_INNER
    xw = _pad_heads(w_in[..., o:o + SSD_INNER])
    bc = w_in[..., o + SSD_INNER:o + SSD_CONV_DIM]
    o += SSD_CONV_DIM
    dtw = _pad_dt_lanes(w_in[..., o:o + 2 * SSD_HEADS].reshape(DEPTH, D_MODEL, 2, SSD_HEADS))
    o += 2 * SSD_HEADS
    qw = w_in[..., o:o + NA_WIDTH] * (HEAD_DIM ** -0.5)
    kv = w_in[..., o + NA_WIDTH:]
    w_in_pad = jnp.concatenate([w_in[..., :FN_WIDTH], z, xw, bc, dtw, qw, kv], axis=-1).astype(BF16)
    a = -jnp.exp(ssd_a_log.astype(F32))
    a_x = jnp.repeat(_pad_dt_lanes(a)[:, :2 * SLOTS].reshape(DEPTH, 2, SLOTS), HEAD_DIM, axis=-1)
    ssd = {
        "conv_w": jnp.concatenate([_pad_heads(ssd_conv_w[..., :SSD_INNER]), ssd_conv_w[..., SSD_INNER:]], axis=-1),
        "conv_b": jnp.concatenate([_pad_heads(ssd_conv_b[..., :SSD_INNER]),
                                   ssd_conv_b[..., SSD_INNER:]], axis=-1)[:, None, :],
        "dt_bias": _pad_dt_lanes(ssd_dt_bias)[:, None, :],
        "a_row": _pad_dt_lanes(a)[:, None, :],
        "a_x": a_x.reshape(DEPTH, 2, 1, SSD_PAD),
        "d_row": _pad_heads(jnp.repeat(ssd_d, HEAD_DIM, axis=-1))[:, None, :],
        "norm_w": _pad_heads(ssd_norm)[:, None, :],
    }
    return {
        "w_in": w_in_pad,
        "w_out_fn": w_out[:, :FN_WIDTH].astype(BF16),
        "w_out_ssd": _pad_heads(w_out[:, FN_WIDTH:FN_WIDTH + SSD_INNER], axis=1).astype(BF16),
        "w_out_att": w_out[:, FN_WIDTH + SSD_INNER:].astype(BF16),
        "ssd": ssd,
    }


def _pick_tile(rows, want):
    t = min(rows, want)
    while rows % t:
        t //= 2
    return t


def kernel(x_prompt, x_sample, c, state_ssd_fwd, state_ssd_bwd, cache_attn_k, cache_attn_v, c_ctx, mod_w, mod_b, norm_pre, norm_post, ffn_w13, ffn_w2, w_in, w_out, ssd_conv_w, ssd_conv_b, ssd_dt_bias, ssd_a_log, ssd_d, ssd_norm, na_rpb):
    nbp, lp, _ = x_prompt.shape
    nbs, ls, _ = x_sample.shape
    n_ctx, n_lat = nbp * lp, nbs * ls
    assert n_ctx == n_lat and n_ctx % ls == 0 and nbs + 1 <= 8
    rpg = ls
    xs = (x_prompt.reshape(n_ctx, D_MODEL), x_sample.reshape(n_lat, D_MODEL))

    cvec = jnp.zeros((8, D_MODEL), F32).at[0].set(c_ctx).at[1:1 + nbs].set(c)
    mods = _mods(cvec, mod_w, mod_b).reshape(DEPTH, 8, N_MOD, D_MODEL)
    w13 = ffn_w13.astype(BF16)
    w2 = ffn_w2.astype(BF16)
    p = _mixer_params(w_in, w_out, ssd_conv_w, ssd_conv_b, ssd_dt_bias, ssd_a_log, ssd_d, ssd_norm)
    cache_k = cache_attn_k.reshape(nbs, DEPTH, cache_attn_k.shape[2], NA_WIDTH)
    cache_v = cache_attn_v.reshape(nbs, DEPTH, cache_attn_v.shape[2], NA_WIDTH)
    s0 = (state_ssd_fwd.astype(F32), state_ssd_bwd.astype(F32))
    na_bias = _na_bias_inputs(na_rpb, ls // GRID_W)
    tm_ffn = _pick_tile(rpg, 512)
    tm_proj = _pick_tile(rpg, 512)

    new_sf, new_sb, new_k, new_v = [], [], [], []
    for l in range(DEPTH):
        gm = jnp.concatenate([jnp.broadcast_to(mods[l, 0], (n_ctx // rpg, N_MOD, D_MODEL)),
                              mods[l, 1:1 + nbs]], axis=0)
        xs = _ffn(xs, gm, 0, norm_pre[l, 0], norm_post[l, 0], w13, w2, l, 0, rpg, tm_ffn)
        u_fn, u_ssd, u_qkv, k_ctx, v_ctx = _inproj(xs, gm[:, 3:6], norm_pre[l, 1], p["w_in"], l, rpg, tm_proj)

        y_ssd_c, sf_c, sb_c = _ssd(u_ssd, 0, nbp, lp, p["ssd"], l, False, None)
        mix_ctx = (_fourier(u_fn, 0, nbp, lp), y_ssd_c, _ctx_attn(u_qkv, nbp, lp))
        new_sf.append(sf_c)
        new_sb.append(sb_c)
        new_k.append(k_ctx.reshape(nbp, lp, NA_WIDTH))
        new_v.append(v_ctx.reshape(nbp, lp, NA_WIDTH))

        y_ssd_l, _, _ = _ssd(u_ssd, n_ctx, nbs, ls, p["ssd"], l, True, s0)
        mix_lat = (_fourier(u_fn, n_ctx, nbs, ls), y_ssd_l,
                   _na_attn(u_qkv, n_ctx, nbs, ls, cache_k, cache_v, l, na_bias))

        mix = (mix_ctx, mix_lat, (p["w_out_fn"], p["w_out_ssd"], p["w_out_att"]), norm_post[l, 1])
        xs = _ffn(xs, gm, 6, norm_pre[l, 2], norm_post[l, 2], w13, w2, l, 1, rpg, tm_ffn, mix=mix)

    return (xs[0].reshape(nbp, lp, D_MODEL), xs[1].reshape(nbs, ls, D_MODEL),
            jnp.stack(new_sf, axis=1), jnp.stack(new_sb, axis=1),
            jnp.stack(new_k, axis=1).reshape(nbp, DEPTH, lp, NA_HEADS, HEAD_DIM),
            jnp.stack(new_v, axis=1).reshape(nbp, DEPTH, lp, NA_HEADS, HEAD_DIM))
```

```python
import functools
import math

import numpy as np
import jax
import jax.numpy as jnp
from jax import lax
from jax.experimental import pallas as pl
from jax.experimental.pallas import tpu as pltpu

F32 = jnp.float32
BF16 = jnp.bfloat16

D_MODEL = 1024
DEPTH = 2
GRID_W = 64
FF_HIDDEN = 2816
N_MOD = 9
HEAD_DIM = 64
FN_WIDTH = 256
FN_GROUPS = 4
SSD_INNER = 384
SSD_HEADS = 6
SSD_STATE = 64
SSD_NGROUPS = 2
SSD_CONV = 5
SSD_CHUNK = 128
SSD_CONV_DIM = 640
SSD_IN = 1420
NA_WIDTH = 384
NA_HEADS = 6
NA_WIN_ROWS = 8
NA_WIN_COLS = 16
ROPE_BASE = 10000.0
EPS = 1e-6

LANES = 128
VMEM_LIMIT = 56 * 1024 * 1024

SLOTS = 8
SSD_PAD = SLOTS * HEAD_DIM
U_FN = FN_WIDTH
U_SSD = 2 * SSD_PAD + 2 * LANES + LANES
U_QKV = 3 * NA_WIDTH
U_TOTAL = U_FN + U_SSD + U_QKV
NEG_BIG = -1e30


def _slot_of_head(h):
    return 4 * (h // 3) + (h % 3)


def _cparams(sem):
    return pltpu.CompilerParams(dimension_semantics=sem, vmem_limit_bytes=VMEM_LIMIT)


def _rms(x):
    return x * lax.rsqrt(jnp.mean(x * x, axis=-1, keepdims=True) + EPS)


def _silu(x):
    return x * jax.nn.sigmoid(x)


def _bdot(a, b):
    return jnp.dot(a.astype(BF16), b.astype(BF16), preferred_element_type=F32)


def _bdot_nt(a, b):
    return lax.dot_general(a.astype(BF16), b.astype(BF16), (((1,), (1,)), ((), ())),
                           preferred_element_type=F32)


MOD_TN = 1152


def _mods_kernel(c_ref, w_ref, b_ref, o_ref):
    s = _silu(c_ref[...])
    o_ref[0] = _bdot(s, w_ref[0]) + b_ref[0]


def _mods(cvec, mod_w, mod_b):
    ncol = N_MOD * D_MODEL
    return pl.pallas_call(
        _mods_kernel,
        grid=(DEPTH, ncol // MOD_TN),
        in_specs=[pl.BlockSpec((8, D_MODEL), lambda l, j: (0, 0)),
                  pl.BlockSpec((1, D_MODEL, MOD_TN), lambda l, j: (l, 0, j)),
                  pl.BlockSpec((1, 1, MOD_TN), lambda l, j: (l, 0, j))],
        out_specs=pl.BlockSpec((1, 8, MOD_TN), lambda l, j: (l, 0, j)),
        out_shape=jax.ShapeDtypeStruct((DEPTH, 8, ncol), F32),
        compiler_params=_cparams(("parallel", "parallel")),
        name="mods",
    )(cvec, mod_w, mod_b.reshape(DEPTH, 1, ncol))


FFN_TH = 256


def _halves(nhalf):
    first = lambda i, *_: (jnp.minimum(i, nhalf - 1), 0)
    second = lambda i, *_: (jnp.maximum(i - nhalf, 0), 0)
    return first, second


def _on_half(nhalf, fn):
    i = pl.program_id(0)
    pl.when(i < nhalf)(functools.partial(fn, 0))
    pl.when(i >= nhalf)(functools.partial(fn, 1))


def _ffn_kernel(nhalf, mrow, with_mix, *refs):
    xa_ref, xb_ref, m_ref, gpre_ref, gpost_ref, w13_ref, w2_ref = refs[:7]
    pos = 7
    if with_mix:
        mix_refs = (refs[pos:pos + 3], refs[pos + 3:pos + 6])
        wmix_refs = refs[pos + 6:pos + 9]
        gmix_ref = refs[pos + 9]
        pos += 10
    oa_ref, ob_ref, h_ref, acc_ref = refs[pos:pos + 4]
    x_refs = (xa_ref, xb_ref)
    o_refs = (oa_ref, ob_ref)
    if with_mix:
        x1_ref = refs[pos + 4]

    def body(half):
        x = x_refs[half][...]
        if with_mix:
            y = None
            for y_ref, w_ref in zip(mix_refs[half], wmix_refs):
                d = jnp.dot(y_ref[...], w_ref[...], preferred_element_type=F32)
                y = d if y is None else y + d
            x = x + _rms(y) * (gmix_ref[...] * m_ref[0, mrow - 1:mrow, :])
            x1_ref[...] = x
        shift = m_ref[0, mrow:mrow + 1, :]
        scale = m_ref[0, mrow + 1:mrow + 2, :]
        h_ref[...] = (_rms(x) * (gpre_ref[...] * (1.0 + scale)) + shift).astype(BF16)

        h = h_ref[...]
        for j in range(FF_HIDDEN // FFN_TH):
            cols = slice(j * FFN_TH, (j + 1) * FFN_TH)
            g = jnp.dot(h, w13_ref[:, cols], preferred_element_type=F32)
            u = jnp.dot(h, w13_ref[:, FF_HIDDEN + j * FFN_TH:FF_HIDDEN + (j + 1) * FFN_TH],
                        preferred_element_type=F32)
            a = (_silu(g) * u).astype(BF16)
            part = jnp.dot(a, w2_ref[cols, :], preferred_element_type=F32)
            if j == 0:
                acc_ref[...] = part
            else:
                acc_ref[...] += part

        gate = m_ref[0, mrow + 2:mrow + 3, :]
        y = _rms(acc_ref[...]) * (gpost_ref[...] * (0.5 * gate))
        x = x1_ref[...] if with_mix else x_refs[half][...]
        o_refs[half][...] = x + y

    _on_half(nhalf, body)


def _ffn(xs, gmods, mrow, g_pre, g_post, w13, w2, layer, sub, rows_per_group, tm, mix=None):
    nh = xs[0].shape[0]
    nhalf = nh // tm
    tpg = rows_per_group // tm
    first, second = _halves(nhalf)
    resident = pl.Buffered(1)
    row = lambda a: a.reshape(1, D_MODEL)
    args = [xs[0], xs[1], gmods, row(g_pre), row(g_post), w13, w2]
    in_specs = [pl.BlockSpec((tm, D_MODEL), first),
                pl.BlockSpec((tm, D_MODEL), second),
                pl.BlockSpec((1, N_MOD, D_MODEL), lambda i: (i // tpg, 0, 0)),
                pl.BlockSpec((1, D_MODEL), lambda i: (0, 0)),
                pl.BlockSpec((1, D_MODEL), lambda i: (0, 0)),
                pl.BlockSpec((None, None, D_MODEL, 2 * FF_HIDDEN), lambda i: (layer, sub, 0, 0),
                             pipeline_mode=resident),
                pl.BlockSpec((None, None, FF_HIDDEN, D_MODEL), lambda i: (layer, sub, 0, 0),
                             pipeline_mode=resident)]
    scratch = [pltpu.VMEM((tm, D_MODEL), BF16), pltpu.VMEM((tm, D_MODEL), F32)]
    if mix is not None:
        mix_ctx, mix_lat, w_mix, g_post_mix = mix
        args += list(mix_ctx) + list(mix_lat) + list(w_mix) + [row(g_post_mix)]
        in_specs += ([pl.BlockSpec((tm, a.shape[1]), first) for a in mix_ctx]
                     + [pl.BlockSpec((tm, a.shape[1]), second) for a in mix_lat]
                     + [pl.BlockSpec((None,) + w.shape[1:], lambda i: (layer, 0, 0), pipeline_mode=resident)
                        for w in w_mix]
                     + [pl.BlockSpec((1, D_MODEL), lambda i: (0, 0))])
        scratch.append(pltpu.VMEM((tm, D_MODEL), F32))
    return pl.pallas_call(
        functools.partial(_ffn_kernel, nhalf, mrow, mix is not None),
        grid=(2 * nhalf,),
        in_specs=in_specs,
        out_specs=[pl.BlockSpec((tm, D_MODEL), first), pl.BlockSpec((tm, D_MODEL), second)],
        out_shape=[jax.ShapeDtypeStruct((nh, D_MODEL), F32)] * 2,
        scratch_shapes=scratch,
        compiler_params=_cparams(("arbitrary",)),
        name="ffn",
    )(*args)


def _inproj_kernel(nhalf, xa_ref, xb_ref, m_ref, gpre_ref, w_ref, ofn_ref, ossd_ref, oqkv_ref, ok_ref, ov_ref):
    x_refs = (xa_ref, xb_ref)

    def body(half):
        shift = m_ref[0, 0:1, :]
        scale = m_ref[0, 1:2, :]
        h = (_rms(x_refs[half][...]) * (gpre_ref[...] * (1.0 + scale)) + shift).astype(BF16)
        u = jnp.dot(h, w_ref[...], preferred_element_type=F32)
        ofn_ref[...] = u[:, :U_FN].astype(ofn_ref.dtype)
        ossd_ref[...] = u[:, U_FN:U_FN + U_SSD]
        oqkv_ref[...] = u[:, U_FN + U_SSD:].astype(oqkv_ref.dtype)
        if half == 0:
            ok_ref[...] = u[:, U_FN + U_SSD + NA_WIDTH:U_FN + U_SSD + 2 * NA_WIDTH]
            ov_ref[...] = u[:, U_FN + U_SSD + 2 * NA_WIDTH:]

    _on_half(nhalf, body)


def _inproj(xs, gmods, g_pre, w_in_pad, layer, rows_per_group, tm):
    nh = xs[0].shape[0]
    n = 2 * nh
    nhalf = nh // tm
    tpg = rows_per_group // tm
    first, second = _halves(nhalf)
    return pl.pallas_call(
        functools.partial(_inproj_kernel, nhalf),
        grid=(n // tm,),
        in_specs=[pl.BlockSpec((tm, D_MODEL), first),
                  pl.BlockSpec((tm, D_MODEL), second),
                  pl.BlockSpec((1, 3, D_MODEL), lambda i: (i // tpg, 0, 0)),
                  pl.BlockSpec((1, D_MODEL), lambda i: (0, 0)),
                  pl.BlockSpec((None, D_MODEL, U_TOTAL), lambda i: (layer, 0, 0))],
        out_specs=[pl.BlockSpec((tm, U_FN), lambda i: (i, 0)),
                   pl.BlockSpec((tm, U_SSD), lambda i: (i, 0)),
                   pl.BlockSpec((tm, U_QKV), lambda i: (i, 0)),
                   pl.BlockSpec((tm, NA_WIDTH), first),
                   pl.BlockSpec((tm, NA_WIDTH), first)],
        out_shape=[jax.ShapeDtypeStruct((n, U_FN), BF16),
                   jax.ShapeDtypeStruct((n, U_SSD), F32),
                   jax.ShapeDtypeStruct((n, U_QKV), BF16),
                   jax.ShapeDtypeStruct((nh, NA_WIDTH), F32),
                   jax.ShapeDtypeStruct((nh, NA_WIDTH), F32)],
        compiler_params=_cparams(("arbitrary",)),
        name="inproj",
    )(xs[0], xs[1], gmods, g_pre.reshape(1, D_MODEL), w_in_pad)


def _dft_tables(L):
    k = np.arange(L, dtype=np.int64)
    ang = 2.0 * np.pi * ((k[:, None] * k[None, :]) % L).astype(np.float64) / L
    sc = 1.0 / math.sqrt(L * HEAD_DIM)
    cl = (np.cos(ang) * sc).astype(np.float32)
    sl = (-np.sin(ang) * sc).astype(np.float32)
    m = np.arange(HEAD_DIM, dtype=np.int64)
    a64 = 2.0 * np.pi * ((m[:, None] * m[None, :]) % HEAD_DIM).astype(np.float64) / HEAD_DIM
    eye = np.eye(FN_GROUPS)
    w1 = np.concatenate([np.kron(eye, np.cos(a64)), np.kron(eye, np.sin(a64))], axis=1).astype(np.float32)
    return cl, sl, w1


def _fourier_kernel(nseq, L, tl, u_ref, w1_ref, cl_ref, sl_ref, o_ref, ab_ref):
    rt = pl.program_id(0)
    g = pl.program_id(1)

    @pl.when(rt == 0)
    def _():
        ab_ref[g] = _bdot(u_ref[...], w1_ref[...]).astype(BF16)

    for s in range(nseq):
        ab = ab_ref[g, s * L:(s + 1) * L, :]
        y = (jnp.dot(cl_ref[...], ab[:, :FN_WIDTH], preferred_element_type=F32)
             + jnp.dot(sl_ref[...], ab[:, FN_WIDTH:], preferred_element_type=F32))
        o_ref[s * tl:(s + 1) * tl, :] = y.astype(o_ref.dtype)


def _fourier(u_fn, row0, nb, L):
    cl, sl, w1 = _dft_tables(L)
    cl = jnp.asarray(cl).astype(BF16)
    sl = jnp.asarray(sl).astype(BF16)
    w1 = jnp.asarray(w1).astype(BF16)
    tl = min(L, 512)
    nrt = L // tl
    nseq = max(1, min(nb, 2048 // L)) if nrt == 1 else 1
    while nb % nseq:
        nseq -= 1
    ng = nb // nseq
    blk0 = row0 // (nseq * L)
    assert row0 % (nseq * L) == 0
    return pl.pallas_call(
        functools.partial(_fourier_kernel, nseq, L, tl),
        grid=(nrt, ng),
        in_specs=[pl.BlockSpec((nseq * L, FN_WIDTH), lambda rt, g: (blk0 + jnp.where(rt == 0, g, ng - 1), 0)),
                  pl.BlockSpec((FN_WIDTH, 2 * FN_WIDTH), lambda rt, g: (0, 0)),
                  pl.BlockSpec((tl, L), lambda rt, g: (rt, 0)),
                  pl.BlockSpec((tl, L), lambda rt, g: (rt, 0))],
        out_specs=pl.BlockSpec((nseq * tl, FN_WIDTH), lambda rt, g: (g * nrt + rt, 0)),
        out_shape=jax.ShapeDtypeStruct((nb * L, FN_WIDTH), BF16),
        scratch_shapes=[pltpu.VMEM((ng, nseq * L, 2 * FN_WIDTH), BF16)],
        compiler_params=_cparams(("arbitrary", "arbitrary")),
        name="fourier",
    )(u_fn, w1, cl, sl)


def _head_masks():
    lane = lax.broadcasted_iota(jnp.int32, (1, LANES), 1)
    return lane < HEAD_DIM


def _ctx_attn_kernel(nseq, L, q_ref, k_ref, v_ref, o_ref):
    low = _head_masks()
    for t in range(nseq):
        rows = slice(t * L, (t + 1) * L)
        for p in range(NA_HEADS // 2):
            lanes = slice(p * LANES, (p + 1) * LANES)
            q = q_ref[rows, lanes]
            k = k_ref[rows, lanes].astype(BF16)
            v = v_ref[rows, lanes].astype(BF16)
            q2 = jnp.concatenate([jnp.where(low, q, 0.0), jnp.where(low, 0.0, q)], axis=0)
            s = _bdot_nt(q2, k)
            m = jnp.max(s, axis=-1, keepdims=True)
            e = jnp.exp(s - m)
            inv = 1.0 / jnp.sum(e, axis=-1, keepdims=True)
            o2 = jnp.dot(e.astype(BF16), v, preferred_element_type=F32) * inv
            o_ref[rows, lanes] = jnp.where(low, o2[:L], o2[L:]).astype(o_ref.dtype)


def _ctx_attn(u_qkv, nb, L):
    nseq = 4 if nb % 4 == 0 else 1
    return pl.pallas_call(
        functools.partial(_ctx_attn_kernel, nseq, L),
        grid=(nb // nseq,),
        in_specs=[pl.BlockSpec((nseq * L, NA_WIDTH), lambda b: (b, 0)),
                  pl.BlockSpec((nseq * L, NA_WIDTH), lambda b: (b, 1)),
                  pl.BlockSpec((nseq * L, NA_WIDTH), lambda b: (b, 2))],
        out_specs=pl.BlockSpec((nseq * L, NA_WIDTH), lambda b: (b, 0)),
        out_shape=jax.ShapeDtypeStruct((nb * L, NA_WIDTH), BF16),
        compiler_params=_cparams(("parallel",)),
        name="ctx_attn",
    )(u_qkv, u_qkv, u_qkv)


NA_QR = 8
NA_QC = 16
NA_KR = 16
NA_KC = 32
NA_TQ = NA_QR * NA_QC
NA_TK = NA_KR * NA_KC


def _na_tile_geometry(rows):
    wr = min(NA_WIN_ROWS, rows)
    n_rb = rows // NA_QR
    n_cb = GRID_W // NA_QC
    rb = [0, 1, n_rb - 1]
    cb = [0, 1, n_cb - 1]
    dr = np.zeros((3, NA_QR, NA_KR), np.int64)
    vr = np.zeros((3, NA_QR, NA_KR), bool)
    for ci, i in enumerate(rb):
        kr0 = int(np.clip(NA_QR * i - NA_WIN_ROWS // 2, 0, rows - NA_KR))
        for rr in range(NA_QR):
            r = NA_QR * i + rr
            rs = int(np.clip(r - wr // 2, 0, rows - wr))
            for kk in range(NA_KR):
                kr = kr0 + kk
                vr[ci, rr, kk] = rs <= kr < rs + wr
                dr[ci, rr, kk] = np.clip(kr - r + NA_WIN_ROWS - 1, 0, 2 * NA_WIN_ROWS - 2)
    dc = np.zeros((3, NA_QC, NA_KC), np.int64)
    vc = np.zeros((3, NA_QC, NA_KC), bool)
    for ci, j in enumerate(cb):
        kc0 = int(np.clip(NA_QC * j - NA_WIN_COLS // 2, 0, GRID_W - NA_KC))
        for cq in range(NA_QC):
            c = NA_QC * j + cq
            cs = int(np.clip(c - NA_WIN_COLS // 2, 0, GRID_W - NA_WIN_COLS))
            for ck in range(NA_KC):
                kc = kc0 + ck
                vc[ci, cq, ck] = cs <= kc < cs + NA_WIN_COLS
                dc[ci, cq, ck] = np.clip(kc - c + NA_WIN_COLS - 1, 0, 2 * NA_WIN_COLS - 2)
    return dr, vr, dc, vc


def _na_bias_inputs(rpb, rows):
    dr, vr, dc, vc = _na_tile_geometry(rows)
    n_dc = 2 * NA_WIN_COLS - 1
    oh_c = (dc[..., None] == np.arange(n_dc)).astype(np.float32)
    oh_j = np.broadcast_to(oh_c[:, :, None], (3, NA_QC, NA_KR, NA_KC, n_dc)).reshape(3, NA_QC, NA_TK, n_dc)
    vq = jnp.einsum("lhab,ycjb->lhaycj", rpb.astype(F32), jnp.asarray(oh_j), precision=lax.Precision.HIGHEST)
    valid_c = np.broadcast_to(vc[:, :, None], (3, NA_QC, NA_KR, NA_KC)).reshape(3, NA_QC, NA_TK)
    vq = jnp.where(jnp.asarray(valid_c)[None, None, None], vq, NEG_BIG)
    drt = np.where(vr, dr, -1)[..., None]
    drt = np.broadcast_to(drt, (3, NA_QR, NA_KR, NA_KC)).reshape(3, NA_QR, NA_TK).astype(np.int32)
    offsets = [[sorted(set(dr[x, rr][vr[x, rr]].tolist())) for rr in range(NA_QR)] for x in range(3)]
    return vq, jnp.asarray(drt), offsets


def _na_kernel(rows, offsets, q_ref, k16_ref, v16_ref, kc_ref, vc_ref, vq_ref, drt_ref, o_ref,
               bias_ref, k_ref, v_ref):
    k_ref[...] = k16_ref[...].astype(F32)
    v_ref[...] = v16_ref[...].astype(F32)
    n_rb = rows // NA_QR
    n_cb = GRID_W // NA_QC
    low = _head_masks()
    kctx = kc_ref[0, 0].astype(BF16)
    vctx = vc_ref[0, 0].astype(BF16)

    @pl.when(pl.program_id(1) == 0)
    def _():
        def build(t, carry):
            hh = t // 3
            cc = t % 3
            for rc in range(3):
                for rr in range(NA_QR):
                    drrow = drt_ref[rc, rr:rr + 1, :]
                    acc = jnp.full((NA_QC, NA_TK), NEG_BIG, F32)
                    for a in offsets[rc][rr]:
                        acc = jnp.where(drrow == a, vq_ref[hh, a, cc], acc)
                    row0 = pl.multiple_of(hh * NA_TQ + rr * NA_QC, NA_QC)
                    bias_ref[rc, cc, pl.ds(row0, NA_QC), :] = acc
            return carry

        lax.fori_loop(0, 6, build, 0)

    def row_block(i, carry):
        kr0 = jnp.clip(NA_QR * i - NA_WIN_ROWS // 2, 0, rows - NA_KR)
        rcfg = jnp.where(i > 0, 1, 0) + jnp.where(i == n_rb - 1, 1, 0)
        for j in range(n_cb):
            kc0 = int(np.clip(NA_QC * j - NA_WIN_COLS // 2, 0, GRID_W - NA_KC))
            ccfg = 0 if j == 0 else (2 if j == n_cb - 1 else 1)
            q_parts = []
            for rr in range(NA_QR):
                start = pl.multiple_of((NA_QR * i + rr) * GRID_W + NA_QC * j, NA_QC)
                q_parts.append(q_ref[pl.ds(start, NA_QC), :])
            q = jnp.concatenate(q_parts, axis=0)
            k_parts, v_parts = [], []
            for kk in range(NA_KR):
                start = pl.multiple_of((kr0 + kk) * GRID_W + kc0, 8)
                k_parts.append(k_ref[pl.ds(start, NA_KC), :])
                v_parts.append(v_ref[pl.ds(start, NA_KC), :])
            k = jnp.concatenate(k_parts, axis=0).astype(BF16)
            v = jnp.concatenate(v_parts, axis=0).astype(BF16)
            q2 = jnp.concatenate([jnp.where(low, q, 0.0), jnp.where(low, 0.0, q)], axis=0).astype(BF16)
            s_loc = _bdot_nt(q2, k) + bias_ref[rcfg, ccfg]
            s_ctx = _bdot_nt(q2, kctx)
            m = jnp.maximum(jnp.max(s_loc, axis=-1, keepdims=True),
                            jnp.max(s_ctx, axis=-1, keepdims=True))
            p_loc = jnp.exp(s_loc - m)
            p_ctx = jnp.exp(s_ctx - m)
            l = jnp.sum(p_loc, axis=-1, keepdims=True) + jnp.sum(p_ctx, axis=-1, keepdims=True)
            o2 = (jnp.dot(p_loc.astype(BF16), v, preferred_element_type=F32)
                  + jnp.dot(p_ctx.astype(BF16), vctx, preferred_element_type=F32)) * (1.0 / l)
            o = jnp.where(low, o2[:NA_TQ], o2[NA_TQ:])
            for rr in range(NA_QR):
                start = pl.multiple_of((NA_QR * i + rr) * GRID_W + NA_QC * j, NA_QC)
                o_ref[pl.ds(start, NA_QC), :] = o[rr * NA_QC:(rr + 1) * NA_QC, :].astype(o_ref.dtype)
        return carry

    lax.fori_loop(0, n_rb, row_block, 0)


def _na_attn(u_qkv, row0, nb, L, cache_k, cache_v, layer, bias_inputs):
    npair = NA_HEADS // 2
    rows = L // GRID_W
    blk0 = row0 // L
    lc = cache_k.shape[2]
    vq, drt, offsets = bias_inputs
    return pl.pallas_call(
        functools.partial(_na_kernel, rows, offsets),
        grid=(npair, nb),
        in_specs=[pl.BlockSpec((L, LANES), lambda p, b: (blk0 + b, p)),
                  pl.BlockSpec((L, LANES), lambda p, b: (blk0 + b, npair + p)),
                  pl.BlockSpec((L, LANES), lambda p, b: (blk0 + b, 2 * npair + p)),
                  pl.BlockSpec((1, 1, lc, LANES), lambda p, b: (b, layer, 0, p)),
                  pl.BlockSpec((1, 1, lc, LANES), lambda p, b: (b, layer, 0, p)),
                  pl.BlockSpec((None, 2) + vq.shape[2:], lambda p, b: (layer, p, 0, 0, 0, 0)),
                  pl.BlockSpec(drt.shape, lambda p, b: (0, 0, 0))],
        out_specs=pl.BlockSpec((L, LANES), lambda p, b: (b, p)),
        out_shape=jax.ShapeDtypeStruct((nb * L, NA_WIDTH), BF16),
        scratch_shapes=[pltpu.VMEM((3, 3, 2 * NA_TQ, NA_TK), F32),
                        pltpu.VMEM((L, LANES), F32), pltpu.VMEM((L, LANES), F32)],
        compiler_params=_cparams(("arbitrary", "arbitrary")),
        name="na_attn",
    )(u_qkv, u_qkv, u_qkv, cache_k, cache_v, vq, drt)


def _ssd_constants():
    tril = np.tril(np.ones((SSD_CHUNK, SSD_CHUNK), np.float32))
    expand = np.zeros((2, LANES, SSD_PAD), np.float32)
    colb = np.zeros((2, LANES, SSD_HEADS * LANES), np.float32)
    for d in range(2):
        for h in range(SSD_HEADS):
            s = _slot_of_head(h)
            expand[d, SLOTS * d + s, HEAD_DIM * s:HEAD_DIM * (s + 1)] = 1.0
            colb[d, SLOTS * d + s, LANES * h:LANES * (h + 1)] = 1.0
    two = lambda m: np.concatenate([m, m], axis=-2)
    return np.concatenate([tril, tril], axis=1), two(expand), two(colb)


def _rope_tables(L):
    t = np.arange(L)
    rows = (t // GRID_W).astype(np.float64)
    cols = (t % GRID_W).astype(np.float64)
    quarter = SSD_STATE // 4
    inv = ROPE_BASE ** (-np.arange(quarter, dtype=np.float64) / quarter)
    n = np.arange(SSD_STATE)
    pos = np.where(n[None, :] < SSD_STATE // 2, rows[:, None], cols[:, None])
    ang = pos * inv[n % quarter][None, :]
    first = (n % (SSD_STATE // 2)) < quarter
    cos = np.cos(ang)
    sin = np.where(first[None, :], -np.sin(ang), np.sin(ang))
    tile = lambda a: np.concatenate([a] * SSD_NGROUPS, axis=1).astype(np.float32)
    return tile(cos), tile(sin)


def _split2(x):
    hi = x.astype(BF16)
    lo = (x - hi.astype(F32)).astype(BF16)
    return hi, lo


def _ssd_kernel(L, use_rope, use_init, *refs):
    (u_ref, convw_ref, convb_ref, dtb_ref, a_ref, ax_ref, d_ref, nw_ref,
     tril_ref, exp_ref, colb_ref) = refs[:11]
    pos = 11
    if use_rope:
        cos_ref, sin_ref = refs[pos:pos + 2]
        pos += 2
    if use_init:
        s0_refs = refs[pos:pos + 2]
        pos += 2
    y_ref = refs[pos]
    sfin_refs = refs[pos + 1:pos + 3]
    act_ref, dt_ref, s_ref, yacc_ref = refs[pos + 3:]

    nc = L // SSD_CHUNK
    C = SSD_CHUNK
    GW = SSD_PAD // SSD_NGROUPS
    XC = SSD_PAD
    CONV_W = SSD_PAD + 2 * LANES
    DTC = XC + CONV_W
    HALO = 8

    lane = lax.broadcasted_iota(jnp.int32, (1, LANES), 1)
    first_q = (lane % (SSD_STATE // 2)) < (SSD_STATE // 4)

    def prep(c, carry):
        r0 = pl.multiple_of(c * C, C)
        main = u_ref[pl.ds(r0, C), XC:XC + CONV_W]
        pstart = pl.multiple_of(jnp.maximum(r0 - HALO, 0), HALO)
        nstart = pl.multiple_of(jnp.minimum(r0 + C, L - HALO), HALO)
        prev = u_ref[pl.ds(pstart, HALO), XC:XC + CONV_W] * jnp.where(c > 0, 1.0, 0.0)
        nxt = u_ref[pl.ds(nstart, HALO), XC:XC + CONV_W] * jnp.where(c < nc - 1, 1.0, 0.0)
        win = jnp.concatenate([prev, main, nxt], axis=0)
        acc = jnp.zeros((C, CONV_W), F32) + convb_ref[...]
        for k in range(SSD_CONV):
            off = HALO + k - SSD_CONV // 2
            acc = acc + win[off:off + C, :] * convw_ref[k:k + 1, :]
        act = _silu(acc)
        act_ref[pl.ds(r0, C), 0:SSD_PAD] = act[:, 0:SSD_PAD]
        for t in range(2):
            bc = act[:, SSD_PAD + t * LANES:SSD_PAD + (t + 1) * LANES]
            if use_rope:
                partner = jnp.where(first_q, pltpu.roll(bc, LANES - SSD_STATE // 4, 1),
                                    pltpu.roll(bc, SSD_STATE // 4, 1))
                bc = bc * cos_ref[pl.ds(r0, C), :] + partner * sin_ref[pl.ds(r0, C), :]
            act_ref[pl.ds(r0, C), SSD_PAD + t * LANES:SSD_PAD + (t + 1) * LANES] = bc
        raw = u_ref[pl.ds(r0, C), DTC:DTC + LANES] + dtb_ref[...]
        dt_ref[pl.ds(r0, C), :] = jnp.maximum(raw, 0.0) + jnp.log1p(jnp.exp(-jnp.abs(raw)))
        yacc_ref[pl.ds(r0, C), :] = jnp.zeros((C, SSD_PAD), F32)
        return carry

    lax.fori_loop(0, nc, prep, 0)

    if use_init:
        zero = jnp.zeros((HEAD_DIM, SSD_STATE), F32)
        for d in range(2):
            for g in range(SSD_NGROUPS):
                rows_pn = []
                for hh in range(3):
                    blk = s0_refs[d][0, 0, 3 * g + hh]
                    rows_pn.append(jnp.concatenate([blk, zero] if g == 0 else [zero, blk], axis=1))
                rows_pn.append(jnp.zeros((HEAD_DIM, LANES), F32))
                s_ref[d, g] = jnp.concatenate(rows_pn, axis=0).T
    else:
        s_ref[...] = jnp.zeros_like(s_ref)

    a_row = a_ref[...]
    li = lax.broadcasted_iota(jnp.int32, (C, C), 0)
    si = lax.broadcasted_iota(jnp.int32, (C, C), 1)
    causal = [li >= si, si >= li]
    low64 = lane < HEAD_DIM
    grp_mask = [low64, jnp.logical_not(low64)]

    def one_direction(d, c):
        r0 = pl.multiple_of(c * C, C)
        x = act_ref[pl.ds(r0, C), 0:SSD_PAD]
        bmat = act_ref[pl.ds(r0, C), SSD_PAD:SSD_PAD + LANES]
        cmat = act_ref[pl.ds(r0, C), SSD_PAD + LANES:SSD_PAD + 2 * LANES]
        dt = dt_ref[pl.ds(r0, C), :]
        dta = dt * a_row
        hi, lo = _split2(dta)
        cs = jnp.dot(tril_ref[...], jnp.concatenate([hi, lo], axis=0),
                     preferred_element_type=F32)
        q = cs if d == 0 else cs - dta
        expand = lambda v: jnp.dot(jnp.concatenate(_split2(v), axis=1), exp_ref[d],
                                   preferred_element_type=F32)
        dt_x = expand(dt)
        cs_x = expand(cs)
        end_x = cs_x[C - 1:C, :]
        if d == 0:
            off_scale = jnp.exp(cs_x)
            w_state = jnp.exp(end_x - cs_x)
        else:
            e_x = cs_x - dt_x * ax_ref[d]
            off_scale = jnp.exp(end_x - e_x)
            w_state = jnp.exp(e_x)
        chunk_decay = jnp.exp(end_x)
        xdt = x * dt_x
        xdt_b = xdt.astype(BF16)
        rhs_state = (xdt * w_state).astype(BF16)
        qcol = jnp.dot(jnp.concatenate(_split2(q), axis=1), colb_ref[d],
                       preferred_element_type=F32)
        q_t = q.T
        b_t = bmat.T.astype(BF16)
        b_b = bmat.astype(BF16)
        cms = [jnp.where(grp_mask[g], cmat, 0.0).astype(BF16) for g in range(SSD_NGROUPS)]
        gmats = lax.dot_general(jnp.concatenate(cms, axis=0), b_b, (((1,), (1,)), ((), ())),
                                preferred_element_type=F32)
        st_all = jnp.dot(b_t, rhs_state, preferred_element_type=F32)
        for g in range(SSD_NGROUPS):
            gmat = gmats[g * C:(g + 1) * C]
            s_old = s_ref[d, g]
            y_off = (jnp.dot(cms[g], s_old.astype(BF16), preferred_element_type=F32)
                     * off_scale[:, g * GW:(g + 1) * GW])
            ms = []
            for hh in range(3):
                h = 3 * g + hh
                slot = 4 * g + hh
                row = q_t[SLOTS * d + slot:SLOTS * d + slot + 1, :]
                col = qcol[:, h * LANES:(h + 1) * LANES]
                seg = (col - row) if d == 0 else (row - col)
                ms.append((gmat * jnp.exp(jnp.where(causal[d], seg, NEG_BIG))).astype(BF16))
            r01 = jnp.dot(jnp.concatenate(ms[:2], axis=0), xdt_b[:, 2 * g * LANES:(2 * g + 1) * LANES],
                          preferred_element_type=F32)
            r2 = jnp.dot(ms[2], xdt_b[:, (2 * g + 1) * LANES:(2 * g + 2) * LANES],
                         preferred_element_type=F32)
            y_g = jnp.concatenate([jnp.where(low64, r01[:C], r01[C:]), r2], axis=1) + y_off
            yacc_ref[pl.ds(r0, C), g * GW:(g + 1) * GW] += y_g
            s_ref[d, g] = (s_old * chunk_decay[:, g * GW:(g + 1) * GW]
                           + st_all[:, g * GW:(g + 1) * GW])

    def scan(i, carry):
        one_direction(0, i)
        one_direction(1, nc - 1 - i)
        return carry

    lax.fori_loop(0, nc, scan, 0, unroll=2)
    for d in range(2):
        for g in range(SSD_NGROUPS):
            s_t = s_ref[d, g].T
            for hh in range(3):
                sfin_refs[d][0, 3 * g + hh] = s_t[hh * HEAD_DIM:(hh + 1) * HEAD_DIM,
                                                  g * SSD_STATE:(g + 1) * SSD_STATE]

    def finish(c, carry):
        r0 = pl.multiple_of(c * C, C)
        y = yacc_ref[pl.ds(r0, C), :] + act_ref[pl.ds(r0, C), 0:SSD_PAD] * d_ref[...]
        y = y * _silu(u_ref[pl.ds(r0, C), 0:SSD_PAD])
        for g in range(SSD_NGROUPS):
            yg = y[:, g * GW:(g + 1) * GW]
            ms = jnp.sum(yg * yg, axis=-1, keepdims=True) * (1.0 / (SSD_INNER // SSD_NGROUPS))
            yn = yg * lax.rsqrt(ms + EPS) * nw_ref[:, g * GW:(g + 1) * GW]
            y_ref[pl.ds(r0, C), g * GW:(g + 1) * GW] = yn.astype(y_ref.dtype)
        return carry

    lax.fori_loop(0, nc, finish, 0)


def _ssd(u_ssd, row0, nb, L, prm, layer, use_rope, s0):
    blk0 = row0 // L
    tril2, exp2, colb2 = _ssd_constants()
    consts = [jnp.asarray(tril2, dtype=BF16), jnp.asarray(exp2, dtype=BF16), jnp.asarray(colb2, dtype=BF16)]
    full = lambda a: pl.BlockSpec(a.shape, lambda b, _n=a.ndim: (0,) * _n)
    of_layer = lambda a: pl.BlockSpec((None,) + a.shape[1:], lambda b, _n=a.ndim: (layer,) + (0,) * (_n - 1))
    per_layer = [prm[k] for k in ("conv_w", "conv_b", "dt_bias", "a_row", "a_x", "d_row", "norm_w")]
    args = [u_ssd] + per_layer + consts
    in_specs = ([pl.BlockSpec((L, U_SSD), lambda b: (blk0 + b, 0))] + [of_layer(a) for a in per_layer]
                + [full(a) for a in consts])
    if use_rope:
        cos, sin = _rope_tables(L)
        tabs = [jnp.asarray(cos), jnp.asarray(sin)]
        args += tabs
        in_specs += [full(a) for a in tabs]
    state_block = (SSD_HEADS, HEAD_DIM, SSD_STATE)
    if s0 is not None:
        args += list(s0)
        in_specs += [pl.BlockSpec((1, 1) + state_block, lambda b: (b, layer, 0, 0, 0))] * 2
    sshape = (2, SSD_NGROUPS, LANES, SSD_PAD // SSD_NGROUPS)
    return pl.pallas_call(
        functools.partial(_ssd_kernel, L, use_rope, s0 is not None),
        grid=(nb,),
        in_specs=in_specs,
        out_specs=[pl.BlockSpec((L, SSD_PAD), lambda b: (b, 0))]
                  + [pl.BlockSpec((1,) + state_block, lambda b: (b, 0, 0, 0))] * 2,
        out_shape=[jax.ShapeDtypeStruct((nb * L, SSD_PAD), BF16)]
                  + [jax.ShapeDtypeStruct((nb,) + state_block, F32)] * 2,
        scratch_shapes=[pltpu.VMEM((L, SSD_PAD + 2 * LANES), F32),
                        pltpu.VMEM((L, LANES), F32),
                        pltpu.VMEM(sshape, F32),
                        pltpu.VMEM((L, SSD_PAD), F32)],
        compiler_params=_cparams(("parallel",)),
        name="ssd",
    )(*args)


def _pad_heads(a, axis=-1):
    a = jnp.moveaxis(a, axis, -1)
    lead = a.shape[:-1]
    a = a.reshape(lead + (SSD_NGROUPS, 3, HEAD_DIM))
    a = jnp.pad(a, [(0, 0)] * len(lead) + [(0, 0), (0, 1), (0, 0)])
    return jnp.moveaxis(a.reshape(lead + (SSD_PAD,)), -1, axis)


def _pad_dt_lanes(a):
    lead = a.shape[:-2]
    a = a.reshape(lead + (2, SSD_NGROUPS, 3))
    a = jnp.pad(a, [(0, 0)] * len(lead) + [(0, 0), (0, 0), (0, 1)]).reshape(lead + (2 * SLOTS,))
    return jnp.pad(a, [(0, 0)] * len(lead) + [(0, LANES - 2 * SLOTS)])


def _mixer_params(w_in, w_out, ssd_conv_w, ssd_conv_b, ssd_dt_bias, ssd_a_log, ssd_d, ssd_norm):
    w_in = w_in.astype(BF16)
    o = FN_WIDTH
    z = _pad_heads(w_in[..., o:o + SSD_INNER])
    o += SSD_INNER
    xw = _pad_heads(w_in[..., o:o + SSD_INNER])
    bc = w_in[..., o + SSD_INNER:o + SSD_CONV_DIM]
    o += SSD_CONV_DIM
    dtw = _pad_dt_lanes(w_in[..., o:o + 2 * SSD_HEADS].reshape(DEPTH, D_MODEL, 2, SSD_HEADS))
    o += 2 * SSD_HEADS
    qw = w_in[..., o:o + NA_WIDTH] * (HEAD_DIM ** -0.5)
    kv = w_in[..., o + NA_WIDTH:]
    w_in_pad = jnp.concatenate([w_in[..., :FN_WIDTH], z, xw, bc, dtw, qw, kv], axis=-1).astype(BF16)
    a = -jnp.exp(ssd_a_log.astype(F32))
    a_x = jnp.repeat(_pad_dt_lanes(a)[:, :2 * SLOTS].reshape(DEPTH, 2, SLOTS), HEAD_DIM, axis=-1)
    ssd = {
        "conv_w": jnp.concatenate([_pad_heads(ssd_conv_w[..., :SSD_INNER]), ssd_conv_w[..., SSD_INNER:]], axis=-1),
        "conv_b": jnp.concatenate([_pad_heads(ssd_conv_b[..., :SSD_INNER]),
                                   ssd_conv_b[..., SSD_INNER:]], axis=-1)[:, None, :],
        "dt_bias": _pad_dt_lanes(ssd_dt_bias)[:, None, :],
        "a_row": _pad_dt_lanes(a)[:, None, :],
        "a_x": a_x.reshape(DEPTH, 2, 1, SSD_PAD),
        "d_row": _pad_heads(jnp.repeat(ssd_d, HEAD_DIM, axis=-1))[:, None, :],
        "norm_w": _pad_heads(ssd_norm)[:, None, :],
    }
    return {
        "w_in": w_in_pad,
        "w_out_fn": w_out[:, :FN_WIDTH].astype(BF16),
        "w_out_ssd": _pad_heads(w_out[:, FN_WIDTH:FN_WIDTH + SSD_INNER], axis=1).astype(BF16),
        "w_out_att": w_out[:, FN_WIDTH + SSD_INNER:].astype(BF16),
        "ssd": ssd,
    }


def _pick_tile(rows, want):
    t = min(rows, want)
    while rows % t:
        t //= 2
    return t


def kernel(x_prompt, x_sample, c, state_ssd_fwd, state_ssd_bwd, cache_attn_k, cache_attn_v, c_ctx, mod_w, mod_b, norm_pre, norm_post, ffn_w13, ffn_w2, w_in, w_out, ssd_conv_w, ssd_conv_b, ssd_dt_bias, ssd_a_log, ssd_d, ssd_norm, na_rpb):
    nbp, lp, _ = x_prompt.shape
    nbs, ls, _ = x_sample.shape
    n_ctx, n_lat = nbp * lp, nbs * ls
    assert n_ctx == n_lat and n_ctx % ls == 0 and nbs + 1 <= 8
    rpg = ls
    xs = (x_prompt.reshape(n_ctx, D_MODEL), x_sample.reshape(n_lat, D_MODEL))

    cvec = jnp.zeros((8, D_MODEL), F32).at[0].set(c_ctx).at[1:1 + nbs].set(c)
    mods = _mods(cvec, mod_w, mod_b).reshape(DEPTH, 8, N_MOD, D_MODEL)
    w13 = ffn_w13.astype(BF16)
    w2 = ffn_w2.astype(BF16)
    p = _mixer_params(w_in, w_out, ssd_conv_w, ssd_conv_b, ssd_dt_bias, ssd_a_log, ssd_d, ssd_norm)
    cache_k = cache_attn_k.reshape(nbs, DEPTH, cache_attn_k.shape[2], NA_WIDTH)
    cache_v = cache_attn_v.reshape(nbs, DEPTH, cache_attn_v.shape[2], NA_WIDTH)
    s0 = (state_ssd_fwd.astype(F32), state_ssd_bwd.astype(F32))
    na_bias = _na_bias_inputs(na_rpb, ls // GRID_W)
    tm_ffn = _pick_tile(rpg, 512)
    tm_proj = _pick_tile(rpg, 512)

    new_sf, new_sb, new_k, new_v = [], [], [], []
    for l in range(DEPTH):
        gm = jnp.concatenate([jnp.broadcast_to(mods[l, 0], (n_ctx // rpg, N_MOD, D_MODEL)),
                              mods[l, 1:1 + nbs]], axis=0)
        xs = _ffn(xs, gm, 0, norm_pre[l, 0], norm_post[l, 0], w13, w2, l, 0, rpg, tm_ffn)
        u_fn, u_ssd, u_qkv, k_ctx, v_ctx = _inproj(xs, gm[:, 3:6], norm_pre[l, 1], p["w_in"], l, rpg, tm_proj)

        y_ssd_c, sf_c, sb_c = _ssd(u_ssd, 0, nbp, lp, p["ssd"], l, False, None)
        mix_ctx = (_fourier(u_fn, 0, nbp, lp), y_ssd_c, _ctx_attn(u_qkv, nbp, lp))
        new_sf.append(sf_c)
        new_sb.append(sb_c)
        new_k.append(k_ctx.reshape(nbp, lp, NA_WIDTH))
        new_v.append(v_ctx.reshape(nbp, lp, NA_WIDTH))

        y_ssd_l, _, _ = _ssd(u_ssd, n_ctx, nbs, ls, p["ssd"], l, True, s0)
        mix_lat = (_fourier(u_fn, n_ctx, nbs, ls), y_ssd_l,
                   _na_attn(u_qkv, n_ctx, nbs, ls, cache_k, cache_v, l, na_bias))

        mix = (mix_ctx, mix_lat, (p["w_out_fn"], p["w_out_ssd"], p["w_out_att"]), norm_post[l, 1])
        xs = _ffn(xs, gm, 6, norm_pre[l, 2], norm_post[l, 2], w13, w2, l, 1, rpg, tm_ffn, mix=mix)

    return (xs[0].reshape(nbp, lp, D_MODEL), xs[1].reshape(nbs, ls, D_MODEL),
            jnp.stack(new_sf, axis=1), jnp.stack(new_sb, axis=1),
            jnp.stack(new_k, axis=1).reshape(nbp, DEPTH, lp, NA_HEADS, HEAD_DIM),
            jnp.stack(new_v, axis=1).reshape(nbp, DEPTH, lp, NA_HEADS, HEAD_DIM))
```

```python
import functools
import math

import numpy as np
import jax
import jax.numpy as jnp
from jax import lax
from jax.experimental import pallas as pl
from jax.experimental.pallas import tpu as pltpu

F32 = jnp.float32
BF16 = jnp.bfloat16

D_MODEL = 1024
DEPTH = 2
GRID_W = 64
FF_HIDDEN = 2816
N_MOD = 9
HEAD_DIM = 64
FN_WIDTH = 256
FN_GROUPS = 4
SSD_INNER = 384
SSD_HEADS = 6
SSD_STATE = 64
SSD_NGROUPS = 2
SSD_CONV = 5
SSD_CHUNK = 128
SSD_CONV_DIM = 640
SSD_IN = 1420
NA_WIDTH = 384
NA_HEADS = 6
NA_WIN_ROWS = 8
NA_WIN_COLS = 16
ROPE_BASE = 10000.0
EPS = 1e-6

LANES = 128
VMEM_LIMIT = 56 * 1024 * 1024

SLOTS = 8
SSD_PAD = SLOTS * HEAD_DIM
U_FN = FN_WIDTH
U_SSD = 2 * SSD_PAD + 2 * LANES + LANES
U_QKV = 3 * NA_WIDTH
U_TOTAL = U_FN + U_SSD + U_QKV
NEG_BIG = -1e30


def _slot_of_head(h):
    return 4 * (h // 3) + (h % 3)


def _cparams(sem):
    return pltpu.CompilerParams(dimension_semantics=sem, vmem_limit_bytes=VMEM_LIMIT)


def _rms(x):
    return x * lax.rsqrt(jnp.mean(x * x, axis=-1, keepdims=True) + EPS)


def _silu(x):
    return x * jax.nn.sigmoid(x)


def _bdot(a, b):
    return jnp.dot(a.astype(BF16), b.astype(BF16), preferred_element_type=F32)


def _bdot_nt(a, b):
    return lax.dot_general(a.astype(BF16), b.astype(BF16), (((1,), (1,)), ((), ())),
                           preferred_element_type=F32)


MOD_TN = 1152


def _mods_kernel(c_ref, w_ref, b_ref, o_ref):
    s = _silu(c_ref[...])
    o_ref[0] = _bdot(s, w_ref[0]) + b_ref[0]


def _mods(cvec, mod_w, mod_b):
    ncol = N_MOD * D_MODEL
    return pl.pallas_call(
        _mods_kernel,
        grid=(DEPTH, ncol // MOD_TN),
        in_specs=[pl.BlockSpec((8, D_MODEL), lambda l, j: (0, 0)),
                  pl.BlockSpec((1, D_MODEL, MOD_TN), lambda l, j: (l, 0, j)),
                  pl.BlockSpec((1, 1, MOD_TN), lambda l, j: (l, 0, j))],
        out_specs=pl.BlockSpec((1, 8, MOD_TN), lambda l, j: (l, 0, j)),
        out_shape=jax.ShapeDtypeStruct((DEPTH, 8, ncol), F32),
        compiler_params=_cparams(("parallel", "parallel")),
        name="mods",
    )(cvec, mod_w, mod_b.reshape(DEPTH, 1, ncol))


FFN_TH = 256


def _halves(nhalf):
    first = lambda i, *_: (jnp.minimum(i, nhalf - 1), 0)
    second = lambda i, *_: (jnp.maximum(i - nhalf, 0), 0)
    return first, second


def _on_half(nhalf, fn):
    i = pl.program_id(0)
    pl.when(i < nhalf)(functools.partial(fn, 0))
    pl.when(i >= nhalf)(functools.partial(fn, 1))


def _ffn_kernel(nhalf, mrow, with_mix, *refs):
    xa_ref, xb_ref, m_ref, gpre_ref, gpost_ref, w13_ref, w2_ref = refs[:7]
    pos = 7
    if with_mix:
        mix_refs = (refs[pos:pos + 3], refs[pos + 3:pos + 6])
        wmix_refs = refs[pos + 6:pos + 9]
        gmix_ref = refs[pos + 9]
        pos += 10
    oa_ref, ob_ref, h_ref, acc_ref = refs[pos:pos + 4]
    x_refs = (xa_ref, xb_ref)
    o_refs = (oa_ref, ob_ref)
    if with_mix:
        x1_ref = refs[pos + 4]

    def body(half):
        x = x_refs[half][...]
        if with_mix:
            y = None
            for y_ref, w_ref in zip(mix_refs[half], wmix_refs):
                d = jnp.dot(y_ref[...], w_ref[...], preferred_element_type=F32)
                y = d if y is None else y + d
            x = x + _rms(y) * (gmix_ref[...] * m_ref[0, mrow - 1:mrow, :])
            x1_ref[...] = x
        shift = m_ref[0, mrow:mrow + 1, :]
        scale = m_ref[0, mrow + 1:mrow + 2, :]
        h_ref[...] = (_rms(x) * (gpre_ref[...] * (1.0 + scale)) + shift).astype(BF16)

        h = h_ref[...]
        for j in range(FF_HIDDEN // FFN_TH):
            cols = slice(j * FFN_TH, (j + 1) * FFN_TH)
            g = jnp.dot(h, w13_ref[:, cols], preferred_element_type=F32)
            u = jnp.dot(h, w13_ref[:, FF_HIDDEN + j * FFN_TH:FF_HIDDEN + (j + 1) * FFN_TH],
                        preferred_element_type=F32)
            a = (_silu(g) * u).astype(BF16)
            part = jnp.dot(a, w2_ref[cols, :], preferred_element_type=F32)
            if j == 0:
                acc_ref[...] = part
            else:
                acc_ref[...] += part

        gate = m_ref[0, mrow + 2:mrow + 3, :]
        y = _rms(acc_ref[...]) * (gpost_ref[...] * (0.5 * gate))
        x = x1_ref[...] if with_mix else x_refs[half][...]
        o_refs[half][...] = x + y

    _on_half(nhalf, body)


def _ffn(xs, gmods, mrow, g_pre, g_post, w13, w2, layer, sub, rows_per_group, tm, mix=None):
    nh = xs[0].shape[0]
    nhalf = nh // tm
    tpg = rows_per_group // tm
    first, second = _halves(nhalf)
    resident = pl.Buffered(1)
    row = lambda a: a.reshape(1, D_MODEL)
    args = [xs[0], xs[1], gmods, row(g_pre), row(g_post), w13, w2]
    in_specs = [pl.BlockSpec((tm, D_MODEL), first),
                pl.BlockSpec((tm, D_MODEL), second),
                pl.BlockSpec((1, N_MOD, D_MODEL), lambda i: (i // tpg, 0, 0)),
                pl.BlockSpec((1, D_MODEL), lambda i: (0, 0)),
                pl.BlockSpec((1, D_MODEL), lambda i: (0, 0)),
                pl.BlockSpec((None, None, D_MODEL, 2 * FF_HIDDEN), lambda i: (layer, sub, 0, 0),
                             pipeline_mode=resident),
                pl.BlockSpec((None, None, FF_HIDDEN, D_MODEL), lambda i: (layer, sub, 0, 0),
                             pipeline_mode=resident)]
    scratch = [pltpu.VMEM((tm, D_MODEL), BF16), pltpu.VMEM((tm, D_MODEL), F32)]
    if mix is not None:
        mix_ctx, mix_lat, w_mix, g_post_mix = mix
        args += list(mix_ctx) + list(mix_lat) + list(w_mix) + [row(g_post_mix)]
        in_specs += ([pl.BlockSpec((tm, a.shape[1]), first) for a in mix_ctx]
                     + [pl.BlockSpec((tm, a.shape[1]), second) for a in mix_lat]
                     + [pl.BlockSpec((None,) + w.shape[1:], lambda i: (layer, 0, 0), pipeline_mode=resident)
                        for w in w_mix]
                     + [pl.BlockSpec((1, D_MODEL), lambda i: (0, 0))])
        scratch.append(pltpu.VMEM((tm, D_MODEL), F32))
    return pl.pallas_call(
        functools.partial(_ffn_kernel, nhalf, mrow, mix is not None),
        grid=(2 * nhalf,),
        in_specs=in_specs,
        out_specs=[pl.BlockSpec((tm, D_MODEL), first), pl.BlockSpec((tm, D_MODEL), second)],
        out_shape=[jax.ShapeDtypeStruct((nh, D_MODEL), F32)] * 2,
        scratch_shapes=scratch,
        compiler_params=_cparams(("arbitrary",)),
        name="ffn",
    )(*args)


def _inproj_kernel(nhalf, xa_ref, xb_ref, m_ref, gpre_ref, w_ref, ofn_ref, ossd_ref, oqkv_ref, ok_ref, ov_ref):
    x_refs = (xa_ref, xb_ref)

    def body(half):
        shift = m_ref[0, 0:1, :]
        scale = m_ref[0, 1:2, :]
        h = (_rms(x_refs[half][...]) * (gpre_ref[...] * (1.0 + scale)) + shift).astype(BF16)
        u = jnp.dot(h, w_ref[...], preferred_element_type=F32)
        ofn_ref[...] = u[:, :U_FN].astype(ofn_ref.dtype)
        ossd_ref[...] = u[:, U_FN:U_FN + U_SSD]
        oqkv_ref[...] = u[:, U_FN + U_SSD:].astype(oqkv_ref.dtype)
        if half == 0:
            ok_ref[...] = u[:, U_FN + U_SSD + NA_WIDTH:U_FN + U_SSD + 2 * NA_WIDTH]
            ov_ref[...] = u[:, U_FN + U_SSD + 2 * NA_WIDTH:]

    _on_half(nhalf, body)


def _inproj(xs, gmods, g_pre, w_in_pad, layer, rows_per_group, tm):
    nh = xs[0].shape[0]
    n = 2 * nh
    nhalf = nh // tm
    tpg = rows_per_group // tm
    first, second = _halves(nhalf)
    return pl.pallas_call(
        functools.partial(_inproj_kernel, nhalf),
        grid=(n // tm,),
        in_specs=[pl.BlockSpec((tm, D_MODEL), first),
                  pl.BlockSpec((tm, D_MODEL), second),
                  pl.BlockSpec((1, 3, D_MODEL), lambda i: (i // tpg, 0, 0)),
                  pl.BlockSpec((1, D_MODEL), lambda i: (0, 0)),
                  pl.BlockSpec((None, D_MODEL, U_TOTAL), lambda i: (layer, 0, 0))],
        out_specs=[pl.BlockSpec((tm, U_FN), lambda i: (i, 0)),
                   pl.BlockSpec((tm, U_SSD), lambda i: (i, 0)),
                   pl.BlockSpec((tm, U_QKV), lambda i: (i, 0)),
                   pl.BlockSpec((tm, NA_WIDTH), first),
                   pl.BlockSpec((tm, NA_WIDTH), first)],
        out_shape=[jax.ShapeDtypeStruct((n, U_FN), BF16),
                   jax.ShapeDtypeStruct((n, U_SSD), F32),
                   jax.ShapeDtypeStruct((n, U_QKV), BF16),
                   jax.ShapeDtypeStruct((nh, NA_WIDTH), F32),
                   jax.ShapeDtypeStruct((nh, NA_WIDTH), F32)],
        compiler_params=_cparams(("arbitrary",)),
        name="inproj",
    )(xs[0], xs[1], gmods, g_pre.reshape(1, D_MODEL), w_in_pad)


def _dft_tables(L):
    k = np.arange(L, dtype=np.int64)
    ang = 2.0 * np.pi * ((k[:, None] * k[None, :]) % L).astype(np.float64) / L
    sc = 1.0 / math.sqrt(L * HEAD_DIM)
    cl = (np.cos(ang) * sc).astype(np.float32)
    sl = (-np.sin(ang) * sc).astype(np.float32)
    m = np.arange(HEAD_DIM, dtype=np.int64)
    a64 = 2.0 * np.pi * ((m[:, None] * m[None, :]) % HEAD_DIM).astype(np.float64) / HEAD_DIM
    eye = np.eye(FN_GROUPS)
    w1 = np.concatenate([np.kron(eye, np.cos(a64)), np.kron(eye, np.sin(a64))], axis=1).astype(np.float32)
    return cl, sl, w1


def _fourier_kernel(nseq, L, tl, u_ref, w1_ref, cl_ref, sl_ref, o_ref, ab_ref):
    rt = pl.program_id(0)
    g = pl.program_id(1)

    @pl.when(rt == 0)
    def _():
        ab_ref[g] = _bdot(u_ref[...], w1_ref[...]).astype(BF16)

    for s in range(nseq):
        ab = ab_ref[g, s * L:(s + 1) * L, :]
        y = (jnp.dot(cl_ref[...], ab[:, :FN_WIDTH], preferred_element_type=F32)
             + jnp.dot(sl_ref[...], ab[:, FN_WIDTH:], preferred_element_type=F32))
        o_ref[s * tl:(s + 1) * tl, :] = y.astype(o_ref.dtype)


def _fourier(u_fn, row0, nb, L):
    cl, sl, w1 = _dft_tables(L)
    cl = jnp.asarray(cl).astype(BF16)
    sl = jnp.asarray(sl).astype(BF16)
    w1 = jnp.asarray(w1).astype(BF16)
    tl = min(L, 1024)
    nrt = L // tl
    nseq = max(1, min(nb, 2048 // L)) if nrt == 1 else 1
    while nb % nseq:
        nseq -= 1
    ng = nb // nseq
    blk0 = row0 // (nseq * L)
    assert row0 % (nseq * L) == 0
    return pl.pallas_call(
        functools.partial(_fourier_kernel, nseq, L, tl),
        grid=(nrt, ng),
        in_specs=[pl.BlockSpec((nseq * L, FN_WIDTH), lambda rt, g: (blk0 + jnp.where(rt == 0, g, ng - 1), 0)),
                  pl.BlockSpec((FN_WIDTH, 2 * FN_WIDTH), lambda rt, g: (0, 0)),
                  pl.BlockSpec((tl, L), lambda rt, g: (rt, 0)),
                  pl.BlockSpec((tl, L), lambda rt, g: (rt, 0))],
        out_specs=pl.BlockSpec((nseq * tl, FN_WIDTH), lambda rt, g: (g * nrt + rt, 0)),
        out_shape=jax.ShapeDtypeStruct((nb * L, FN_WIDTH), BF16),
        scratch_shapes=[pltpu.VMEM((ng, nseq * L, 2 * FN_WIDTH), BF16)],
        compiler_params=_cparams(("arbitrary", "arbitrary")),
        name="fourier",
    )(u_fn, w1, cl, sl)


def _head_masks():
    lane = lax.broadcasted_iota(jnp.int32, (1, LANES), 1)
    return lane < HEAD_DIM


def _ctx_attn_kernel(nseq, L, q_ref, k_ref, v_ref, o_ref):
    low = _head_masks()
    for t in range(nseq):
        rows = slice(t * L, (t + 1) * L)
        for p in range(NA_HEADS // 2):
            lanes = slice(p * LANES, (p + 1) * LANES)
            q = q_ref[rows, lanes]
            k = k_ref[rows, lanes].astype(BF16)
            v = v_ref[rows, lanes].astype(BF16)
            q2 = jnp.concatenate([jnp.where(low, q, 0.0), jnp.where(low, 0.0, q)], axis=0)
            s = _bdot_nt(q2, k)
            m = jnp.max(s, axis=-1, keepdims=True)
            e = jnp.exp(s - m)
            inv = 1.0 / jnp.sum(e, axis=-1, keepdims=True)
            o2 = jnp.dot(e.astype(BF16), v, preferred_element_type=F32) * inv
            o_ref[rows, lanes] = jnp.where(low, o2[:L], o2[L:]).astype(o_ref.dtype)


def _ctx_attn(u_qkv, nb, L):
    nseq = 4 if nb % 4 == 0 else 1
    return pl.pallas_call(
        functools.partial(_ctx_attn_kernel, nseq, L),
        grid=(nb // nseq,),
        in_specs=[pl.BlockSpec((nseq * L, NA_WIDTH), lambda b: (b, 0)),
                  pl.BlockSpec((nseq * L, NA_WIDTH), lambda b: (b, 1)),
                  pl.BlockSpec((nseq * L, NA_WIDTH), lambda b: (b, 2))],
        out_specs=pl.BlockSpec((nseq * L, NA_WIDTH), lambda b: (b, 0)),
        out_shape=jax.ShapeDtypeStruct((nb * L, NA_WIDTH), BF16),
        compiler_params=_cparams(("parallel",)),
        name="ctx_attn",
    )(u_qkv, u_qkv, u_qkv)


NA_QR = 8
NA_QC = 16
NA_KR = 16
NA_KC = 32
NA_TQ = NA_QR * NA_QC
NA_TK = NA_KR * NA_KC


def _na_tile_geometry(rows):
    wr = min(NA_WIN_ROWS, rows)
    n_rb = rows // NA_QR
    n_cb = GRID_W // NA_QC
    rb = [0, 1, n_rb - 1]
    cb = [0, 1, n_cb - 1]
    dr = np.zeros((3, NA_QR, NA_KR), np.int64)
    vr = np.zeros((3, NA_QR, NA_KR), bool)
    for ci, i in enumerate(rb):
        kr0 = int(np.clip(NA_QR * i - NA_WIN_ROWS // 2, 0, rows - NA_KR))
        for rr in range(NA_QR):
            r = NA_QR * i + rr
            rs = int(np.clip(r - wr // 2, 0, rows - wr))
            for kk in range(NA_KR):
                kr = kr0 + kk
                vr[ci, rr, kk] = rs <= kr < rs + wr
                dr[ci, rr, kk] = np.clip(kr - r + NA_WIN_ROWS - 1, 0, 2 * NA_WIN_ROWS - 2)
    dc = np.zeros((3, NA_QC, NA_KC), np.int64)
    vc = np.zeros((3, NA_QC, NA_KC), bool)
    for ci, j in enumerate(cb):
        kc0 = int(np.clip(NA_QC * j - NA_WIN_COLS // 2, 0, GRID_W - NA_KC))
        for cq in range(NA_QC):
            c = NA_QC * j + cq
            cs = int(np.clip(c - NA_WIN_COLS // 2, 0, GRID_W - NA_WIN_COLS))
            for ck in range(NA_KC):
                kc = kc0 + ck
                vc[ci, cq, ck] = cs <= kc < cs + NA_WIN_COLS
                dc[ci, cq, ck] = np.clip(kc - c + NA_WIN_COLS - 1, 0, 2 * NA_WIN_COLS - 2)
    return dr, vr, dc, vc


def _na_bias_inputs(rpb, rows):
    dr, vr, dc, vc = _na_tile_geometry(rows)
    n_dc = 2 * NA_WIN_COLS - 1
    oh_c = (dc[..., None] == np.arange(n_dc)).astype(np.float32)
    oh_j = np.broadcast_to(oh_c[:, :, None], (3, NA_QC, NA_KR, NA_KC, n_dc)).reshape(3, NA_QC, NA_TK, n_dc)
    vq = jnp.einsum("lhab,ycjb->lhaycj", rpb.astype(F32), jnp.asarray(oh_j), precision=lax.Precision.HIGHEST)
    valid_c = np.broadcast_to(vc[:, :, None], (3, NA_QC, NA_KR, NA_KC)).reshape(3, NA_QC, NA_TK)
    vq = jnp.where(jnp.asarray(valid_c)[None, None, None], vq, NEG_BIG)
    drt = np.where(vr, dr, -1)[..., None]
    drt = np.broadcast_to(drt, (3, NA_QR, NA_KR, NA_KC)).reshape(3, NA_QR, NA_TK).astype(np.int32)
    offsets = [[sorted(set(dr[x, rr][vr[x, rr]].tolist())) for rr in range(NA_QR)] for x in range(3)]
    return vq, jnp.asarray(drt), offsets


def _na_kernel(rows, offsets, q_ref, k16_ref, v16_ref, kc_ref, vc_ref, vq_ref, drt_ref, o_ref,
               bias_ref, k_ref, v_ref):
    k_ref[...] = k16_ref[...].astype(F32)
    v_ref[...] = v16_ref[...].astype(F32)
    n_rb = rows // NA_QR
    n_cb = GRID_W // NA_QC
    low = _head_masks()
    kctx = kc_ref[0, 0].astype(BF16)
    vctx = vc_ref[0, 0].astype(BF16)

    @pl.when(pl.program_id(1) == 0)
    def _():
        def build(t, carry):
            hh = t // 3
            cc = t % 3
            for rc in range(3):
                for rr in range(NA_QR):
                    drrow = drt_ref[rc, rr:rr + 1, :]
                    acc = jnp.full((NA_QC, NA_TK), NEG_BIG, F32)
                    for a in offsets[rc][rr]:
                        acc = jnp.where(drrow == a, vq_ref[hh, a, cc], acc)
                    row0 = pl.multiple_of(hh * NA_TQ + rr * NA_QC, NA_QC)
                    bias_ref[rc, cc, pl.ds(row0, NA_QC), :] = acc
            return carry

        lax.fori_loop(0, 6, build, 0)

    def row_block(i, carry):
        kr0 = jnp.clip(NA_QR * i - NA_WIN_ROWS // 2, 0, rows - NA_KR)
        rcfg = jnp.where(i > 0, 1, 0) + jnp.where(i == n_rb - 1, 1, 0)
        for j in range(n_cb):
            kc0 = int(np.clip(NA_QC * j - NA_WIN_COLS // 2, 0, GRID_W - NA_KC))
            ccfg = 0 if j == 0 else (2 if j == n_cb - 1 else 1)
            q_parts = []
            for rr in range(NA_QR):
                start = pl.multiple_of((NA_QR * i + rr) * GRID_W + NA_QC * j, NA_QC)
                q_parts.append(q_ref[pl.ds(start, NA_QC), :])
            q = jnp.concatenate(q_parts, axis=0)
            k_parts, v_parts = [], []
            for kk in range(NA_KR):
                start = pl.multiple_of((kr0 + kk) * GRID_W + kc0, 8)
                k_parts.append(k_ref[pl.ds(start, NA_KC), :])
                v_parts.append(v_ref[pl.ds(start, NA_KC), :])
            k = jnp.concatenate(k_parts, axis=0).astype(BF16)
            v = jnp.concatenate(v_parts, axis=0).astype(BF16)
            q2 = jnp.concatenate([jnp.where(low, q, 0.0), jnp.where(low, 0.0, q)], axis=0).astype(BF16)
            s_loc = _bdot_nt(q2, k) + bias_ref[rcfg, ccfg]
            s_ctx = _bdot_nt(q2, kctx)
            m = jnp.maximum(jnp.max(s_loc, axis=-1, keepdims=True),
                            jnp.max(s_ctx, axis=-1, keepdims=True))
            p_loc = jnp.exp(s_loc - m)
            p_ctx = jnp.exp(s_ctx - m)
            l = jnp.sum(p_loc, axis=-1, keepdims=True) + jnp.sum(p_ctx, axis=-1, keepdims=True)
            o2 = (jnp.dot(p_loc.astype(BF16), v, preferred_element_type=F32)
                  + jnp.dot(p_ctx.astype(BF16), vctx, preferred_element_type=F32)) * (1.0 / l)
            o = jnp.where(low, o2[:NA_TQ], o2[NA_TQ:])
            for rr in range(NA_QR):
                start = pl.multiple_of((NA_QR * i + rr) * GRID_W + NA_QC * j, NA_QC)
                o_ref[pl.ds(start, NA_QC), :] = o[rr * NA_QC:(rr + 1) * NA_QC, :].astype(o_ref.dtype)
        return carry

    lax.fori_loop(0, n_rb, row_block, 0, unroll=2)


def _na_attn(u_qkv, row0, nb, L, cache_k, cache_v, layer, bias_inputs):
    npair = NA_HEADS // 2
    rows = L // GRID_W
    blk0 = row0 // L
    lc = cache_k.shape[2]
    vq, drt, offsets = bias_inputs
    return pl.pallas_call(
        functools.partial(_na_kernel, rows, offsets),
        grid=(npair, nb),
        in_specs=[pl.BlockSpec((L, LANES), lambda p, b: (blk0 + b, p)),
                  pl.BlockSpec((L, LANES), lambda p, b: (blk0 + b, npair + p)),
                  pl.BlockSpec((L, LANES), lambda p, b: (blk0 + b, 2 * npair + p)),
                  pl.BlockSpec((1, 1, lc, LANES), lambda p, b: (b, layer, 0, p)),
                  pl.BlockSpec((1, 1, lc, LANES), lambda p, b: (b, layer, 0, p)),
                  pl.BlockSpec((None, 2) + vq.shape[2:], lambda p, b: (layer, p, 0, 0, 0, 0)),
                  pl.BlockSpec(drt.shape, lambda p, b: (0, 0, 0))],
        out_specs=pl.BlockSpec((L, LANES), lambda p, b: (b, p)),
        out_shape=jax.ShapeDtypeStruct((nb * L, NA_WIDTH), BF16),
        scratch_shapes=[pltpu.VMEM((3, 3, 2 * NA_TQ, NA_TK), F32),
                        pltpu.VMEM((L, LANES), F32), pltpu.VMEM((L, LANES), F32)],
        compiler_params=_cparams(("arbitrary", "arbitrary")),
        name="na_attn",
    )(u_qkv, u_qkv, u_qkv, cache_k, cache_v, vq, drt)


def _ssd_constants():
    tril = np.tril(np.ones((SSD_CHUNK, SSD_CHUNK), np.float32))
    expand = np.zeros((2, LANES, SSD_PAD), np.float32)
    colb = np.zeros((2, LANES, SSD_HEADS * LANES), np.float32)
    for d in range(2):
        for h in range(SSD_HEADS):
            s = _slot_of_head(h)
            expand[d, SLOTS * d + s, HEAD_DIM * s:HEAD_DIM * (s + 1)] = 1.0
            colb[d, SLOTS * d + s, LANES * h:LANES * (h + 1)] = 1.0
    two = lambda m: np.concatenate([m, m], axis=-2)
    return np.concatenate([tril, tril], axis=1), two(expand), two(colb)


def _rope_tables(L):
    t = np.arange(L)
    rows = (t // GRID_W).astype(np.float64)
    cols = (t % GRID_W).astype(np.float64)
    quarter = SSD_STATE // 4
    inv = ROPE_BASE ** (-np.arange(quarter, dtype=np.float64) / quarter)
    n = np.arange(SSD_STATE)
    pos = np.where(n[None, :] < SSD_STATE // 2, rows[:, None], cols[:, None])
    ang = pos * inv[n % quarter][None, :]
    first = (n % (SSD_STATE // 2)) < quarter
    cos = np.cos(ang)
    sin = np.where(first[None, :], -np.sin(ang), np.sin(ang))
    tile = lambda a: np.concatenate([a] * SSD_NGROUPS, axis=1).astype(np.float32)
    return tile(cos), tile(sin)


def _split2(x):
    hi = x.astype(BF16)
    lo = (x - hi.astype(F32)).astype(BF16)
    return hi, lo


def _ssd_kernel(L, use_rope, use_init, *refs):
    (u_ref, convw_ref, convb_ref, dtb_ref, a_ref, ax_ref, d_ref, nw_ref,
     tril_ref, exp_ref, colb_ref) = refs[:11]
    pos = 11
    if use_rope:
        cos_ref, sin_ref = refs[pos:pos + 2]
        pos += 2
    if use_init:
        s0_refs = refs[pos:pos + 2]
        pos += 2
    y_ref = refs[pos]
    sfin_refs = refs[pos + 1:pos + 3]
    act_ref, dt_ref, s_ref, yacc_ref = refs[pos + 3:]

    nc = L // SSD_CHUNK
    C = SSD_CHUNK
    GW = SSD_PAD // SSD_NGROUPS
    XC = SSD_PAD
    CONV_W = SSD_PAD + 2 * LANES
    DTC = XC + CONV_W
    HALO = 8

    lane = lax.broadcasted_iota(jnp.int32, (1, LANES), 1)
    first_q = (lane % (SSD_STATE // 2)) < (SSD_STATE // 4)

    def prep(c, carry):
        r0 = pl.multiple_of(c * C, C)
        main = u_ref[pl.ds(r0, C), XC:XC + CONV_W]
        pstart = pl.multiple_of(jnp.maximum(r0 - HALO, 0), HALO)
        nstart = pl.multiple_of(jnp.minimum(r0 + C, L - HALO), HALO)
        prev = u_ref[pl.ds(pstart, HALO), XC:XC + CONV_W] * jnp.where(c > 0, 1.0, 0.0)
        nxt = u_ref[pl.ds(nstart, HALO), XC:XC + CONV_W] * jnp.where(c < nc - 1, 1.0, 0.0)
        win = jnp.concatenate([prev, main, nxt], axis=0)
        acc = jnp.zeros((C, CONV_W), F32) + convb_ref[...]
        for k in range(SSD_CONV):
            off = HALO + k - SSD_CONV // 2
            acc = acc + win[off:off + C, :] * convw_ref[k:k + 1, :]
        act = _silu(acc)
        act_ref[pl.ds(r0, C), 0:SSD_PAD] = act[:, 0:SSD_PAD]
        for t in range(2):
            bc = act[:, SSD_PAD + t * LANES:SSD_PAD + (t + 1) * LANES]
            if use_rope:
                partner = jnp.where(first_q, pltpu.roll(bc, LANES - SSD_STATE // 4, 1),
                                    pltpu.roll(bc, SSD_STATE // 4, 1))
                bc = bc * cos_ref[pl.ds(r0, C), :] + partner * sin_ref[pl.ds(r0, C), :]
            act_ref[pl.ds(r0, C), SSD_PAD + t * LANES:SSD_PAD + (t + 1) * LANES] = bc
        raw = u_ref[pl.ds(r0, C), DTC:DTC + LANES] + dtb_ref[...]
        dt_ref[pl.ds(r0, C), :] = jnp.maximum(raw, 0.0) + jnp.log1p(jnp.exp(-jnp.abs(raw)))
        yacc_ref[pl.ds(r0, C), :] = jnp.zeros((C, SSD_PAD), F32)
        return carry

    lax.fori_loop(0, nc, prep, 0)

    if use_init:
        zero = jnp.zeros((HEAD_DIM, SSD_STATE), F32)
        for d in range(2):
            for g in range(SSD_NGROUPS):
                rows_pn = []
                for hh in range(3):
                    blk = s0_refs[d][0, 0, 3 * g + hh]
                    rows_pn.append(jnp.concatenate([blk, zero] if g == 0 else [zero, blk], axis=1))
                rows_pn.append(jnp.zeros((HEAD_DIM, LANES), F32))
                s_ref[d, g] = jnp.concatenate(rows_pn, axis=0).T
    else:
        s_ref[...] = jnp.zeros_like(s_ref)

    a_row = a_ref[...]
    li = lax.broadcasted_iota(jnp.int32, (C, C), 0)
    si = lax.broadcasted_iota(jnp.int32, (C, C), 1)
    causal = [li >= si, si >= li]
    low64 = lane < HEAD_DIM
    grp_mask = [low64, jnp.logical_not(low64)]

    def one_direction(d, c):
        r0 = pl.multiple_of(c * C, C)
        x = act_ref[pl.ds(r0, C), 0:SSD_PAD]
        bmat = act_ref[pl.ds(r0, C), SSD_PAD:SSD_PAD + LANES]
        cmat = act_ref[pl.ds(r0, C), SSD_PAD + LANES:SSD_PAD + 2 * LANES]
        dt = dt_ref[pl.ds(r0, C), :]
        dta = dt * a_row
        hi, lo = _split2(dta)
        cs = jnp.dot(tril_ref[...], jnp.concatenate([hi, lo], axis=0),
                     preferred_element_type=F32)
        q = cs if d == 0 else cs - dta
        expand = lambda v: jnp.dot(jnp.concatenate(_split2(v), axis=1), exp_ref[d],
                                   preferred_element_type=F32)
        dt_x = expand(dt)
        cs_x = expand(cs)
        end_x = cs_x[C - 1:C, :]
        if d == 0:
            off_scale = jnp.exp(cs_x)
            w_state = jnp.exp(end_x - cs_x)
        else:
            e_x = cs_x - dt_x * ax_ref[d]
            off_scale = jnp.exp(end_x - e_x)
            w_state = jnp.exp(e_x)
        chunk_decay = jnp.exp(end_x)
        xdt = x * dt_x
        xdt_b = xdt.astype(BF16)
        rhs_state = (xdt * w_state).astype(BF16)
        qcol = jnp.dot(jnp.concatenate(_split2(q), axis=1), colb_ref[d],
                       preferred_element_type=F32)
        q_t = q.T
        b_t = bmat.T.astype(BF16)
        b_b = bmat.astype(BF16)
        cms = [jnp.where(grp_mask[g], cmat, 0.0).astype(BF16) for g in range(SSD_NGROUPS)]
        gmats = lax.dot_general(jnp.concatenate(cms, axis=0), b_b, (((1,), (1,)), ((), ())),
                                preferred_element_type=F32)
        st_all = jnp.dot(b_t, rhs_state, preferred_element_type=F32)
        for g in range(SSD_NGROUPS):
            gmat = gmats[g * C:(g + 1) * C]
            s_old = s_ref[d, g]
            y_off = (jnp.dot(cms[g], s_old.astype(BF16), preferred_element_type=F32)
                     * off_scale[:, g * GW:(g + 1) * GW])
            ms = []
            for hh in range(3):
                h = 3 * g + hh
                slot = 4 * g + hh
                row = q_t[SLOTS * d + slot:SLOTS * d + slot + 1, :]
                col = qcol[:, h * LANES:(h + 1) * LANES]
                seg = (col - row) if d == 0 else (row - col)
                ms.append((gmat * jnp.exp(jnp.where(causal[d], seg, NEG_BIG))).astype(BF16))
            r01 = jnp.dot(jnp.concatenate(ms[:2], axis=0), xdt_b[:, 2 * g * LANES:(2 * g + 1) * LANES],
                          preferred_element_type=F32)
            r2 = jnp.dot(ms[2], xdt_b[:, (2 * g + 1) * LANES:(2 * g + 2) * LANES],
                         preferred_element_type=F32)
            y_g = jnp.concatenate([jnp.where(low64, r01[:C], r01[C:]), r2], axis=1) + y_off
            yacc_ref[pl.ds(r0, C), g * GW:(g + 1) * GW] += y_g
            s_ref[d, g] = (s_old * chunk_decay[:, g * GW:(g + 1) * GW]
                           + st_all[:, g * GW:(g + 1) * GW])

    def scan(i, carry):
        one_direction(0, i)
        one_direction(1, nc - 1 - i)
        return carry

    lax.fori_loop(0, nc, scan, 0, unroll=2)
    for d in range(2):
        for g in range(SSD_NGROUPS):
            s_t = s_ref[d, g].T
            for hh in range(3):
                sfin_refs[d][0, 3 * g + hh] = s_t[hh * HEAD_DIM:(hh + 1) * HEAD_DIM,
                                                  g * SSD_STATE:(g + 1) * SSD_STATE]

    def finish(c, carry):
        r0 = pl.multiple_of(c * C, C)
        y = yacc_ref[pl.ds(r0, C), :] + act_ref[pl.ds(r0, C), 0:SSD_PAD] * d_ref[...]
        y = y * _silu(u_ref[pl.ds(r0, C), 0:SSD_PAD])
        for g in range(SSD_NGROUPS):
            yg = y[:, g * GW:(g + 1) * GW]
            ms = jnp.sum(yg * yg, axis=-1, keepdims=True) * (1.0 / (SSD_INNER // SSD_NGROUPS))
            yn = yg * lax.rsqrt(ms + EPS) * nw_ref[:, g * GW:(g + 1) * GW]
            y_ref[pl.ds(r0, C), g * GW:(g + 1) * GW] = yn.astype(y_ref.dtype)
        return carry

    lax.fori_loop(0, nc, finish, 0)


def _ssd(u_ssd, row0, nb, L, prm, layer, use_rope, s0):
    blk0 = row0 // L
    tril2, exp2, colb2 = _ssd_constants()
    consts = [jnp.asarray(tril2, dtype=BF16), jnp.asarray(exp2, dtype=BF16), jnp.asarray(colb2, dtype=BF16)]
    full = lambda a: pl.BlockSpec(a.shape, lambda b, _n=a.ndim: (0,) * _n)
    of_layer = lambda a: pl.BlockSpec((None,) + a.shape[1:], lambda b, _n=a.ndim: (layer,) + (0,) * (_n - 1))
    per_layer = [prm[k] for k in ("conv_w", "conv_b", "dt_bias", "a_row", "a_x", "d_row", "norm_w")]
    args = [u_ssd] + per_layer + consts
    in_specs = ([pl.BlockSpec((L, U_SSD), lambda b: (blk0 + b, 0))] + [of_layer(a) for a in per_layer]
                + [full(a) for a in consts])
    if use_rope:
        cos, sin = _rope_tables(L)
        tabs = [jnp.asarray(cos), jnp.asarray(sin)]
        args += tabs
        in_specs += [full(a) for a in tabs]
    state_block = (SSD_HEADS, HEAD_DIM, SSD_STATE)
    if s0 is not None:
        args += list(s0)
        in_specs += [pl.BlockSpec((1, 1) + state_block, lambda b: (b, layer, 0, 0, 0))] * 2
    sshape = (2, SSD_NGROUPS, LANES, SSD_PAD // SSD_NGROUPS)
    return pl.pallas_call(
        functools.partial(_ssd_kernel, L, use_rope, s0 is not None),
        grid=(nb,),
        in_specs=in_specs,
        out_specs=[pl.BlockSpec((L, SSD_PAD), lambda b: (b, 0))]
                  + [pl.BlockSpec((1,) + state_block, lambda b: (b, 0, 0, 0))] * 2,
        out_shape=[jax.ShapeDtypeStruct((nb * L, SSD_PAD), BF16)]
                  + [jax.ShapeDtypeStruct((nb,) + state_block, F32)] * 2,
        scratch_shapes=[pltpu.VMEM((L, SSD_PAD + 2 * LANES), F32),
                        pltpu.VMEM((L, LANES), F32),
                        pltpu.VMEM(sshape, F32),
                        pltpu.VMEM((L, SSD_PAD), F32)],
        compiler_params=_cparams(("parallel",)),
        name="ssd",
    )(*args)


def _pad_heads(a, axis=-1):
    a = jnp.moveaxis(a, axis, -1)
    lead = a.shape[:-1]
    a = a.reshape(lead + (SSD_NGROUPS, 3, HEAD_DIM))
    a = jnp.pad(a, [(0, 0)] * len(lead) + [(0, 0), (0, 1), (0, 0)])
    return jnp.moveaxis(a.reshape(lead + (SSD_PAD,)), -1, axis)


def _pad_dt_lanes(a):
    lead = a.shape[:-2]
    a = a.reshape(lead + (2, SSD_NGROUPS, 3))
    a = jnp.pad(a, [(0, 0)] * len(lead) + [(0, 0), (0, 0), (0, 1)]).reshape(lead + (2 * SLOTS,))
    return jnp.pad(a, [(0, 0)] * len(lead) + [(0, LANES - 2 * SLOTS)])


W_IN_TR = 256


def _w_in_relayout_kernel(w_ref, pdt_ref, o_ref):
    half_group = SSD_INNER // SSD_NGROUPS
    gw = SSD_PAD // SSD_NGROUPS
    zero = jnp.zeros((W_IN_TR, gw - half_group), o_ref.dtype)

    def put(dst, v):
        o_ref[:, dst:dst + v.shape[1]] = v.astype(o_ref.dtype)

    put(0, w_ref[:, 0:FN_WIDTH])
    for seg in range(2):
        for g in range(SSD_NGROUPS):
            src = FN_WIDTH + seg * SSD_INNER + g * half_group
            dst = U_FN + seg * SSD_PAD + g * gw
            put(dst, w_ref[:, src:src + half_group])
            put(dst + half_group, zero)
    src = FN_WIDTH + 2 * SSD_INNER
    put(U_FN + 2 * SSD_PAD, w_ref[:, src:src + 2 * LANES])
    src += 2 * LANES
    put(U_FN + 2 * SSD_PAD + 2 * LANES,
        jnp.dot(w_ref[:, src:src + LANES].astype(BF16), pdt_ref[...], preferred_element_type=F32))
    src += 2 * SSD_HEADS
    put(U_FN + U_SSD, w_ref[:, src:src + NA_WIDTH] * (HEAD_DIM ** -0.5))
    put(U_FN + U_SSD + NA_WIDTH, w_ref[:, src + NA_WIDTH:src + 3 * NA_WIDTH])


def _w_in_relayout(w_in):
    pdt = np.zeros((LANES, LANES), np.float32)
    for d in range(2):
        for h in range(SSD_HEADS):
            pdt[SSD_HEADS * d + h, SLOTS * d + _slot_of_head(h)] = 1.0
    return pl.pallas_call(
        _w_in_relayout_kernel,
        grid=(DEPTH, D_MODEL // W_IN_TR),
        in_specs=[pl.BlockSpec((None, W_IN_TR, w_in.shape[-1]), lambda l, i: (l, i, 0)),
                  pl.BlockSpec((LANES, LANES), lambda l, i: (0, 0))],
        out_specs=pl.BlockSpec((None, W_IN_TR, U_TOTAL), lambda l, i: (l, i, 0)),
        out_shape=jax.ShapeDtypeStruct((DEPTH, D_MODEL, U_TOTAL), BF16),
        compiler_params=_cparams(("parallel", "parallel")),
        name="w_in_relayout",
    )(w_in, jnp.asarray(pdt, dtype=BF16))


def _mixer_params(w_in, w_out, ssd_conv_w, ssd_conv_b, ssd_dt_bias, ssd_a_log, ssd_d, ssd_norm):
    w_in_pad = _w_in_relayout(w_in)
    a = -jnp.exp(ssd_a_log.astype(F32))
    a_x = jnp.repeat(_pad_dt_lanes(a)[:, :2 * SLOTS].reshape(DEPTH, 2, SLOTS), HEAD_DIM, axis=-1)
    ssd = {
        "conv_w": jnp.concatenate([_pad_heads(ssd_conv_w[..., :SSD_INNER]), ssd_conv_w[..., SSD_INNER:]], axis=-1),
        "conv_b": jnp.concatenate([_pad_heads(ssd_conv_b[..., :SSD_INNER]),
                                   ssd_conv_b[..., SSD_INNER:]], axis=-1)[:, None, :],
        "dt_bias": _pad_dt_lanes(ssd_dt_bias)[:, None, :],
        "a_row": _pad_dt_lanes(a)[:, None, :],
        "a_x": a_x.reshape(DEPTH, 2, 1, SSD_PAD),
        "d_row": _pad_heads(jnp.repeat(ssd_d, HEAD_DIM, axis=-1))[:, None, :],
        "norm_w": _pad_heads(ssd_norm)[:, None, :],
    }
    return {
        "w_in": w_in_pad,
        "w_out_fn": w_out[:, :FN_WIDTH].astype(BF16),
        "w_out_ssd": _pad_heads(w_out[:, FN_WIDTH:FN_WIDTH + SSD_INNER], axis=1).astype(BF16),
        "w_out_att": w_out[:, FN_WIDTH + SSD_INNER:].astype(BF16),
        "ssd": ssd,
    }


def _pick_tile(rows, want):
    t = min(rows, want)
    while rows % t:
        t //= 2
    return t


def kernel(x_prompt, x_sample, c, state_ssd_fwd, state_ssd_bwd, cache_attn_k, cache_attn_v, c_ctx, mod_w, mod_b, norm_pre, norm_post, ffn_w13, ffn_w2, w_in, w_out, ssd_conv_w, ssd_conv_b, ssd_dt_bias, ssd_a_log, ssd_d, ssd_norm, na_rpb):
    nbp, lp, _ = x_prompt.shape
    nbs, ls, _ = x_sample.shape
    n_ctx, n_lat = nbp * lp, nbs * ls
    assert n_ctx == n_lat and n_ctx % ls == 0 and nbs + 1 <= 8
    rpg = ls
    xs = (x_prompt.reshape(n_ctx, D_MODEL), x_sample.reshape(n_lat, D_MODEL))

    cvec = jnp.zeros((8, D_MODEL), F32).at[0].set(c_ctx).at[1:1 + nbs].set(c)
    mods = _mods(cvec, mod_w, mod_b).reshape(DEPTH, 8, N_MOD, D_MODEL)
    w13 = ffn_w13.astype(BF16)
    w2 = ffn_w2.astype(BF16)
    p = _mixer_params(w_in, w_out, ssd_conv_w, ssd_conv_b, ssd_dt_bias, ssd_a_log, ssd_d, ssd_norm)
    cache_k = cache_attn_k.reshape(nbs, DEPTH, cache_attn_k.shape[2], NA_WIDTH)
    cache_v = cache_attn_v.reshape(nbs, DEPTH, cache_attn_v.shape[2], NA_WIDTH)
    s0 = (state_ssd_fwd.astype(F32), state_ssd_bwd.astype(F32))
    na_bias = _na_bias_inputs(na_rpb, ls // GRID_W)
    tm_ffn = _pick_tile(rpg, 512)
    tm_proj = _pick_tile(rpg, 512)

    new_sf, new_sb, new_k, new_v = [], [], [], []
    for l in range(DEPTH):
        gm = jnp.concatenate([jnp.broadcast_to(mods[l, 0], (n_ctx // rpg, N_MOD, D_MODEL)),
                              mods[l, 1:1 + nbs]], axis=0)
        xs = _ffn(xs, gm, 0, norm_pre[l, 0], norm_post[l, 0], w13, w2, l, 0, rpg, tm_ffn)
        u_fn, u_ssd, u_qkv, k_ctx, v_ctx = _inproj(xs, gm[:, 3:6], norm_pre[l, 1], p["w_in"], l, rpg, tm_proj)

        y_ssd_c, sf_c, sb_c = _ssd(u_ssd, 0, nbp, lp, p["ssd"], l, False, None)
        mix_ctx = (_fourier(u_fn, 0, nbp, lp), y_ssd_c, _ctx_attn(u_qkv, nbp, lp))
        new_sf.append(sf_c)
        new_sb.append(sb_c)
        new_k.append(k_ctx.reshape(nbp, lp, NA_WIDTH))
        new_v.append(v_ctx.reshape(nbp, lp, NA_WIDTH))

        y_ssd_l, _, _ = _ssd(u_ssd, n_ctx, nbs, ls, p["ssd"], l, True, s0)
        mix_lat = (_fourier(u_fn, n_ctx, nbs, ls), y_ssd_l,
                   _na_attn(u_qkv, n_ctx, nbs, ls, cache_k, cache_v, l, na_bias))

        mix = (mix_ctx, mix_lat, (p["w_out_fn"], p["w_out_ssd"], p["w_out_att"]), norm_post[l, 1])
        xs = _ffn(xs, gm, 6, norm_pre[l, 2], norm_post[l, 2], w13, w2, l, 1, rpg, tm_ffn, mix=mix)

    return (xs[0].reshape(nbp, lp, D_MODEL), xs[1].reshape(nbs, ls, D_MODEL),
            jnp.stack(new_sf, axis=1), jnp.stack(new_sb, axis=1),
            jnp.stack(new_k, axis=1).reshape(nbp, DEPTH, lp, NA_HEADS, HEAD_DIM),
            jnp.stack(new_v, axis=1).reshape(nbp, DEPTH, lp, NA_HEADS, HEAD_DIM))
```

```python
import functools
import math

import numpy as np
import jax
import jax.numpy as jnp
from jax import lax
from jax.experimental import pallas as pl
from jax.experimental.pallas import tpu as pltpu

F32 = jnp.float32
BF16 = jnp.bfloat16

D_MODEL = 1024
DEPTH = 2
GRID_W = 64
FF_HIDDEN = 2816
N_MOD = 9
HEAD_DIM = 64
FN_WIDTH = 256
FN_GROUPS = 4
SSD_INNER = 384
SSD_HEADS = 6
SSD_STATE = 64
SSD_NGROUPS = 2
SSD_CONV = 5
SSD_CHUNK = 128
SSD_CONV_DIM = 640
SSD_IN = 1420
NA_WIDTH = 384
NA_HEADS = 6
NA_WIN_ROWS = 8
NA_WIN_COLS = 16
ROPE_BASE = 10000.0
EPS = 1e-6

LANES = 128
VMEM_LIMIT = 56 * 1024 * 1024

SLOTS = 8
SSD_PAD = SLOTS * HEAD_DIM
U_FN = FN_WIDTH
U_SSD = 2 * SSD_PAD + 2 * LANES + LANES
U_QKV = 3 * NA_WIDTH
U_TOTAL = U_FN + U_SSD + U_QKV
NEG_BIG = -1e30


def _slot_of_head(h):
    return 4 * (h // 3) + (h % 3)


def _cparams(sem):
    return pltpu.CompilerParams(dimension_semantics=sem, vmem_limit_bytes=VMEM_LIMIT)


def _rms(x):
    return x * lax.rsqrt(jnp.mean(x * x, axis=-1, keepdims=True) + EPS)


def _silu(x):
    return x * jax.nn.sigmoid(x)


def _bdot(a, b):
    return jnp.dot(a.astype(BF16), b.astype(BF16), preferred_element_type=F32)


def _bdot_nt(a, b):
    return lax.dot_general(a.astype(BF16), b.astype(BF16), (((1,), (1,)), ((), ())),
                           preferred_element_type=F32)


MOD_TN = 1152


def _mods_kernel(c_ref, w_ref, b_ref, o_ref):
    s = _silu(c_ref[...])
    o_ref[0] = _bdot(s, w_ref[0]) + b_ref[0]


def _mods(cvec, mod_w, mod_b):
    ncol = N_MOD * D_MODEL
    return pl.pallas_call(
        _mods_kernel,
        grid=(DEPTH, ncol // MOD_TN),
        in_specs=[pl.BlockSpec((8, D_MODEL), lambda l, j: (0, 0)),
                  pl.BlockSpec((1, D_MODEL, MOD_TN), lambda l, j: (l, 0, j)),
                  pl.BlockSpec((1, 1, MOD_TN), lambda l, j: (l, 0, j))],
        out_specs=pl.BlockSpec((1, 8, MOD_TN), lambda l, j: (l, 0, j)),
        out_shape=jax.ShapeDtypeStruct((DEPTH, 8, ncol), F32),
        compiler_params=_cparams(("parallel", "parallel")),
        name="mods",
    )(cvec, mod_w, mod_b.reshape(DEPTH, 1, ncol))


FFN_TH = 256


def _halves(nhalf):
    first = lambda i, *_: (jnp.minimum(i, nhalf - 1), 0)
    second = lambda i, *_: (jnp.maximum(i - nhalf, 0), 0)
    return first, second


def _on_half(nhalf, fn):
    i = pl.program_id(0)
    pl.when(i < nhalf)(functools.partial(fn, 0))
    pl.when(i >= nhalf)(functools.partial(fn, 1))


def _ffn_kernel(nhalf, mrow, with_mix, *refs):
    xa_ref, xb_ref, m_ref, gpre_ref, gpost_ref, w13_ref, w2_ref = refs[:7]
    pos = 7
    if with_mix:
        mix_refs = (refs[pos:pos + 3], refs[pos + 3:pos + 6])
        wmix_refs = refs[pos + 6:pos + 9]
        gmix_ref = refs[pos + 9]
        pos += 10
    oa_ref, ob_ref, h_ref, acc_ref = refs[pos:pos + 4]
    x_refs = (xa_ref, xb_ref)
    o_refs = (oa_ref, ob_ref)
    if with_mix:
        x1_ref = refs[pos + 4]

    def body(half):
        x = x_refs[half][...]
        if with_mix:
            y = None
            for y_ref, w_ref in zip(mix_refs[half], wmix_refs):
                d = jnp.dot(y_ref[...], w_ref[...], preferred_element_type=F32)
                y = d if y is None else y + d
            x = x + _rms(y) * (gmix_ref[...] * m_ref[0, mrow - 1:mrow, :])
            x1_ref[...] = x
        shift = m_ref[0, mrow:mrow + 1, :]
        scale = m_ref[0, mrow + 1:mrow + 2, :]
        h_ref[...] = (_rms(x) * (gpre_ref[...] * (1.0 + scale)) + shift).astype(BF16)

        h = h_ref[...]
        for j in range(FF_HIDDEN // FFN_TH):
            cols = slice(j * FFN_TH, (j + 1) * FFN_TH)
            g = jnp.dot(h, w13_ref[:, cols], preferred_element_type=F32)
            u = jnp.dot(h, w13_ref[:, FF_HIDDEN + j * FFN_TH:FF_HIDDEN + (j + 1) * FFN_TH],
                        preferred_element_type=F32)
            a = (_silu(g) * u).astype(BF16)
            part = jnp.dot(a, w2_ref[cols, :], preferred_element_type=F32)
            if j == 0:
                acc_ref[...] = part
            else:
                acc_ref[...] += part

        gate = m_ref[0, mrow + 2:mrow + 3, :]
        y = _rms(acc_ref[...]) * (gpost_ref[...] * (0.5 * gate))
        x = x1_ref[...] if with_mix else x_refs[half][...]
        o_refs[half][...] = x + y

    _on_half(nhalf, body)


def _ffn(xs, gmods, mrow, g_pre, g_post, w13, w2, layer, sub, rows_per_group, tm, mix=None):
    nh = xs[0].shape[0]
    nhalf = nh // tm
    tpg = rows_per_group // tm
    first, second = _halves(nhalf)
    resident = pl.Buffered(1)
    row = lambda a: a.reshape(1, D_MODEL)
    args = [xs[0], xs[1], gmods, row(g_pre), row(g_post), w13, w2]
    in_specs = [pl.BlockSpec((tm, D_MODEL), first),
                pl.BlockSpec((tm, D_MODEL), second),
                pl.BlockSpec((1, N_MOD, D_MODEL), lambda i: (i // tpg, 0, 0)),
                pl.BlockSpec((1, D_MODEL), lambda i: (0, 0)),
                pl.BlockSpec((1, D_MODEL), lambda i: (0, 0)),
                pl.BlockSpec((None, None, D_MODEL, 2 * FF_HIDDEN), lambda i: (layer, sub, 0, 0),
                             pipeline_mode=resident),
                pl.BlockSpec((None, None, FF_HIDDEN, D_MODEL), lambda i: (layer, sub, 0, 0),
                             pipeline_mode=resident)]
    scratch = [pltpu.VMEM((tm, D_MODEL), BF16), pltpu.VMEM((tm, D_MODEL), F32)]
    if mix is not None:
        mix_ctx, mix_lat, w_mix, g_post_mix = mix
        args += list(mix_ctx) + list(mix_lat) + list(w_mix) + [row(g_post_mix)]
        in_specs += ([pl.BlockSpec((tm, a.shape[1]), first) for a in mix_ctx]
                     + [pl.BlockSpec((tm, a.shape[1]), second) for a in mix_lat]
                     + [pl.BlockSpec((None,) + w.shape[1:], lambda i: (layer, 0, 0), pipeline_mode=resident)
                        for w in w_mix]
                     + [pl.BlockSpec((1, D_MODEL), lambda i: (0, 0))])
        scratch.append(pltpu.VMEM((tm, D_MODEL), F32))
    return pl.pallas_call(
        functools.partial(_ffn_kernel, nhalf, mrow, mix is not None),
        grid=(2 * nhalf,),
        in_specs=in_specs,
        out_specs=[pl.BlockSpec((tm, D_MODEL), first), pl.BlockSpec((tm, D_MODEL), second)],
        out_shape=[jax.ShapeDtypeStruct((nh, D_MODEL), F32)] * 2,
        scratch_shapes=scratch,
        compiler_params=_cparams(("arbitrary",)),
        name="ffn",
    )(*args)


def _inproj_kernel(nhalf, xa_ref, xb_ref, m_ref, gpre_ref, w_ref, ofn_ref, ossd_ref, oqkv_ref, ok_ref, ov_ref):
    x_refs = (xa_ref, xb_ref)

    def body(half):
        shift = m_ref[0, 0:1, :]
        scale = m_ref[0, 1:2, :]
        h = (_rms(x_refs[half][...]) * (gpre_ref[...] * (1.0 + scale)) + shift).astype(BF16)
        u = jnp.dot(h, w_ref[...], preferred_element_type=F32)
        ofn_ref[...] = u[:, :U_FN].astype(ofn_ref.dtype)
        ossd_ref[...] = u[:, U_FN:U_FN + U_SSD]
        oqkv_ref[...] = u[:, U_FN + U_SSD:].astype(oqkv_ref.dtype)
        if half == 0:
            nseq, _, seq_len = ok_ref.shape
            for s in range(nseq):
                rows = slice(s * seq_len, (s + 1) * seq_len)
                ok_ref[s] = u[rows, U_FN + U_SSD + NA_WIDTH:U_FN + U_SSD + 2 * NA_WIDTH].T
                ov_ref[s] = u[rows, U_FN + U_SSD + 2 * NA_WIDTH:].T

    _on_half(nhalf, body)


def _inproj(xs, gmods, g_pre, w_in_pad, layer, rows_per_group, tm, ctx_len):
    nh = xs[0].shape[0]
    n = 2 * nh
    nhalf = nh // tm
    tpg = rows_per_group // tm
    first, second = _halves(nhalf)
    assert tm % ctx_len == 0
    seq_per_tile = tm // ctx_len
    first_seq = lambda i: (jnp.minimum(i, nhalf - 1), 0, 0)
    return pl.pallas_call(
        functools.partial(_inproj_kernel, nhalf),
        grid=(n // tm,),
        in_specs=[pl.BlockSpec((tm, D_MODEL), first),
                  pl.BlockSpec((tm, D_MODEL), second),
                  pl.BlockSpec((1, 3, D_MODEL), lambda i: (i // tpg, 0, 0)),
                  pl.BlockSpec((1, D_MODEL), lambda i: (0, 0)),
                  pl.BlockSpec((None, D_MODEL, U_TOTAL), lambda i: (layer, 0, 0))],
        out_specs=[pl.BlockSpec((tm, U_FN), lambda i: (i, 0)),
                   pl.BlockSpec((tm, U_SSD), lambda i: (i, 0)),
                   pl.BlockSpec((tm, U_QKV), lambda i: (i, 0)),
                   pl.BlockSpec((seq_per_tile, NA_WIDTH, ctx_len), first_seq),
                   pl.BlockSpec((seq_per_tile, NA_WIDTH, ctx_len), first_seq)],
        out_shape=[jax.ShapeDtypeStruct((n, U_FN), BF16),
                   jax.ShapeDtypeStruct((n, U_SSD), F32),
                   jax.ShapeDtypeStruct((n, U_QKV), BF16),
                   jax.ShapeDtypeStruct((nh // ctx_len, NA_WIDTH, ctx_len), F32),
                   jax.ShapeDtypeStruct((nh // ctx_len, NA_WIDTH, ctx_len), F32)],
        compiler_params=_cparams(("arbitrary",)),
        name="inproj",
    )(xs[0], xs[1], gmods, g_pre.reshape(1, D_MODEL), w_in_pad)


def _dft_tables(L):
    k = np.arange(L, dtype=np.int64)
    ang = 2.0 * np.pi * ((k[:, None] * k[None, :]) % L).astype(np.float64) / L
    sc = 1.0 / math.sqrt(L * HEAD_DIM)
    cl = (np.cos(ang) * sc).astype(np.float32)
    sl = (-np.sin(ang) * sc).astype(np.float32)
    m = np.arange(HEAD_DIM, dtype=np.int64)
    a64 = 2.0 * np.pi * ((m[:, None] * m[None, :]) % HEAD_DIM).astype(np.float64) / HEAD_DIM
    eye = np.eye(FN_GROUPS)
    w1 = np.concatenate([np.kron(eye, np.cos(a64)), np.kron(eye, np.sin(a64))], axis=1).astype(np.float32)
    return cl, sl, w1


def _fourier_kernel(nseq, L, tl, u_ref, w1_ref, cl_ref, sl_ref, o_ref, ab_ref):
    rt = pl.program_id(0)
    g = pl.program_id(1)

    @pl.when(rt == 0)
    def _():
        ab_ref[g] = _bdot(u_ref[...], w1_ref[...]).astype(BF16)

    for s in range(nseq):
        ab = ab_ref[g, s * L:(s + 1) * L, :]
        y = (jnp.dot(cl_ref[...], ab[:, :FN_WIDTH], preferred_element_type=F32)
             + jnp.dot(sl_ref[...], ab[:, FN_WIDTH:], preferred_element_type=F32))
        o_ref[s * tl:(s + 1) * tl, :] = y.astype(o_ref.dtype)


def _fourier(u_fn, row0, nb, L):
    cl, sl, w1 = _dft_tables(L)
    cl = jnp.asarray(cl).astype(BF16)
    sl = jnp.asarray(sl).astype(BF16)
    w1 = jnp.asarray(w1).astype(BF16)
    tl = min(L, 1024)
    nrt = L // tl
    nseq = max(1, min(nb, 2048 // L)) if nrt == 1 else 1
    while nb % nseq:
        nseq -= 1
    ng = nb // nseq
    blk0 = row0 // (nseq * L)
    assert row0 % (nseq * L) == 0
    return pl.pallas_call(
        functools.partial(_fourier_kernel, nseq, L, tl),
        grid=(nrt, ng),
        in_specs=[pl.BlockSpec((nseq * L, FN_WIDTH), lambda rt, g: (blk0 + jnp.where(rt == 0, g, ng - 1), 0)),
                  pl.BlockSpec((FN_WIDTH, 2 * FN_WIDTH), lambda rt, g: (0, 0)),
                  pl.BlockSpec((tl, L), lambda rt, g: (rt, 0)),
                  pl.BlockSpec((tl, L), lambda rt, g: (rt, 0))],
        out_specs=pl.BlockSpec((nseq * tl, FN_WIDTH), lambda rt, g: (g * nrt + rt, 0)),
        out_shape=jax.ShapeDtypeStruct((nb * L, FN_WIDTH), BF16),
        scratch_shapes=[pltpu.VMEM((ng, nseq * L, 2 * FN_WIDTH), BF16)],
        compiler_params=_cparams(("arbitrary", "arbitrary")),
        name="fourier",
    )(u_fn, w1, cl, sl)


def _head_masks():
    lane = lax.broadcasted_iota(jnp.int32, (1, LANES), 1)
    return lane < HEAD_DIM


def _ctx_attn_kernel(nseq, L, q_ref, k_ref, v_ref, o_ref):
    low = _head_masks()
    for t in range(nseq):
        rows = slice(t * L, (t + 1) * L)
        for p in range(NA_HEADS // 2):
            lanes = slice(p * LANES, (p + 1) * LANES)
            q = q_ref[rows, lanes]
            k = k_ref[rows, lanes].astype(BF16)
            v = v_ref[rows, lanes].astype(BF16)
            q2 = jnp.concatenate([jnp.where(low, q, 0.0), jnp.where(low, 0.0, q)], axis=0)
            s = _bdot_nt(q2, k)
            m = jnp.max(s, axis=-1, keepdims=True)
            e = jnp.exp(s - m)
            inv = 1.0 / jnp.sum(e, axis=-1, keepdims=True)
            o2 = jnp.dot(e.astype(BF16), v, preferred_element_type=F32) * inv
            o_ref[rows, lanes] = jnp.where(low, o2[:L], o2[L:]).astype(o_ref.dtype)


def _ctx_attn(u_qkv, nb, L):
    nseq = 4 if nb % 4 == 0 else 1
    return pl.pallas_call(
        functools.partial(_ctx_attn_kernel, nseq, L),
        grid=(nb // nseq,),
        in_specs=[pl.BlockSpec((nseq * L, NA_WIDTH), lambda b: (b, 0)),
                  pl.BlockSpec((nseq * L, NA_WIDTH), lambda b: (b, 1)),
                  pl.BlockSpec((nseq * L, NA_WIDTH), lambda b: (b, 2))],
        out_specs=pl.BlockSpec((nseq * L, NA_WIDTH), lambda b: (b, 0)),
        out_shape=jax.ShapeDtypeStruct((nb * L, NA_WIDTH), BF16),
        compiler_params=_cparams(("parallel",)),
        name="ctx_attn",
    )(u_qkv, u_qkv, u_qkv)


NA_QR = 8
NA_QC = 16
NA_KR = 16
NA_KC = 32
NA_TQ = NA_QR * NA_QC
NA_TK = NA_KR * NA_KC


def _na_tile_geometry(rows):
    wr = min(NA_WIN_ROWS, rows)
    n_rb = rows // NA_QR
    n_cb = GRID_W // NA_QC
    rb = [0, 1, n_rb - 1]
    cb = [0, 1, n_cb - 1]
    dr = np.zeros((3, NA_QR, NA_KR), np.int64)
    vr = np.zeros((3, NA_QR, NA_KR), bool)
    for ci, i in enumerate(rb):
        kr0 = int(np.clip(NA_QR * i - NA_WIN_ROWS // 2, 0, rows - NA_KR))
        for rr in range(NA_QR):
            r = NA_QR * i + rr
            rs = int(np.clip(r - wr // 2, 0, rows - wr))
            for kk in range(NA_KR):
                kr = kr0 + kk
                vr[ci, rr, kk] = rs <= kr < rs + wr
                dr[ci, rr, kk] = np.clip(kr - r + NA_WIN_ROWS - 1, 0, 2 * NA_WIN_ROWS - 2)
    dc = np.zeros((3, NA_QC, NA_KC), np.int64)
    vc = np.zeros((3, NA_QC, NA_KC), bool)
    for ci, j in enumerate(cb):
        kc0 = int(np.clip(NA_QC * j - NA_WIN_COLS // 2, 0, GRID_W - NA_KC))
        for cq in range(NA_QC):
            c = NA_QC * j + cq
            cs = int(np.clip(c - NA_WIN_COLS // 2, 0, GRID_W - NA_WIN_COLS))
            for ck in range(NA_KC):
                kc = kc0 + ck
                vc[ci, cq, ck] = cs <= kc < cs + NA_WIN_COLS
                dc[ci, cq, ck] = np.clip(kc - c + NA_WIN_COLS - 1, 0, 2 * NA_WIN_COLS - 2)
    return dr, vr, dc, vc


def _na_bias_inputs(rpb, rows):
    dr, vr, dc, vc = _na_tile_geometry(rows)
    n_dc = 2 * NA_WIN_COLS - 1
    oh_c = (dc[..., None] == np.arange(n_dc)).astype(np.float32)
    oh_j = np.broadcast_to(oh_c[:, :, None], (3, NA_QC, NA_KR, NA_KC, n_dc)).reshape(3, NA_QC, NA_TK, n_dc)
    vq = jnp.einsum("lhab,ycjb->lhaycj", rpb.astype(F32), jnp.asarray(oh_j), precision=lax.Precision.HIGHEST)
    valid_c = np.broadcast_to(vc[:, :, None], (3, NA_QC, NA_KR, NA_KC)).reshape(3, NA_QC, NA_TK)
    vq = jnp.where(jnp.asarray(valid_c)[None, None, None], vq, NEG_BIG)
    drt = np.where(vr, dr, -1)[..., None]
    drt = np.broadcast_to(drt, (3, NA_QR, NA_KR, NA_KC)).reshape(3, NA_QR, NA_TK).astype(np.int32)
    offsets = [[sorted(set(dr[x, rr][vr[x, rr]].tolist())) for rr in range(NA_QR)] for x in range(3)]
    return vq, jnp.asarray(drt), offsets


def _na_kernel(rows, offsets, q_ref, k16_ref, v16_ref, kc_ref, vc_ref, vq_ref, drt_ref, o_ref,
               bias_ref, k_ref, v_ref):
    k_ref[...] = k16_ref[...].astype(F32)
    v_ref[...] = v16_ref[...].astype(F32)
    n_rb = rows // NA_QR
    n_cb = GRID_W // NA_QC
    low = _head_masks()
    kctx_t = kc_ref[0, 0].astype(BF16)
    vctx_t = vc_ref[0, 0].astype(BF16)

    @pl.when(pl.program_id(1) == 0)
    def _():
        def build(t, carry):
            hh = t // 3
            cc = t % 3
            for rc in range(3):
                for rr in range(NA_QR):
                    drrow = drt_ref[rc, rr:rr + 1, :]
                    acc = jnp.full((NA_QC, NA_TK), NEG_BIG, F32)
                    for a in offsets[rc][rr]:
                        acc = jnp.where(drrow == a, vq_ref[hh, a, cc], acc)
                    row0 = pl.multiple_of(hh * NA_TQ + rr * NA_QC, NA_QC)
                    bias_ref[rc, cc, pl.ds(row0, NA_QC), :] = acc
            return carry

        lax.fori_loop(0, 6, build, 0)

    def row_block(i, carry):
        kr0 = jnp.clip(NA_QR * i - NA_WIN_ROWS // 2, 0, rows - NA_KR)
        rcfg = jnp.where(i > 0, 1, 0) + jnp.where(i == n_rb - 1, 1, 0)
        for j in range(n_cb):
            kc0 = int(np.clip(NA_QC * j - NA_WIN_COLS // 2, 0, GRID_W - NA_KC))
            ccfg = 0 if j == 0 else (2 if j == n_cb - 1 else 1)
            q_parts = []
            for rr in range(NA_QR):
                start = pl.multiple_of((NA_QR * i + rr) * GRID_W + NA_QC * j, NA_QC)
                q_parts.append(q_ref[pl.ds(start, NA_QC), :])
            q = jnp.concatenate(q_parts, axis=0)
            k_parts, v_parts = [], []
            for kk in range(NA_KR):
                start = pl.multiple_of((kr0 + kk) * GRID_W + kc0, 8)
                k_parts.append(k_ref[pl.ds(start, NA_KC), :])
                v_parts.append(v_ref[pl.ds(start, NA_KC), :])
            k = jnp.concatenate(k_parts, axis=0).astype(BF16)
            v = jnp.concatenate(v_parts, axis=0).astype(BF16)
            q2 = jnp.concatenate([jnp.where(low, q, 0.0), jnp.where(low, 0.0, q)], axis=0).astype(BF16)
            s_loc = _bdot_nt(q2, k) + bias_ref[rcfg, ccfg]
            s_ctx = jnp.dot(q2, kctx_t, preferred_element_type=F32)
            m = jnp.maximum(jnp.max(s_loc, axis=-1, keepdims=True),
                            jnp.max(s_ctx, axis=-1, keepdims=True))
            p_loc = jnp.exp(s_loc - m)
            p_ctx = jnp.exp(s_ctx - m)
            l = jnp.sum(p_loc, axis=-1, keepdims=True) + jnp.sum(p_ctx, axis=-1, keepdims=True)
            o2 = (jnp.dot(p_loc.astype(BF16), v, preferred_element_type=F32)
                  + _bdot_nt(p_ctx, vctx_t)) * (1.0 / l)
            o = jnp.where(low, o2[:NA_TQ], o2[NA_TQ:])
            for rr in range(NA_QR):
                start = pl.multiple_of((NA_QR * i + rr) * GRID_W + NA_QC * j, NA_QC)
                o_ref[pl.ds(start, NA_QC), :] = o[rr * NA_QC:(rr + 1) * NA_QC, :].astype(o_ref.dtype)
        return carry

    lax.fori_loop(0, n_rb, row_block, 0, unroll=2)


def _na_attn(u_qkv, row0, nb, L, cache_k, cache_v, layer, bias_inputs):
    npair = NA_HEADS // 2
    rows = L // GRID_W
    blk0 = row0 // L
    lc = cache_k.shape[3]
    vq, drt, offsets = bias_inputs
    return pl.pallas_call(
        functools.partial(_na_kernel, rows, offsets),
        grid=(npair, nb),
        in_specs=[pl.BlockSpec((L, LANES), lambda p, b: (blk0 + b, p)),
                  pl.BlockSpec((L, LANES), lambda p, b: (blk0 + b, npair + p)),
                  pl.BlockSpec((L, LANES), lambda p, b: (blk0 + b, 2 * npair + p)),
                  pl.BlockSpec((1, 1, LANES, lc), lambda p, b: (b, layer, p, 0)),
                  pl.BlockSpec((1, 1, LANES, lc), lambda p, b: (b, layer, p, 0)),
                  pl.BlockSpec((None, 2) + vq.shape[2:], lambda p, b: (layer, p, 0, 0, 0, 0)),
                  pl.BlockSpec(drt.shape, lambda p, b: (0, 0, 0))],
        out_specs=pl.BlockSpec((L, LANES), lambda p, b: (b, p)),
        out_shape=jax.ShapeDtypeStruct((nb * L, NA_WIDTH), BF16),
        scratch_shapes=[pltpu.VMEM((3, 3, 2 * NA_TQ, NA_TK), F32),
                        pltpu.VMEM((L, LANES), F32), pltpu.VMEM((L, LANES), F32)],
        compiler_params=_cparams(("arbitrary", "arbitrary")),
        name="na_attn",
    )(u_qkv, u_qkv, u_qkv, cache_k, cache_v, vq, drt)


def _ssd_constants():
    tril = np.tril(np.ones((SSD_CHUNK, SSD_CHUNK), np.float32))
    expand = np.zeros((2, LANES, SSD_PAD), np.float32)
    colb = np.zeros((2, LANES, SSD_HEADS * LANES), np.float32)
    for d in range(2):
        for h in range(SSD_HEADS):
            s = _slot_of_head(h)
            expand[d, SLOTS * d + s, HEAD_DIM * s:HEAD_DIM * (s + 1)] = 1.0
            colb[d, SLOTS * d + s, LANES * h:LANES * (h + 1)] = 1.0
    two = lambda m: np.concatenate([m, m], axis=-2)
    return np.concatenate([tril, tril], axis=1), two(expand), two(colb)


def _rope_tables(L):
    t = np.arange(L)
    rows = (t // GRID_W).astype(np.float64)
    cols = (t % GRID_W).astype(np.float64)
    quarter = SSD_STATE // 4
    inv = ROPE_BASE ** (-np.arange(quarter, dtype=np.float64) / quarter)
    n = np.arange(SSD_STATE)
    pos = np.where(n[None, :] < SSD_STATE // 2, rows[:, None], cols[:, None])
    ang = pos * inv[n % quarter][None, :]
    first = (n % (SSD_STATE // 2)) < quarter
    cos = np.cos(ang)
    sin = np.where(first[None, :], -np.sin(ang), np.sin(ang))
    tile = lambda a: np.concatenate([a] * SSD_NGROUPS, axis=1).astype(np.float32)
    return tile(cos), tile(sin)


def _split2(x):
    hi = x.astype(BF16)
    lo = (x - hi.astype(F32)).astype(BF16)
    return hi, lo


def _ssd_kernel(L, use_rope, use_init, *refs):
    (u_ref, convw_ref, convb_ref, dtb_ref, a_ref, ax_ref, d_ref, nw_ref,
     tril_ref, exp_ref, colb_ref) = refs[:11]
    pos = 11
    if use_rope:
        cos_ref, sin_ref = refs[pos:pos + 2]
        pos += 2
    if use_init:
        s0_refs = refs[pos:pos + 2]
        pos += 2
    y_ref = refs[pos]
    sfin_refs = refs[pos + 1:pos + 3]
    act_ref, dt_ref, s_ref, yacc_ref = refs[pos + 3:]

    nc = L // SSD_CHUNK
    C = SSD_CHUNK
    GW = SSD_PAD // SSD_NGROUPS
    XC = SSD_PAD
    CONV_W = SSD_PAD + 2 * LANES
    DTC = XC + CONV_W
    HALO = 8

    lane = lax.broadcasted_iota(jnp.int32, (1, LANES), 1)
    first_q = (lane % (SSD_STATE // 2)) < (SSD_STATE // 4)

    def prep(c, carry):
        r0 = pl.multiple_of(c * C, C)
        main = u_ref[pl.ds(r0, C), XC:XC + CONV_W]
        pstart = pl.multiple_of(jnp.maximum(r0 - HALO, 0), HALO)
        nstart = pl.multiple_of(jnp.minimum(r0 + C, L - HALO), HALO)
        prev = u_ref[pl.ds(pstart, HALO), XC:XC + CONV_W] * jnp.where(c > 0, 1.0, 0.0)
        nxt = u_ref[pl.ds(nstart, HALO), XC:XC + CONV_W] * jnp.where(c < nc - 1, 1.0, 0.0)
        win = jnp.concatenate([prev, main, nxt], axis=0)
        acc = jnp.zeros((C, CONV_W), F32) + convb_ref[...]
        for k in range(SSD_CONV):
            off = HALO + k - SSD_CONV // 2
            acc = acc + win[off:off + C, :] * convw_ref[k:k + 1, :]
        act = _silu(acc)
        act_ref[pl.ds(r0, C), 0:SSD_PAD] = act[:, 0:SSD_PAD]
        for t in range(2):
            bc = act[:, SSD_PAD + t * LANES:SSD_PAD + (t + 1) * LANES]
            if use_rope:
                partner = jnp.where(first_q, pltpu.roll(bc, LANES - SSD_STATE // 4, 1),
                                    pltpu.roll(bc, SSD_STATE // 4, 1))
                bc = bc * cos_ref[pl.ds(r0, C), :] + partner * sin_ref[pl.ds(r0, C), :]
            act_ref[pl.ds(r0, C), SSD_PAD + t * LANES:SSD_PAD + (t + 1) * LANES] = bc
        raw = u_ref[pl.ds(r0, C), DTC:DTC + LANES] + dtb_ref[...]
        dt_ref[pl.ds(r0, C), :] = jnp.maximum(raw, 0.0) + jnp.log1p(jnp.exp(-jnp.abs(raw)))
        yacc_ref[pl.ds(r0, C), :] = jnp.zeros((C, SSD_PAD), F32)
        return carry

    lax.fori_loop(0, nc, prep, 0)

    if use_init:
        zero = jnp.zeros((HEAD_DIM, SSD_STATE), F32)
        for d in range(2):
            for g in range(SSD_NGROUPS):
                rows_pn = []
                for hh in range(3):
                    blk = s0_refs[d][0, 0, 3 * g + hh]
                    rows_pn.append(jnp.concatenate([blk, zero] if g == 0 else [zero, blk], axis=1))
                rows_pn.append(jnp.zeros((HEAD_DIM, LANES), F32))
                s_ref[d, g] = jnp.concatenate(rows_pn, axis=0).T
    else:
        s_ref[...] = jnp.zeros_like(s_ref)

    a_row = a_ref[...]
    li = lax.broadcasted_iota(jnp.int32, (C, C), 0)
    si = lax.broadcasted_iota(jnp.int32, (C, C), 1)
    causal = [li >= si, si >= li]
    low64 = lane < HEAD_DIM
    grp_mask = [low64, jnp.logical_not(low64)]

    def one_direction(d, c):
        r0 = pl.multiple_of(c * C, C)
        x = act_ref[pl.ds(r0, C), 0:SSD_PAD]
        bmat = act_ref[pl.ds(r0, C), SSD_PAD:SSD_PAD + LANES]
        cmat = act_ref[pl.ds(r0, C), SSD_PAD + LANES:SSD_PAD + 2 * LANES]
        dt = dt_ref[pl.ds(r0, C), :]
        dta = dt * a_row
        hi, lo = _split2(dta)
        cs = jnp.dot(tril_ref[...], jnp.concatenate([hi, lo], axis=0),
                     preferred_element_type=F32)
        q = cs if d == 0 else cs - dta
        expand = lambda v: jnp.dot(jnp.concatenate(_split2(v), axis=1), exp_ref[d],
                                   preferred_element_type=F32)
        dt_x = expand(dt)
        cs_x = expand(cs)
        end_x = cs_x[C - 1:C, :]
        if d == 0:
            off_scale = jnp.exp(cs_x)
            w_state = jnp.exp(end_x - cs_x)
        else:
            e_x = cs_x - dt_x * ax_ref[d]
            off_scale = jnp.exp(end_x - e_x)
            w_state = jnp.exp(e_x)
        chunk_decay = jnp.exp(end_x)
        xdt = x * dt_x
        xdt_b = xdt.astype(BF16)
        rhs_state = (xdt * w_state).astype(BF16)
        qcol = jnp.dot(jnp.concatenate(_split2(q), axis=1), colb_ref[d],
                       preferred_element_type=F32)
        q_t = q.T
        b_t = bmat.T.astype(BF16)
        b_b = bmat.astype(BF16)
        cms = [jnp.where(grp_mask[g], cmat, 0.0).astype(BF16) for g in range(SSD_NGROUPS)]
        gmats = lax.dot_general(jnp.concatenate(cms, axis=0), b_b, (((1,), (1,)), ((), ())),
                                preferred_element_type=F32)
        st_all = jnp.dot(b_t, rhs_state, preferred_element_type=F32)
        for g in range(SSD_NGROUPS):
            gmat = gmats[g * C:(g + 1) * C]
            s_old = s_ref[d, g]
            y_off = (jnp.dot(cms[g], s_old.astype(BF16), preferred_element_type=F32)
                     * off_scale[:, g * GW:(g + 1) * GW])
            ms = []
            for hh in range(3):
                h = 3 * g + hh
                slot = 4 * g + hh
                row = q_t[SLOTS * d + slot:SLOTS * d + slot + 1, :]
                col = qcol[:, h * LANES:(h + 1) * LANES]
                seg = (col - row) if d == 0 else (row - col)
                ms.append((gmat * jnp.exp(jnp.where(causal[d], seg, NEG_BIG))).astype(BF16))
            r01 = jnp.dot(jnp.concatenate(ms[:2], axis=0), xdt_b[:, 2 * g * LANES:(2 * g + 1) * LANES],
                          preferred_element_type=F32)
            r2 = jnp.dot(ms[2], xdt_b[:, (2 * g + 1) * LANES:(2 * g + 2) * LANES],
                         preferred_element_type=F32)
            y_g = jnp.concatenate([jnp.where(low64, r01[:C], r01[C:]), r2], axis=1) + y_off
            yacc_ref[pl.ds(r0, C), g * GW:(g + 1) * GW] += y_g
            s_ref[d, g] = (s_old * chunk_decay[:, g * GW:(g + 1) * GW]
                           + st_all[:, g * GW:(g + 1) * GW])

    def scan(i, carry):
        one_direction(0, i)
        one_direction(1, nc - 1 - i)
        return carry

    lax.fori_loop(0, nc, scan, 0, unroll=2)
    for d in range(2):
        for g in range(SSD_NGROUPS):
            s_t = s_ref[d, g].T
            for hh in range(3):
                sfin_refs[d][0, 3 * g + hh] = s_t[hh * HEAD_DIM:(hh + 1) * HEAD_DIM,
                                                  g * SSD_STATE:(g + 1) * SSD_STATE]

    def finish(c, carry):
        r0 = pl.multiple_of(c * C, C)
        y = yacc_ref[pl.ds(r0, C), :] + act_ref[pl.ds(r0, C), 0:SSD_PAD] * d_ref[...]
        y = y * _silu(u_ref[pl.ds(r0, C), 0:SSD_PAD])
        for g in range(SSD_NGROUPS):
            yg = y[:, g * GW:(g + 1) * GW]
            ms = jnp.sum(yg * yg, axis=-1, keepdims=True) * (1.0 / (SSD_INNER // SSD_NGROUPS))
            yn = yg * lax.rsqrt(ms + EPS) * nw_ref[:, g * GW:(g + 1) * GW]
            y_ref[pl.ds(r0, C), g * GW:(g + 1) * GW] = yn.astype(y_ref.dtype)
        return carry

    lax.fori_loop(0, nc, finish, 0)


def _ssd(u_ssd, row0, nb, L, prm, layer, use_rope, s0):
    blk0 = row0 // L
    tril2, exp2, colb2 = _ssd_constants()
    consts = [jnp.asarray(tril2, dtype=BF16), jnp.asarray(exp2, dtype=BF16), jnp.asarray(colb2, dtype=BF16)]
    full = lambda a: pl.BlockSpec(a.shape, lambda b, _n=a.ndim: (0,) * _n)
    of_layer = lambda a: pl.BlockSpec((None,) + a.shape[1:], lambda b, _n=a.ndim: (layer,) + (0,) * (_n - 1))
    per_layer = [prm[k] for k in ("conv_w", "conv_b", "dt_bias", "a_row", "a_x", "d_row", "norm_w")]
    args = [u_ssd] + per_layer + consts
    in_specs = ([pl.BlockSpec((L, U_SSD), lambda b: (blk0 + b, 0))] + [of_layer(a) for a in per_layer]
                + [full(a) for a in consts])
    if use_rope:
        cos, sin = _rope_tables(L)
        tabs = [jnp.asarray(cos), jnp.asarray(sin)]
        args += tabs
        in_specs += [full(a) for a in tabs]
    state_block = (SSD_HEADS, HEAD_DIM, SSD_STATE)
    if s0 is not None:
        args += list(s0)
        in_specs += [pl.BlockSpec((1, 1) + state_block, lambda b: (b, layer, 0, 0, 0))] * 2
    sshape = (2, SSD_NGROUPS, LANES, SSD_PAD // SSD_NGROUPS)
    return pl.pallas_call(
        functools.partial(_ssd_kernel, L, use_rope, s0 is not None),
        grid=(nb,),
        in_specs=in_specs,
        out_specs=[pl.BlockSpec((L, SSD_PAD), lambda b: (b, 0))]
                  + [pl.BlockSpec((1,) + state_block, lambda b: (b, 0, 0, 0))] * 2,
        out_shape=[jax.ShapeDtypeStruct((nb * L, SSD_PAD), BF16)]
                  + [jax.ShapeDtypeStruct((nb,) + state_block, F32)] * 2,
        scratch_shapes=[pltpu.VMEM((L, SSD_PAD + 2 * LANES), F32),
                        pltpu.VMEM((L, LANES), F32),
                        pltpu.VMEM(sshape, F32),
                        pltpu.VMEM((L, SSD_PAD), F32)],
        compiler_params=_cparams(("parallel",)),
        name="ssd",
    )(*args)


def _pad_heads(a, axis=-1):
    a = jnp.moveaxis(a, axis, -1)
    lead = a.shape[:-1]
    a = a.reshape(lead + (SSD_NGROUPS, 3, HEAD_DIM))
    a = jnp.pad(a, [(0, 0)] * len(lead) + [(0, 0), (0, 1), (0, 0)])
    return jnp.moveaxis(a.reshape(lead + (SSD_PAD,)), -1, axis)


def _pad_dt_lanes(a):
    lead = a.shape[:-2]
    a = a.reshape(lead + (2, SSD_NGROUPS, 3))
    a = jnp.pad(a, [(0, 0)] * len(lead) + [(0, 0), (0, 0), (0, 1)]).reshape(lead + (2 * SLOTS,))
    return jnp.pad(a, [(0, 0)] * len(lead) + [(0, LANES - 2 * SLOTS)])


W_IN_TR = 256


def _w_in_relayout_kernel(w_ref, pdt_ref, o_ref):
    half_group = SSD_INNER // SSD_NGROUPS
    gw = SSD_PAD // SSD_NGROUPS
    zero = jnp.zeros((W_IN_TR, gw - half_group), o_ref.dtype)

    def put(dst, v):
        o_ref[:, dst:dst + v.shape[1]] = v.astype(o_ref.dtype)

    put(0, w_ref[:, 0:FN_WIDTH])
    for seg in range(2):
        for g in range(SSD_NGROUPS):
            src = FN_WIDTH + seg * SSD_INNER + g * half_group
            dst = U_FN + seg * SSD_PAD + g * gw
            put(dst, w_ref[:, src:src + half_group])
            put(dst + half_group, zero)
    src = FN_WIDTH + 2 * SSD_INNER
    put(U_FN + 2 * SSD_PAD, w_ref[:, src:src + 2 * LANES])
    src += 2 * LANES
    put(U_FN + 2 * SSD_PAD + 2 * LANES,
        jnp.dot(w_ref[:, src:src + LANES].astype(BF16), pdt_ref[...], preferred_element_type=F32))
    src += 2 * SSD_HEADS
    put(U_FN + U_SSD, w_ref[:, src:src + NA_WIDTH] * (HEAD_DIM ** -0.5))
    put(U_FN + U_SSD + NA_WIDTH, w_ref[:, src + NA_WIDTH:src + 3 * NA_WIDTH])


def _w_in_relayout(w_in):
    pdt = np.zeros((LANES, LANES), np.float32)
    for d in range(2):
        for h in range(SSD_HEADS):
            pdt[SSD_HEADS * d + h, SLOTS * d + _slot_of_head(h)] = 1.0
    return pl.pallas_call(
        _w_in_relayout_kernel,
        grid=(DEPTH, D_MODEL // W_IN_TR),
        in_specs=[pl.BlockSpec((None, W_IN_TR, w_in.shape[-1]), lambda l, i: (l, i, 0)),
                  pl.BlockSpec((LANES, LANES), lambda l, i: (0, 0))],
        out_specs=pl.BlockSpec((None, W_IN_TR, U_TOTAL), lambda l, i: (l, i, 0)),
        out_shape=jax.ShapeDtypeStruct((DEPTH, D_MODEL, U_TOTAL), BF16),
        compiler_params=_cparams(("parallel", "parallel")),
        name="w_in_relayout",
    )(w_in, jnp.asarray(pdt, dtype=BF16))


def _mixer_params(w_in, w_out, ssd_conv_w, ssd_conv_b, ssd_dt_bias, ssd_a_log, ssd_d, ssd_norm):
    w_in_pad = _w_in_relayout(w_in)
    a = -jnp.exp(ssd_a_log.astype(F32))
    a_x = jnp.repeat(_pad_dt_lanes(a)[:, :2 * SLOTS].reshape(DEPTH, 2, SLOTS), HEAD_DIM, axis=-1)
    ssd = {
        "conv_w": jnp.concatenate([_pad_heads(ssd_conv_w[..., :SSD_INNER]), ssd_conv_w[..., SSD_INNER:]], axis=-1),
        "conv_b": jnp.concatenate([_pad_heads(ssd_conv_b[..., :SSD_INNER]),
                                   ssd_conv_b[..., SSD_INNER:]], axis=-1)[:, None, :],
        "dt_bias": _pad_dt_lanes(ssd_dt_bias)[:, None, :],
        "a_row": _pad_dt_lanes(a)[:, None, :],
        "a_x": a_x.reshape(DEPTH, 2, 1, SSD_PAD),
        "d_row": _pad_heads(jnp.repeat(ssd_d, HEAD_DIM, axis=-1))[:, None, :],
        "norm_w": _pad_heads(ssd_norm)[:, None, :],
    }
    return {
        "w_in": w_in_pad,
        "w_out_fn": w_out[:, :FN_WIDTH].astype(BF16),
        "w_out_ssd": _pad_heads(w_out[:, FN_WIDTH:FN_WIDTH + SSD_INNER], axis=1).astype(BF16),
        "w_out_att": w_out[:, FN_WIDTH + SSD_INNER:].astype(BF16),
        "ssd": ssd,
    }


def _pick_tile(rows, want):
    t = min(rows, want)
    while rows % t:
        t //= 2
    return t


def kernel(x_prompt, x_sample, c, state_ssd_fwd, state_ssd_bwd, cache_attn_k, cache_attn_v, c_ctx, mod_w, mod_b, norm_pre, norm_post, ffn_w13, ffn_w2, w_in, w_out, ssd_conv_w, ssd_conv_b, ssd_dt_bias, ssd_a_log, ssd_d, ssd_norm, na_rpb):
    nbp, lp, _ = x_prompt.shape
    nbs, ls, _ = x_sample.shape
    n_ctx, n_lat = nbp * lp, nbs * ls
    assert n_ctx == n_lat and n_ctx % ls == 0 and nbs + 1 <= 8
    rpg = ls
    xs = (x_prompt.reshape(n_ctx, D_MODEL), x_sample.reshape(n_lat, D_MODEL))

    cvec = jnp.zeros((8, D_MODEL), F32).at[0].set(c_ctx).at[1:1 + nbs].set(c)
    mods = _mods(cvec, mod_w, mod_b).reshape(DEPTH, 8, N_MOD, D_MODEL)
    w13 = ffn_w13.astype(BF16)
    w2 = ffn_w2.astype(BF16)
    p = _mixer_params(w_in, w_out, ssd_conv_w, ssd_conv_b, ssd_dt_bias, ssd_a_log, ssd_d, ssd_norm)
    to_feature_major = lambda a: jnp.transpose(a, (0, 1, 3, 4, 2)).reshape(a.shape[0], DEPTH, NA_WIDTH, a.shape[2])
    from_feature_major = lambda a: jnp.transpose(
        a.reshape(a.shape[0], DEPTH, NA_HEADS, HEAD_DIM, a.shape[3]), (0, 1, 4, 2, 3))
    cache_k = to_feature_major(cache_attn_k)
    cache_v = to_feature_major(cache_attn_v)
    s0 = (state_ssd_fwd.astype(F32), state_ssd_bwd.astype(F32))
    na_bias = _na_bias_inputs(na_rpb, ls // GRID_W)
    tm_ffn = _pick_tile(rpg, 512)
    tm_proj = _pick_tile(rpg, 512)

    new_sf, new_sb, new_k, new_v = [], [], [], []
    for l in range(DEPTH):
        gm = jnp.concatenate([jnp.broadcast_to(mods[l, 0], (n_ctx // rpg, N_MOD, D_MODEL)),
                              mods[l, 1:1 + nbs]], axis=0)
        xs = _ffn(xs, gm, 0, norm_pre[l, 0], norm_post[l, 0], w13, w2, l, 0, rpg, tm_ffn)
        u_fn, u_ssd, u_qkv, k_ctx, v_ctx = _inproj(xs, gm[:, 3:6], norm_pre[l, 1], p["w_in"], l, rpg,
                                                   tm_proj, lp)

        y_ssd_c, sf_c, sb_c = _ssd(u_ssd, 0, nbp, lp, p["ssd"], l, False, None)
        mix_ctx = (_fourier(u_fn, 0, nbp, lp), y_ssd_c, _ctx_attn(u_qkv, nbp, lp))
        new_sf.append(sf_c)
        new_sb.append(sb_c)
        new_k.append(k_ctx)
        new_v.append(v_ctx)

        y_ssd_l, _, _ = _ssd(u_ssd, n_ctx, nbs, ls, p["ssd"], l, True, s0)
        mix_lat = (_fourier(u_fn, n_ctx, nbs, ls), y_ssd_l,
                   _na_attn(u_qkv, n_ctx, nbs, ls, cache_k, cache_v, l, na_bias))

        mix = (mix_ctx, mix_lat, (p["w_out_fn"], p["w_out_ssd"], p["w_out_att"]), norm_post[l, 1])
        xs = _ffn(xs, gm, 6, norm_pre[l, 2], norm_post[l, 2], w13, w2, l, 1, rpg, tm_ffn, mix=mix)

    return (xs[0].reshape(nbp, lp, D_MODEL), xs[1].reshape(nbs, ls, D_MODEL),
            jnp.stack(new_sf, axis=1), jnp.stack(new_sb, axis=1),
            from_feature_major(jnp.stack(new_k, axis=1)), from_feature_major(jnp.stack(new_v, axis=1)))
```

```python
import functools
import math

import numpy as np
import jax
import jax.numpy as jnp
from jax import lax
from jax.experimental import pallas as pl
from jax.experimental.pallas import tpu as pltpu

F32 = jnp.float32
BF16 = jnp.bfloat16

D_MODEL = 1024
DEPTH = 2
GRID_W = 64
FF_HIDDEN = 2816
N_MOD = 9
HEAD_DIM = 64
FN_WIDTH = 256
FN_GROUPS = 4
SSD_INNER = 384
SSD_HEADS = 6
SSD_STATE = 64
SSD_NGROUPS = 2
SSD_CONV = 5
SSD_CHUNK = 128
SSD_CONV_DIM = 640
SSD_IN = 1420
NA_WIDTH = 384
NA_HEADS = 6
NA_WIN_ROWS = 8
NA_WIN_COLS = 16
ROPE_BASE = 10000.0
EPS = 1e-6

LANES = 128
VMEM_LIMIT = 56 * 1024 * 1024

SLOTS = 8
SSD_PAD = SLOTS * HEAD_DIM
U_FN = FN_WIDTH
U_SSD = 2 * SSD_PAD + 2 * LANES + LANES
U_QKV = 3 * NA_WIDTH
U_TOTAL = U_FN + U_SSD + U_QKV
NEG_BIG = -1e30


def _slot_of_head(h):
    return 4 * (h // 3) + (h % 3)


def _cparams(sem):
    return pltpu.CompilerParams(dimension_semantics=sem, vmem_limit_bytes=VMEM_LIMIT)


def _rms(x):
    return x * lax.rsqrt(jnp.mean(x * x, axis=-1, keepdims=True) + EPS)


def _silu(x):
    return x * jax.nn.sigmoid(x)


def _bdot(a, b):
    return jnp.dot(a.astype(BF16), b.astype(BF16), preferred_element_type=F32)


def _bdot_nt(a, b):
    return lax.dot_general(a.astype(BF16), b.astype(BF16), (((1,), (1,)), ((), ())),
                           preferred_element_type=F32)


MOD_TN = 1152


def _mods_kernel(c_ref, w_ref, b_ref, o_ref):
    s = _silu(c_ref[...])
    o_ref[0] = _bdot(s, w_ref[0]) + b_ref[0]


def _mods(cvec, mod_w, mod_b):
    ncol = N_MOD * D_MODEL
    return pl.pallas_call(
        _mods_kernel,
        grid=(DEPTH, ncol // MOD_TN),
        in_specs=[pl.BlockSpec((8, D_MODEL), lambda l, j: (0, 0)),
                  pl.BlockSpec((1, D_MODEL, MOD_TN), lambda l, j: (l, 0, j)),
                  pl.BlockSpec((1, 1, MOD_TN), lambda l, j: (l, 0, j))],
        out_specs=pl.BlockSpec((1, 8, MOD_TN), lambda l, j: (l, 0, j)),
        out_shape=jax.ShapeDtypeStruct((DEPTH, 8, ncol), F32),
        compiler_params=_cparams(("parallel", "parallel")),
        name="mods",
    )(cvec, mod_w, mod_b.reshape(DEPTH, 1, ncol))


FFN_TH = 256


def _halves(nhalf):
    first = lambda i, *_: (jnp.minimum(i, nhalf - 1), 0)
    second = lambda i, *_: (jnp.maximum(i - nhalf, 0), 0)
    return first, second


def _on_half(nhalf, fn):
    i = pl.program_id(0)
    pl.when(i < nhalf)(functools.partial(fn, 0))
    pl.when(i >= nhalf)(functools.partial(fn, 1))


def _ffn_kernel(nhalf, mrow, with_mix, *refs):
    xa_ref, xb_ref, m_ref, gpre_ref, gpost_ref, w13_ref, w2_ref = refs[:7]
    pos = 7
    if with_mix:
        mix_refs = (refs[pos:pos + 3], refs[pos + 3:pos + 6])
        wmix_refs = refs[pos + 6:pos + 9]
        gmix_ref = refs[pos + 9]
        pos += 10
    oa_ref, ob_ref, h_ref, acc_ref = refs[pos:pos + 4]
    x_refs = (xa_ref, xb_ref)
    o_refs = (oa_ref, ob_ref)
    if with_mix:
        x1_ref = refs[pos + 4]

    def body(half):
        x = x_refs[half][...]
        if with_mix:
            y = None
            for y_ref, w_ref in zip(mix_refs[half], wmix_refs):
                d = jnp.dot(y_ref[...], w_ref[...], preferred_element_type=F32)
                y = d if y is None else y + d
            x = x + _rms(y) * (gmix_ref[...] * m_ref[0, mrow - 1:mrow, :])
            x1_ref[...] = x
        shift = m_ref[0, mrow:mrow + 1, :]
        scale = m_ref[0, mrow + 1:mrow + 2, :]
        h_ref[...] = (_rms(x) * (gpre_ref[...] * (1.0 + scale)) + shift).astype(BF16)

        h = h_ref[...]
        for j in range(FF_HIDDEN // FFN_TH):
            cols = slice(j * FFN_TH, (j + 1) * FFN_TH)
            g = jnp.dot(h, w13_ref[:, cols], preferred_element_type=F32)
            u = jnp.dot(h, w13_ref[:, FF_HIDDEN + j * FFN_TH:FF_HIDDEN + (j + 1) * FFN_TH],
                        preferred_element_type=F32)
            a = (_silu(g) * u).astype(BF16)
            part = jnp.dot(a, w2_ref[cols, :], preferred_element_type=F32)
            if j == 0:
                acc_ref[...] = part
            else:
                acc_ref[...] += part

        gate = m_ref[0, mrow + 2:mrow + 3, :]
        y = _rms(acc_ref[...]) * (gpost_ref[...] * (0.5 * gate))
        x = x1_ref[...] if with_mix else x_refs[half][...]
        o_refs[half][...] = x + y

    _on_half(nhalf, body)


def _ffn(xs, gmods, mrow, g_pre, g_post, w13, w2, layer, sub, rows_per_group, tm, mix=None):
    nh = xs[0].shape[0]
    nhalf = nh // tm
    tpg = rows_per_group // tm
    first, second = _halves(nhalf)
    resident = pl.Buffered(1)
    row = lambda a: a.reshape(1, D_MODEL)
    args = [xs[0], xs[1], gmods, row(g_pre), row(g_post), w13, w2]
    in_specs = [pl.BlockSpec((tm, D_MODEL), first),
                pl.BlockSpec((tm, D_MODEL), second),
                pl.BlockSpec((1, N_MOD, D_MODEL), lambda i: (i // tpg, 0, 0)),
                pl.BlockSpec((1, D_MODEL), lambda i: (0, 0)),
                pl.BlockSpec((1, D_MODEL), lambda i: (0, 0)),
                pl.BlockSpec((None, None, D_MODEL, 2 * FF_HIDDEN), lambda i: (layer, sub, 0, 0),
                             pipeline_mode=resident),
                pl.BlockSpec((None, None, FF_HIDDEN, D_MODEL), lambda i: (layer, sub, 0, 0),
                             pipeline_mode=resident)]
    scratch = [pltpu.VMEM((tm, D_MODEL), BF16), pltpu.VMEM((tm, D_MODEL), F32)]
    if mix is not None:
        mix_ctx, mix_lat, w_mix, g_post_mix = mix
        args += list(mix_ctx) + list(mix_lat) + list(w_mix) + [row(g_post_mix)]
        in_specs += ([pl.BlockSpec((tm, a.shape[1]), first) for a in mix_ctx]
                     + [pl.BlockSpec((tm, a.shape[1]), second) for a in mix_lat]
                     + [pl.BlockSpec((None,) + w.shape[1:], lambda i: (layer, 0, 0), pipeline_mode=resident)
                        for w in w_mix]
                     + [pl.BlockSpec((1, D_MODEL), lambda i: (0, 0))])
        scratch.append(pltpu.VMEM((tm, D_MODEL), F32))
    return pl.pallas_call(
        functools.partial(_ffn_kernel, nhalf, mrow, mix is not None),
        grid=(2 * nhalf,),
        in_specs=in_specs,
        out_specs=[pl.BlockSpec((tm, D_MODEL), first), pl.BlockSpec((tm, D_MODEL), second)],
        out_shape=[jax.ShapeDtypeStruct((nh, D_MODEL), F32)] * 2,
        scratch_shapes=scratch,
        compiler_params=_cparams(("arbitrary",)),
        name="ffn",
    )(*args)


def _inproj_kernel(nhalf, layer, first_layer, *refs):
    xa_ref, xb_ref, m_ref, gpre_ref, w_ref = refs[:5]
    ofn_ref, ossd_ref, oqkv_ref, ok_ref, ov_ref = refs[-5:]
    x_refs = (xa_ref, xb_ref)

    def body(half):
        shift = m_ref[0, 0:1, :]
        scale = m_ref[0, 1:2, :]
        h = (_rms(x_refs[half][...]) * (gpre_ref[...] * (1.0 + scale)) + shift).astype(BF16)
        u = jnp.dot(h, w_ref[...], preferred_element_type=F32)
        ofn_ref[...] = u[:, :U_FN].astype(ofn_ref.dtype)
        ossd_ref[...] = u[:, U_FN:U_FN + U_SSD]
        oqkv_ref[...] = u[:, U_FN + U_SSD:].astype(oqkv_ref.dtype)
        if half == 0:
            nseq, seq_len = ok_ref.shape[0], ok_ref.shape[-1]
            for s in range(nseq):
                rows = slice(s * seq_len, (s + 1) * seq_len)
                k_t = u[rows, U_FN + U_SSD + NA_WIDTH:U_FN + U_SSD + 2 * NA_WIDTH].T
                v_t = u[rows, U_FN + U_SSD + 2 * NA_WIDTH:].T
                if first_layer:
                    for l2 in range(ok_ref.shape[1]):
                        ok_ref[s, l2] = k_t if l2 == layer else jnp.zeros_like(k_t)
                        ov_ref[s, l2] = v_t if l2 == layer else jnp.zeros_like(v_t)
                else:
                    ok_ref[s] = k_t
                    ov_ref[s] = v_t

    _on_half(nhalf, body)


def _inproj(xs, gmods, g_pre, w_in_pad, layer, rows_per_group, tm, ctx_len, caches=None):
    nh = xs[0].shape[0]
    n = 2 * nh
    nhalf = nh // tm
    tpg = rows_per_group // tm
    first, second = _halves(nhalf)
    assert tm % ctx_len == 0
    seq_per_tile = tm // ctx_len
    cache_shape = (nh // ctx_len, DEPTH, NA_WIDTH, ctx_len)
    args = [xs[0], xs[1], gmods, g_pre.reshape(1, D_MODEL), w_in_pad]
    in_specs = [pl.BlockSpec((tm, D_MODEL), first),
                pl.BlockSpec((tm, D_MODEL), second),
                pl.BlockSpec((1, 3, D_MODEL), lambda i: (i // tpg, 0, 0)),
                pl.BlockSpec((1, D_MODEL), lambda i: (0, 0)),
                pl.BlockSpec((None, D_MODEL, U_TOTAL), lambda i: (layer, 0, 0))]
    if caches is None:
        cache_spec = pl.BlockSpec((seq_per_tile, DEPTH, NA_WIDTH, ctx_len),
                                  lambda i: (jnp.minimum(i, nhalf - 1), 0, 0, 0))
        aliases = {}
    else:
        cache_spec = pl.BlockSpec((seq_per_tile, None, NA_WIDTH, ctx_len),
                                  lambda i: (jnp.minimum(i, nhalf - 1), layer, 0, 0))
        aliases = {len(args): 3, len(args) + 1: 4}
        args += list(caches)
        in_specs += [pl.BlockSpec(memory_space=pl.ANY)] * 2
    return pl.pallas_call(
        functools.partial(_inproj_kernel, nhalf, layer, caches is None),
        grid=(n // tm,),
        in_specs=in_specs,
        out_specs=[pl.BlockSpec((tm, U_FN), lambda i: (i, 0)),
                   pl.BlockSpec((tm, U_SSD), lambda i: (i, 0)),
                   pl.BlockSpec((tm, U_QKV), lambda i: (i, 0)),
                   cache_spec, cache_spec],
        out_shape=[jax.ShapeDtypeStruct((n, U_FN), BF16),
                   jax.ShapeDtypeStruct((n, U_SSD), F32),
                   jax.ShapeDtypeStruct((n, U_QKV), BF16),
                   jax.ShapeDtypeStruct(cache_shape, F32),
                   jax.ShapeDtypeStruct(cache_shape, F32)],
        input_output_aliases=aliases,
        compiler_params=_cparams(("arbitrary",)),
        name="inproj",
    )(*args)


def _dft_tables(L):
    k = np.arange(L, dtype=np.int64)
    ang = 2.0 * np.pi * ((k[:, None] * k[None, :]) % L).astype(np.float64) / L
    sc = 1.0 / math.sqrt(L * HEAD_DIM)
    cl = (np.cos(ang) * sc).astype(np.float32)
    sl = (-np.sin(ang) * sc).astype(np.float32)
    m = np.arange(HEAD_DIM, dtype=np.int64)
    a64 = 2.0 * np.pi * ((m[:, None] * m[None, :]) % HEAD_DIM).astype(np.float64) / HEAD_DIM
    eye = np.eye(FN_GROUPS)
    w1 = np.concatenate([np.kron(eye, np.cos(a64)), np.kron(eye, np.sin(a64))], axis=1).astype(np.float32)
    return cl, sl, w1


def _fourier_kernel(nseq, L, tl, u_ref, w1_ref, cl_ref, sl_ref, o_ref, ab_ref):
    rt = pl.program_id(0)
    g = pl.program_id(1)

    @pl.when(rt == 0)
    def _():
        ab_ref[g] = _bdot(u_ref[...], w1_ref[...]).astype(BF16)

    for s in range(nseq):
        ab = ab_ref[g, s * L:(s + 1) * L, :]
        y = (jnp.dot(cl_ref[...], ab[:, :FN_WIDTH], preferred_element_type=F32)
             + jnp.dot(sl_ref[...], ab[:, FN_WIDTH:], preferred_element_type=F32))
        o_ref[s * tl:(s + 1) * tl, :] = y.astype(o_ref.dtype)


def _fourier(u_fn, row0, nb, L):
    cl, sl, w1 = _dft_tables(L)
    cl = jnp.asarray(cl).astype(BF16)
    sl = jnp.asarray(sl).astype(BF16)
    w1 = jnp.asarray(w1).astype(BF16)
    tl = min(L, 1024)
    nrt = L // tl
    nseq = max(1, min(nb, 2048 // L)) if nrt == 1 else 1
    while nb % nseq:
        nseq -= 1
    ng = nb // nseq
    blk0 = row0 // (nseq * L)
    assert row0 % (nseq * L) == 0
    return pl.pallas_call(
        functools.partial(_fourier_kernel, nseq, L, tl),
        grid=(nrt, ng),
        in_specs=[pl.BlockSpec((nseq * L, FN_WIDTH), lambda rt, g: (blk0 + jnp.where(rt == 0, g, ng - 1), 0)),
                  pl.BlockSpec((FN_WIDTH, 2 * FN_WIDTH), lambda rt, g: (0, 0)),
                  pl.BlockSpec((tl, L), lambda rt, g: (rt, 0)),
                  pl.BlockSpec((tl, L), lambda rt, g: (rt, 0))],
        out_specs=pl.BlockSpec((nseq * tl, FN_WIDTH), lambda rt, g: (g * nrt + rt, 0)),
        out_shape=jax.ShapeDtypeStruct((nb * L, FN_WIDTH), BF16),
        scratch_shapes=[pltpu.VMEM((ng, nseq * L, 2 * FN_WIDTH), BF16)],
        compiler_params=_cparams(("arbitrary", "arbitrary")),
        name="fourier",
    )(u_fn, w1, cl, sl)


def _head_masks():
    lane = lax.broadcasted_iota(jnp.int32, (1, LANES), 1)
    return lane < HEAD_DIM


def _ctx_attn_kernel(nseq, L, q_ref, k_ref, v_ref, o_ref):
    low = _head_masks()
    for t in range(nseq):
        rows = slice(t * L, (t + 1) * L)
        for p in range(NA_HEADS // 2):
            lanes = slice(p * LANES, (p + 1) * LANES)
            q = q_ref[rows, lanes]
            k = k_ref[rows, lanes].astype(BF16)
            v = v_ref[rows, lanes].astype(BF16)
            q2 = jnp.concatenate([jnp.where(low, q, 0.0), jnp.where(low, 0.0, q)], axis=0)
            s = _bdot_nt(q2, k)
            m = jnp.max(s, axis=-1, keepdims=True)
            e = jnp.exp(s - m)
            inv = 1.0 / jnp.sum(e, axis=-1, keepdims=True)
            o2 = jnp.dot(e.astype(BF16), v, preferred_element_type=F32) * inv
            o_ref[rows, lanes] = jnp.where(low, o2[:L], o2[L:]).astype(o_ref.dtype)


def _ctx_attn(u_qkv, nb, L):
    nseq = 4 if nb % 4 == 0 else 1
    return pl.pallas_call(
        functools.partial(_ctx_attn_kernel, nseq, L),
        grid=(nb // nseq,),
        in_specs=[pl.BlockSpec((nseq * L, NA_WIDTH), lambda b: (b, 0)),
                  pl.BlockSpec((nseq * L, NA_WIDTH), lambda b: (b, 1)),
                  pl.BlockSpec((nseq * L, NA_WIDTH), lambda b: (b, 2))],
        out_specs=pl.BlockSpec((nseq * L, NA_WIDTH), lambda b: (b, 0)),
        out_shape=jax.ShapeDtypeStruct((nb * L, NA_WIDTH), BF16),
        compiler_params=_cparams(("parallel",)),
        name="ctx_attn",
    )(u_qkv, u_qkv, u_qkv)


NA_QR = 8
NA_QC = 16
NA_KR = 16
NA_KC = 32
NA_TQ = NA_QR * NA_QC
NA_TK = NA_KR * NA_KC


def _na_tile_geometry(rows):
    wr = min(NA_WIN_ROWS, rows)
    n_rb = rows // NA_QR
    n_cb = GRID_W // NA_QC
    rb = [0, 1, n_rb - 1]
    cb = [0, 1, n_cb - 1]
    dr = np.zeros((3, NA_QR, NA_KR), np.int64)
    vr = np.zeros((3, NA_QR, NA_KR), bool)
    for ci, i in enumerate(rb):
        kr0 = int(np.clip(NA_QR * i - NA_WIN_ROWS // 2, 0, rows - NA_KR))
        for rr in range(NA_QR):
            r = NA_QR * i + rr
            rs = int(np.clip(r - wr // 2, 0, rows - wr))
            for kk in range(NA_KR):
                kr = kr0 + kk
                vr[ci, rr, kk] = rs <= kr < rs + wr
                dr[ci, rr, kk] = np.clip(kr - r + NA_WIN_ROWS - 1, 0, 2 * NA_WIN_ROWS - 2)
    dc = np.zeros((3, NA_QC, NA_KC), np.int64)
    vc = np.zeros((3, NA_QC, NA_KC), bool)
    for ci, j in enumerate(cb):
        kc0 = int(np.clip(NA_QC * j - NA_WIN_COLS // 2, 0, GRID_W - NA_KC))
        for cq in range(NA_QC):
            c = NA_QC * j + cq
            cs = int(np.clip(c - NA_WIN_COLS // 2, 0, GRID_W - NA_WIN_COLS))
            for ck in range(NA_KC):
                kc = kc0 + ck
                vc[ci, cq, ck] = cs <= kc < cs + NA_WIN_COLS
                dc[ci, cq, ck] = np.clip(kc - c + NA_WIN_COLS - 1, 0, 2 * NA_WIN_COLS - 2)
    return dr, vr, dc, vc


def _na_bias_inputs(rpb, rows):
    dr, vr, dc, vc = _na_tile_geometry(rows)
    n_dc = 2 * NA_WIN_COLS - 1
    oh_c = (dc[..., None] == np.arange(n_dc)).astype(np.float32)
    oh_j = np.broadcast_to(oh_c[:, :, None], (3, NA_QC, NA_KR, NA_KC, n_dc)).reshape(3, NA_QC, NA_TK, n_dc)
    vq = jnp.einsum("lhab,ycjb->lhaycj", rpb.astype(F32), jnp.asarray(oh_j), precision=lax.Precision.HIGHEST)
    valid_c = np.broadcast_to(vc[:, :, None], (3, NA_QC, NA_KR, NA_KC)).reshape(3, NA_QC, NA_TK)
    vq = jnp.where(jnp.asarray(valid_c)[None, None, None], vq, NEG_BIG)
    drt = np.where(vr, dr, -1)[..., None]
    drt = np.broadcast_to(drt, (3, NA_QR, NA_KR, NA_KC)).reshape(3, NA_QR, NA_TK).astype(np.int32)
    offsets = [[sorted(set(dr[x, rr][vr[x, rr]].tolist())) for rr in range(NA_QR)] for x in range(3)]
    return vq, jnp.asarray(drt), offsets


def _na_kernel(rows, offsets, q_ref, k16_ref, v16_ref, kc_ref, vc_ref, vq_ref, drt_ref, o_ref,
               bias_ref, k_ref, v_ref):
    k_ref[...] = k16_ref[...].astype(F32)
    v_ref[...] = v16_ref[...].astype(F32)
    n_rb = rows // NA_QR
    n_cb = GRID_W // NA_QC
    low = _head_masks()
    kctx_t = kc_ref[0, 0].astype(BF16)
    vctx_t = vc_ref[0, 0].astype(BF16)

    @pl.when(pl.program_id(1) == 0)
    def _():
        def build(t, carry):
            hh = t // 3
            cc = t % 3
            for rc in range(3):
                for rr in range(NA_QR):
                    drrow = drt_ref[rc, rr:rr + 1, :]
                    acc = jnp.full((NA_QC, NA_TK), NEG_BIG, F32)
                    for a in offsets[rc][rr]:
                        acc = jnp.where(drrow == a, vq_ref[hh, a, cc], acc)
                    row0 = pl.multiple_of(hh * NA_TQ + rr * NA_QC, NA_QC)
                    bias_ref[rc, cc, pl.ds(row0, NA_QC), :] = acc
            return carry

        lax.fori_loop(0, 6, build, 0)

    def row_block(i, carry):
        kr0 = jnp.clip(NA_QR * i - NA_WIN_ROWS // 2, 0, rows - NA_KR)
        rcfg = jnp.where(i > 0, 1, 0) + jnp.where(i == n_rb - 1, 1, 0)
        for j in range(n_cb):
            kc0 = int(np.clip(NA_QC * j - NA_WIN_COLS // 2, 0, GRID_W - NA_KC))
            ccfg = 0 if j == 0 else (2 if j == n_cb - 1 else 1)
            q_parts = []
            for rr in range(NA_QR):
                start = pl.multiple_of((NA_QR * i + rr) * GRID_W + NA_QC * j, NA_QC)
                q_parts.append(q_ref[pl.ds(start, NA_QC), :])
            q = jnp.concatenate(q_parts, axis=0)
            k_parts, v_parts = [], []
            for kk in range(NA_KR):
                start = pl.multiple_of((kr0 + kk) * GRID_W + kc0, 8)
                k_parts.append(k_ref[pl.ds(start, NA_KC), :])
                v_parts.append(v_ref[pl.ds(start, NA_KC), :])
            k = jnp.concatenate(k_parts, axis=0).astype(BF16)
            v = jnp.concatenate(v_parts, axis=0).astype(BF16)
            q2 = jnp.concatenate([jnp.where(low, q, 0.0), jnp.where(low, 0.0, q)], axis=0).astype(BF16)
            s_loc = _bdot_nt(q2, k) + bias_ref[rcfg, ccfg]
            s_ctx = jnp.dot(q2, kctx_t, preferred_element_type=F32)
            m = jnp.maximum(jnp.max(s_loc, axis=-1, keepdims=True),
                            jnp.max(s_ctx, axis=-1, keepdims=True))
            p_loc = jnp.exp(s_loc - m)
            p_ctx = jnp.exp(s_ctx - m)
            l = jnp.sum(p_loc, axis=-1, keepdims=True) + jnp.sum(p_ctx, axis=-1, keepdims=True)
            o2 = (jnp.dot(p_loc.astype(BF16), v, preferred_element_type=F32)
                  + _bdot_nt(p_ctx, vctx_t)) * (1.0 / l)
            o = jnp.where(low, o2[:NA_TQ], o2[NA_TQ:])
            for rr in range(NA_QR):
                start = pl.multiple_of((NA_QR * i + rr) * GRID_W + NA_QC * j, NA_QC)
                o_ref[pl.ds(start, NA_QC), :] = o[rr * NA_QC:(rr + 1) * NA_QC, :].astype(o_ref.dtype)
        return carry

    lax.fori_loop(0, n_rb, row_block, 0, unroll=2)


def _na_attn(u_qkv, row0, nb, L, cache_k, cache_v, layer, bias_inputs):
    npair = NA_HEADS // 2
    rows = L // GRID_W
    blk0 = row0 // L
    lc = cache_k.shape[3]
    vq, drt, offsets = bias_inputs
    return pl.pallas_call(
        functools.partial(_na_kernel, rows, offsets),
        grid=(npair, nb),
        in_specs=[pl.BlockSpec((L, LANES), lambda p, b: (blk0 + b, p)),
                  pl.BlockSpec((L, LANES), lambda p, b: (blk0 + b, npair + p)),
                  pl.BlockSpec((L, LANES), lambda p, b: (blk0 + b, 2 * npair + p)),
                  pl.BlockSpec((1, 1, LANES, lc), lambda p, b: (b, layer, p, 0)),
                  pl.BlockSpec((1, 1, LANES, lc), lambda p, b: (b, layer, p, 0)),
                  pl.BlockSpec((None, 2) + vq.shape[2:], lambda p, b: (layer, p, 0, 0, 0, 0)),
                  pl.BlockSpec(drt.shape, lambda p, b: (0, 0, 0))],
        out_specs=pl.BlockSpec((L, LANES), lambda p, b: (b, p)),
        out_shape=jax.ShapeDtypeStruct((nb * L, NA_WIDTH), BF16),
        scratch_shapes=[pltpu.VMEM((3, 3, 2 * NA_TQ, NA_TK), F32),
                        pltpu.VMEM((L, LANES), F32), pltpu.VMEM((L, LANES), F32)],
        compiler_params=_cparams(("arbitrary", "arbitrary")),
        name="na_attn",
    )(u_qkv, u_qkv, u_qkv, cache_k, cache_v, vq, drt)


def _ssd_constants():
    tril = np.tril(np.ones((SSD_CHUNK, SSD_CHUNK), np.float32))
    expand = np.zeros((2, LANES, SSD_PAD), np.float32)
    colb = np.zeros((2, LANES, SSD_HEADS * LANES), np.float32)
    for d in range(2):
        for h in range(SSD_HEADS):
            s = _slot_of_head(h)
            expand[d, SLOTS * d + s, HEAD_DIM * s:HEAD_DIM * (s + 1)] = 1.0
            colb[d, SLOTS * d + s, LANES * h:LANES * (h + 1)] = 1.0
    two = lambda m: np.concatenate([m, m], axis=-2)
    return np.concatenate([tril, tril], axis=1), two(expand), two(colb)


def _rope_tables(L):
    t = np.arange(L)
    rows = (t // GRID_W).astype(np.float64)
    cols = (t % GRID_W).astype(np.float64)
    quarter = SSD_STATE // 4
    inv = ROPE_BASE ** (-np.arange(quarter, dtype=np.float64) / quarter)
    n = np.arange(SSD_STATE)
    pos = np.where(n[None, :] < SSD_STATE // 2, rows[:, None], cols[:, None])
    ang = pos * inv[n % quarter][None, :]
    first = (n % (SSD_STATE // 2)) < quarter
    cos = np.cos(ang)
    sin = np.where(first[None, :], -np.sin(ang), np.sin(ang))
    tile = lambda a: np.concatenate([a] * SSD_NGROUPS, axis=1).astype(np.float32)
    return tile(cos), tile(sin)


def _split2(x):
    hi = x.astype(BF16)
    lo = (x - hi.astype(F32)).astype(BF16)
    return hi, lo


def _ssd_kernel(L, use_rope, use_init, layer, create_states, *refs):
    (u_ref, convw_ref, convb_ref, dtb_ref, a_ref, ax_ref, d_ref, nw_ref,
     tril_ref, exp_ref, colb_ref) = refs[:11]
    pos = 11
    if use_rope:
        cos_ref, sin_ref = refs[pos:pos + 2]
        pos += 2
    if use_init:
        s0_refs = refs[pos:pos + 2]
        pos += 2
    y_ref = refs[-7]
    sfin_refs = refs[-6:-4]
    act_ref, dt_ref, s_ref, yacc_ref = refs[-4:]

    nc = L // SSD_CHUNK
    C = SSD_CHUNK
    GW = SSD_PAD // SSD_NGROUPS
    XC = SSD_PAD
    CONV_W = SSD_PAD + 2 * LANES
    DTC = XC + CONV_W
    HALO = 8

    lane = lax.broadcasted_iota(jnp.int32, (1, LANES), 1)
    first_q = (lane % (SSD_STATE // 2)) < (SSD_STATE // 4)

    def prep(c, carry):
        r0 = pl.multiple_of(c * C, C)
        main = u_ref[pl.ds(r0, C), XC:XC + CONV_W]
        pstart = pl.multiple_of(jnp.maximum(r0 - HALO, 0), HALO)
        nstart = pl.multiple_of(jnp.minimum(r0 + C, L - HALO), HALO)
        prev = u_ref[pl.ds(pstart, HALO), XC:XC + CONV_W] * jnp.where(c > 0, 1.0, 0.0)
        nxt = u_ref[pl.ds(nstart, HALO), XC:XC + CONV_W] * jnp.where(c < nc - 1, 1.0, 0.0)
        win = jnp.concatenate([prev, main, nxt], axis=0)
        acc = jnp.zeros((C, CONV_W), F32) + convb_ref[...]
        for k in range(SSD_CONV):
            off = HALO + k - SSD_CONV // 2
            acc = acc + win[off:off + C, :] * convw_ref[k:k + 1, :]
        act = _silu(acc)
        act_ref[pl.ds(r0, C), 0:SSD_PAD] = act[:, 0:SSD_PAD]
        for t in range(2):
            bc = act[:, SSD_PAD + t * LANES:SSD_PAD + (t + 1) * LANES]
            if use_rope:
                partner = jnp.where(first_q, pltpu.roll(bc, LANES - SSD_STATE // 4, 1),
                                    pltpu.roll(bc, SSD_STATE // 4, 1))
                bc = bc * cos_ref[pl.ds(r0, C), :] + partner * sin_ref[pl.ds(r0, C), :]
            act_ref[pl.ds(r0, C), SSD_PAD + t * LANES:SSD_PAD + (t + 1) * LANES] = bc
        raw = u_ref[pl.ds(r0, C), DTC:DTC + LANES] + dtb_ref[...]
        dt_ref[pl.ds(r0, C), :] = jnp.maximum(raw, 0.0) + jnp.log1p(jnp.exp(-jnp.abs(raw)))
        yacc_ref[pl.ds(r0, C), :] = jnp.zeros((C, SSD_PAD), F32)
        return carry

    lax.fori_loop(0, nc, prep, 0)

    if use_init:
        zero = jnp.zeros((HEAD_DIM, SSD_STATE), F32)
        for d in range(2):
            for g in range(SSD_NGROUPS):
                rows_pn = []
                for hh in range(3):
                    blk = s0_refs[d][0, 0, 3 * g + hh]
                    rows_pn.append(jnp.concatenate([blk, zero] if g == 0 else [zero, blk], axis=1))
                rows_pn.append(jnp.zeros((HEAD_DIM, LANES), F32))
                s_ref[d, g] = jnp.concatenate(rows_pn, axis=0).T
    else:
        s_ref[...] = jnp.zeros_like(s_ref)

    a_row = a_ref[...]
    li = lax.broadcasted_iota(jnp.int32, (C, C), 0)
    si = lax.broadcasted_iota(jnp.int32, (C, C), 1)
    causal = [li >= si, si >= li]
    low64 = lane < HEAD_DIM
    grp_mask = [low64, jnp.logical_not(low64)]

    def one_direction(d, c):
        r0 = pl.multiple_of(c * C, C)
        x = act_ref[pl.ds(r0, C), 0:SSD_PAD]
        bmat = act_ref[pl.ds(r0, C), SSD_PAD:SSD_PAD + LANES]
        cmat = act_ref[pl.ds(r0, C), SSD_PAD + LANES:SSD_PAD + 2 * LANES]
        dt = dt_ref[pl.ds(r0, C), :]
        dta = dt * a_row
        hi, lo = _split2(dta)
        cs = jnp.dot(tril_ref[...], jnp.concatenate([hi, lo], axis=0),
                     preferred_element_type=F32)
        q = cs if d == 0 else cs - dta
        expand = lambda v: jnp.dot(jnp.concatenate(_split2(v), axis=1), exp_ref[d],
                                   preferred_element_type=F32)
        dt_x = expand(dt)
        cs_x = expand(cs)
        end_x = cs_x[C - 1:C, :]
        if d == 0:
            off_scale = jnp.exp(cs_x)
            w_state = jnp.exp(end_x - cs_x)
        else:
            e_x = cs_x - dt_x * ax_ref[d]
            off_scale = jnp.exp(end_x - e_x)
            w_state = jnp.exp(e_x)
        chunk_decay = jnp.exp(end_x)
        xdt = x * dt_x
        xdt_b = xdt.astype(BF16)
        rhs_state = (xdt * w_state).astype(BF16)
        qcol = jnp.dot(jnp.concatenate(_split2(q), axis=1), colb_ref[d],
                       preferred_element_type=F32)
        q_t = q.T
        b_t = bmat.T.astype(BF16)
        b_b = bmat.astype(BF16)
        cms = [jnp.where(grp_mask[g], cmat, 0.0).astype(BF16) for g in range(SSD_NGROUPS)]
        gmats = lax.dot_general(jnp.concatenate(cms, axis=0), b_b, (((1,), (1,)), ((), ())),
                                preferred_element_type=F32)
        st_all = jnp.dot(b_t, rhs_state, preferred_element_type=F32)
        for g in range(SSD_NGROUPS):
            gmat = gmats[g * C:(g + 1) * C]
            s_old = s_ref[d, g]
            y_off = (jnp.dot(cms[g], s_old.astype(BF16), preferred_element_type=F32)
                     * off_scale[:, g * GW:(g + 1) * GW])
            ms = []
            for hh in range(3):
                h = 3 * g + hh
                slot = 4 * g + hh
                row = q_t[SLOTS * d + slot:SLOTS * d + slot + 1, :]
                col = qcol[:, h * LANES:(h + 1) * LANES]
                seg = (col - row) if d == 0 else (row - col)
                ms.append((gmat * jnp.exp(jnp.where(causal[d], seg, NEG_BIG))).astype(BF16))
            r01 = jnp.dot(jnp.concatenate(ms[:2], axis=0), xdt_b[:, 2 * g * LANES:(2 * g + 1) * LANES],
                          preferred_element_type=F32)
            r2 = jnp.dot(ms[2], xdt_b[:, (2 * g + 1) * LANES:(2 * g + 2) * LANES],
                         preferred_element_type=F32)
            y_g = jnp.concatenate([jnp.where(low64, r01[:C], r01[C:]), r2], axis=1) + y_off
            yacc_ref[pl.ds(r0, C), g * GW:(g + 1) * GW] += y_g
            s_ref[d, g] = (s_old * chunk_decay[:, g * GW:(g + 1) * GW]
                           + st_all[:, g * GW:(g + 1) * GW])

    def scan(i, carry):
        one_direction(0, i)
        one_direction(1, nc - 1 - i)
        return carry

    lax.fori_loop(0, nc, scan, 0, unroll=2)
    for d in range(2):
        out = sfin_refs[d].at[0, layer] if create_states else sfin_refs[d].at[0]
        if create_states:
            for l2 in range(sfin_refs[d].shape[1]):
                if l2 != layer:
                    sfin_refs[d][0, l2] = jnp.zeros(sfin_refs[d].shape[2:], F32)
        for g in range(SSD_NGROUPS):
            s_t = s_ref[d, g].T
            for hh in range(3):
                out[3 * g + hh] = s_t[hh * HEAD_DIM:(hh + 1) * HEAD_DIM, g * SSD_STATE:(g + 1) * SSD_STATE]

    def finish(c, carry):
        r0 = pl.multiple_of(c * C, C)
        y = yacc_ref[pl.ds(r0, C), :] + act_ref[pl.ds(r0, C), 0:SSD_PAD] * d_ref[...]
        y = y * _silu(u_ref[pl.ds(r0, C), 0:SSD_PAD])
        for g in range(SSD_NGROUPS):
            yg = y[:, g * GW:(g + 1) * GW]
            ms = jnp.sum(yg * yg, axis=-1, keepdims=True) * (1.0 / (SSD_INNER // SSD_NGROUPS))
            yn = yg * lax.rsqrt(ms + EPS) * nw_ref[:, g * GW:(g + 1) * GW]
            y_ref[pl.ds(r0, C), g * GW:(g + 1) * GW] = yn.astype(y_ref.dtype)
        return carry

    lax.fori_loop(0, nc, finish, 0)


def _ssd(u_ssd, row0, nb, L, prm, layer, use_rope, s0, states="layer"):
    blk0 = row0 // L
    tril2, exp2, colb2 = _ssd_constants()
    consts = [jnp.asarray(tril2, dtype=BF16), jnp.asarray(exp2, dtype=BF16), jnp.asarray(colb2, dtype=BF16)]
    full = lambda a: pl.BlockSpec(a.shape, lambda b, _n=a.ndim: (0,) * _n)
    of_layer = lambda a: pl.BlockSpec((None,) + a.shape[1:], lambda b, _n=a.ndim: (layer,) + (0,) * (_n - 1))
    per_layer = [prm[k] for k in ("conv_w", "conv_b", "dt_bias", "a_row", "a_x", "d_row", "norm_w")]
    args = [u_ssd] + per_layer + consts
    in_specs = ([pl.BlockSpec((L, U_SSD), lambda b: (blk0 + b, 0))] + [of_layer(a) for a in per_layer]
                + [full(a) for a in consts])
    if use_rope:
        cos, sin = _rope_tables(L)
        tabs = [jnp.asarray(cos), jnp.asarray(sin)]
        args += tabs
        in_specs += [full(a) for a in tabs]
    state_block = (SSD_HEADS, HEAD_DIM, SSD_STATE)
    if s0 is not None:
        args += list(s0)
        in_specs += [pl.BlockSpec((1, 1) + state_block, lambda b: (b, layer, 0, 0, 0))] * 2
    sshape = (2, SSD_NGROUPS, LANES, SSD_PAD // SSD_NGROUPS)
    aliases = {}
    if isinstance(states, str) and states == "layer":
        state_spec = pl.BlockSpec((1,) + state_block, lambda b: (b, 0, 0, 0))
        state_shape = (nb,) + state_block
    elif isinstance(states, str):
        state_spec = pl.BlockSpec((1, DEPTH) + state_block, lambda b: (b, 0, 0, 0, 0))
        state_shape = (nb, DEPTH) + state_block
    else:
        state_spec = pl.BlockSpec((1, None) + state_block, lambda b: (b, layer, 0, 0, 0))
        state_shape = (nb, DEPTH) + state_block
        aliases = {len(args): 1, len(args) + 1: 2}
        args += list(states)
        in_specs += [pl.BlockSpec(memory_space=pl.ANY)] * 2
    return pl.pallas_call(
        functools.partial(_ssd_kernel, L, use_rope, s0 is not None, layer,
                          isinstance(states, str) and states == "create"),
        grid=(nb,),
        in_specs=in_specs,
        out_specs=[pl.BlockSpec((L, SSD_PAD), lambda b: (b, 0)), state_spec, state_spec],
        out_shape=[jax.ShapeDtypeStruct((nb * L, SSD_PAD), BF16)]
                  + [jax.ShapeDtypeStruct(state_shape, F32)] * 2,
        input_output_aliases=aliases,
        scratch_shapes=[pltpu.VMEM((L, SSD_PAD + 2 * LANES), F32),
                        pltpu.VMEM((L, LANES), F32),
                        pltpu.VMEM(sshape, F32),
                        pltpu.VMEM((L, SSD_PAD), F32)],
        compiler_params=_cparams(("parallel",)),
        name="ssd",
    )(*args)


def _pad_heads(a, axis=-1):
    a = jnp.moveaxis(a, axis, -1)
    lead = a.shape[:-1]
    a = a.reshape(lead + (SSD_NGROUPS, 3, HEAD_DIM))
    a = jnp.pad(a, [(0, 0)] * len(lead) + [(0, 0), (0, 1), (0, 0)])
    return jnp.moveaxis(a.reshape(lead + (SSD_PAD,)), -1, axis)


def _pad_dt_lanes(a):
    lead = a.shape[:-2]
    a = a.reshape(lead + (2, SSD_NGROUPS, 3))
    a = jnp.pad(a, [(0, 0)] * len(lead) + [(0, 0), (0, 0), (0, 1)]).reshape(lead + (2 * SLOTS,))
    return jnp.pad(a, [(0, 0)] * len(lead) + [(0, LANES - 2 * SLOTS)])


W_IN_TR = 256


def _w_in_relayout_kernel(w_ref, pdt_ref, o_ref):
    half_group = SSD_INNER // SSD_NGROUPS
    gw = SSD_PAD // SSD_NGROUPS
    zero = jnp.zeros((W_IN_TR, gw - half_group), o_ref.dtype)

    def put(dst, v):
        o_ref[:, dst:dst + v.shape[1]] = v.astype(o_ref.dtype)

    put(0, w_ref[:, 0:FN_WIDTH])
    for seg in range(2):
        for g in range(SSD_NGROUPS):
            src = FN_WIDTH + seg * SSD_INNER + g * half_group
            dst = U_FN + seg * SSD_PAD + g * gw
            put(dst, w_ref[:, src:src + half_group])
            put(dst + half_group, zero)
    src = FN_WIDTH + 2 * SSD_INNER
    put(U_FN + 2 * SSD_PAD, w_ref[:, src:src + 2 * LANES])
    src += 2 * LANES
    put(U_FN + 2 * SSD_PAD + 2 * LANES,
        jnp.dot(w_ref[:, src:src + LANES].astype(BF16), pdt_ref[...], preferred_element_type=F32))
    src += 2 * SSD_HEADS
    put(U_FN + U_SSD, w_ref[:, src:src + NA_WIDTH] * (HEAD_DIM ** -0.5))
    put(U_FN + U_SSD + NA_WIDTH, w_ref[:, src + NA_WIDTH:src + 3 * NA_WIDTH])


def _w_in_relayout(w_in):
    pdt = np.zeros((LANES, LANES), np.float32)
    for d in range(2):
        for h in range(SSD_HEADS):
            pdt[SSD_HEADS * d + h, SLOTS * d + _slot_of_head(h)] = 1.0
    return pl.pallas_call(
        _w_in_relayout_kernel,
        grid=(DEPTH, D_MODEL // W_IN_TR),
        in_specs=[pl.BlockSpec((None, W_IN_TR, w_in.shape[-1]), lambda l, i: (l, i, 0)),
                  pl.BlockSpec((LANES, LANES), lambda l, i: (0, 0))],
        out_specs=pl.BlockSpec((None, W_IN_TR, U_TOTAL), lambda l, i: (l, i, 0)),
        out_shape=jax.ShapeDtypeStruct((DEPTH, D_MODEL, U_TOTAL), BF16),
        compiler_params=_cparams(("parallel", "parallel")),
        name="w_in_relayout",
    )(w_in, jnp.asarray(pdt, dtype=BF16))


def _mixer_params(w_in, w_out, ssd_conv_w, ssd_conv_b, ssd_dt_bias, ssd_a_log, ssd_d, ssd_norm):
    w_in_pad = _w_in_relayout(w_in)
    a = -jnp.exp(ssd_a_log.astype(F32))
    a_x = jnp.repeat(_pad_dt_lanes(a)[:, :2 * SLOTS].reshape(DEPTH, 2, SLOTS), HEAD_DIM, axis=-1)
    ssd = {
        "conv_w": jnp.concatenate([_pad_heads(ssd_conv_w[..., :SSD_INNER]), ssd_conv_w[..., SSD_INNER:]], axis=-1),
        "conv_b": jnp.concatenate([_pad_heads(ssd_conv_b[..., :SSD_INNER]),
                                   ssd_conv_b[..., SSD_INNER:]], axis=-1)[:, None, :],
        "dt_bias": _pad_dt_lanes(ssd_dt_bias)[:, None, :],
        "a_row": _pad_dt_lanes(a)[:, None, :],
        "a_x": a_x.reshape(DEPTH, 2, 1, SSD_PAD),
        "d_row": _pad_heads(jnp.repeat(ssd_d, HEAD_DIM, axis=-1))[:, None, :],
        "norm_w": _pad_heads(ssd_norm)[:, None, :],
    }
    return {
        "w_in": w_in_pad,
        "w_out_fn": w_out[:, :FN_WIDTH].astype(BF16),
        "w_out_ssd": _pad_heads(w_out[:, FN_WIDTH:FN_WIDTH + SSD_INNER], axis=1).astype(BF16),
        "w_out_att": w_out[:, FN_WIDTH + SSD_INNER:].astype(BF16),
        "ssd": ssd,
    }


def _pick_tile(rows, want):
    t = min(rows, want)
    while rows % t:
        t //= 2
    return t


def kernel(x_prompt, x_sample, c, state_ssd_fwd, state_ssd_bwd, cache_attn_k, cache_attn_v, c_ctx, mod_w, mod_b, norm_pre, norm_post, ffn_w13, ffn_w2, w_in, w_out, ssd_conv_w, ssd_conv_b, ssd_dt_bias, ssd_a_log, ssd_d, ssd_norm, na_rpb):
    nbp, lp, _ = x_prompt.shape
    nbs, ls, _ = x_sample.shape
    n_ctx, n_lat = nbp * lp, nbs * ls
    assert n_ctx == n_lat and n_ctx % ls == 0 and nbs + 1 <= 8
    rpg = ls
    xs = (x_prompt.reshape(n_ctx, D_MODEL), x_sample.reshape(n_lat, D_MODEL))

    cvec = jnp.zeros((8, D_MODEL), F32).at[0].set(c_ctx).at[1:1 + nbs].set(c)
    mods = _mods(cvec, mod_w, mod_b).reshape(DEPTH, 8, N_MOD, D_MODEL)
    w13 = ffn_w13.astype(BF16)
    w2 = ffn_w2.astype(BF16)
    p = _mixer_params(w_in, w_out, ssd_conv_w, ssd_conv_b, ssd_dt_bias, ssd_a_log, ssd_d, ssd_norm)
    to_feature_major = lambda a: jnp.transpose(a, (0, 1, 3, 4, 2)).reshape(a.shape[0], DEPTH, NA_WIDTH, a.shape[2])
    from_feature_major = lambda a: jnp.transpose(
        a.reshape(a.shape[0], DEPTH, NA_HEADS, HEAD_DIM, a.shape[3]), (0, 1, 4, 2, 3))
    cache_k = to_feature_major(cache_attn_k)
    cache_v = to_feature_major(cache_attn_v)
    s0 = (state_ssd_fwd.astype(F32), state_ssd_bwd.astype(F32))
    na_bias = _na_bias_inputs(na_rpb, ls // GRID_W)
    tm_ffn = _pick_tile(rpg, 512)
    tm_proj = _pick_tile(rpg, 512)

    new_states = "create"
    new_kv = None
    for l in range(DEPTH):
        gm = jnp.concatenate([jnp.broadcast_to(mods[l, 0], (n_ctx // rpg, N_MOD, D_MODEL)),
                              mods[l, 1:1 + nbs]], axis=0)
        xs = _ffn(xs, gm, 0, norm_pre[l, 0], norm_post[l, 0], w13, w2, l, 0, rpg, tm_ffn)
        u_fn, u_ssd, u_qkv, *new_kv = _inproj(xs, gm[:, 3:6], norm_pre[l, 1], p["w_in"], l, rpg,
                                              tm_proj, lp, caches=new_kv)

        y_ssd_c, *new_states = _ssd(u_ssd, 0, nbp, lp, p["ssd"], l, False, None, states=new_states)
        mix_ctx = (_fourier(u_fn, 0, nbp, lp), y_ssd_c, _ctx_attn(u_qkv, nbp, lp))

        y_ssd_l, _, _ = _ssd(u_ssd, n_ctx, nbs, ls, p["ssd"], l, True, s0)
        mix_lat = (_fourier(u_fn, n_ctx, nbs, ls), y_ssd_l,
                   _na_attn(u_qkv, n_ctx, nbs, ls, cache_k, cache_v, l, na_bias))

        mix = (mix_ctx, mix_lat, (p["w_out_fn"], p["w_out_ssd"], p["w_out_att"]), norm_post[l, 1])
        xs = _ffn(xs, gm, 6, norm_pre[l, 2], norm_post[l, 2], w13, w2, l, 1, rpg, tm_ffn, mix=mix)

    return (xs[0].reshape(nbp, lp, D_MODEL), xs[1].reshape(nbs, ls, D_MODEL),
            new_states[0], new_states[1],
            from_feature_major(new_kv[0]), from_feature_major(new_kv[1]))
```

```python
import functools
import math

import numpy as np
import jax
import jax.numpy as jnp
from jax import lax
from jax.experimental import pallas as pl
from jax.experimental.pallas import tpu as pltpu

F32 = jnp.float32
BF16 = jnp.bfloat16

D_MODEL = 1024
DEPTH = 2
GRID_W = 64
FF_HIDDEN = 2816
N_MOD = 9
HEAD_DIM = 64
FN_WIDTH = 256
FN_GROUPS = 4
SSD_INNER = 384
SSD_HEADS = 6
SSD_STATE = 64
SSD_NGROUPS = 2
SSD_CONV = 5
SSD_CHUNK = 128
SSD_CONV_DIM = 640
SSD_IN = 1420
NA_WIDTH = 384
NA_HEADS = 6
NA_WIN_ROWS = 8
NA_WIN_COLS = 16
ROPE_BASE = 10000.0
EPS = 1e-6

LANES = 128
VMEM_LIMIT = 56 * 1024 * 1024

SLOTS = 8
SSD_PAD = SLOTS * HEAD_DIM
U_FN = FN_WIDTH
U_SSD = 2 * SSD_PAD + 2 * LANES + LANES
U_QKV = 3 * NA_WIDTH
U_TOTAL = U_FN + U_SSD + U_QKV
NEG_BIG = -1e30


def _slot_of_head(h):
    return 4 * (h // 3) + (h % 3)


def _cparams(sem):
    return pltpu.CompilerParams(dimension_semantics=sem, vmem_limit_bytes=VMEM_LIMIT)


def _rms(x):
    return x * lax.rsqrt(jnp.mean(x * x, axis=-1, keepdims=True) + EPS)


def _silu(x):
    return x * jax.nn.sigmoid(x)


def _bdot(a, b):
    return jnp.dot(a.astype(BF16), b.astype(BF16), preferred_element_type=F32)


def _bdot_nt(a, b):
    return lax.dot_general(a.astype(BF16), b.astype(BF16), (((1,), (1,)), ((), ())),
                           preferred_element_type=F32)


MOD_TN = 1152


def _mods_kernel(c_ref, w_ref, b_ref, o_ref):
    s = _silu(c_ref[...])
    o_ref[0] = _bdot(s, w_ref[0]) + b_ref[0]


def _mods(cvec, mod_w, mod_b):
    ncol = N_MOD * D_MODEL
    return pl.pallas_call(
        _mods_kernel,
        grid=(DEPTH, ncol // MOD_TN),
        in_specs=[pl.BlockSpec((8, D_MODEL), lambda l, j: (0, 0)),
                  pl.BlockSpec((1, D_MODEL, MOD_TN), lambda l, j: (l, 0, j)),
                  pl.BlockSpec((1, 1, MOD_TN), lambda l, j: (l, 0, j))],
        out_specs=pl.BlockSpec((1, 8, MOD_TN), lambda l, j: (l, 0, j)),
        out_shape=jax.ShapeDtypeStruct((DEPTH, 8, ncol), F32),
        compiler_params=_cparams(("parallel", "parallel")),
        name="mods",
    )(cvec, mod_w, mod_b.reshape(DEPTH, 1, ncol))


FFN_TH = 256


def _halves(nhalf):
    first = lambda i, *_: (jnp.minimum(i, nhalf - 1), 0)
    second = lambda i, *_: (jnp.maximum(i - nhalf, 0), 0)
    return first, second


def _on_half(nhalf, fn):
    i = pl.program_id(0)
    pl.when(i < nhalf)(functools.partial(fn, 0))
    pl.when(i >= nhalf)(functools.partial(fn, 1))


def _ffn_kernel(nhalf, mrow, with_mix, *refs):
    xa_ref, xb_ref, m_ref, gpre_ref, gpost_ref, w13_ref, w2_ref = refs[:7]
    pos = 7
    if with_mix:
        mix_refs = (refs[pos:pos + 3], refs[pos + 3:pos + 6])
        wmix_refs = refs[pos + 6:pos + 9]
        gmix_ref = refs[pos + 9]
        pos += 10
    oa_ref, ob_ref, h_ref, acc_ref = refs[pos:pos + 4]
    x_refs = (xa_ref, xb_ref)
    o_refs = (oa_ref, ob_ref)
    if with_mix:
        x1_ref = refs[pos + 4]

    def body(half):
        x = x_refs[half][...]
        if with_mix:
            y = None
            for y_ref, w_ref in zip(mix_refs[half], wmix_refs):
                d = jnp.dot(y_ref[...], w_ref[...], preferred_element_type=F32)
                y = d if y is None else y + d
            x = x + _rms(y) * (gmix_ref[...] * m_ref[0, mrow - 1:mrow, :])
            x1_ref[...] = x
        shift = m_ref[0, mrow:mrow + 1, :]
        scale = m_ref[0, mrow + 1:mrow + 2, :]
        h_ref[...] = (_rms(x) * (gpre_ref[...] * (1.0 + scale)) + shift).astype(BF16)

        h = h_ref[...]
        for j in range(FF_HIDDEN // FFN_TH):
            cols = slice(j * FFN_TH, (j + 1) * FFN_TH)
            g = jnp.dot(h, w13_ref[:, cols], preferred_element_type=F32)
            u = jnp.dot(h, w13_ref[:, FF_HIDDEN + j * FFN_TH:FF_HIDDEN + (j + 1) * FFN_TH],
                        preferred_element_type=F32)
            a = (_silu(g) * u).astype(BF16)
            part = jnp.dot(a, w2_ref[cols, :], preferred_element_type=F32)
            if j == 0:
                acc_ref[...] = part
            else:
                acc_ref[...] += part

        gate = m_ref[0, mrow + 2:mrow + 3, :]
        y = _rms(acc_ref[...]) * (gpost_ref[...] * (0.5 * gate))
        x = x1_ref[...] if with_mix else x_refs[half][...]
        o_refs[half][...] = x + y

    _on_half(nhalf, body)


def _ffn(xs, gmods, mrow, g_pre, g_post, w13, w2, layer, sub, rows_per_group, tm, mix=None):
    nh = xs[0].shape[0]
    nhalf = nh // tm
    tpg = rows_per_group // tm
    first, second = _halves(nhalf)
    resident = pl.Buffered(1)
    row = lambda a: a.reshape(1, D_MODEL)
    args = [xs[0], xs[1], gmods, row(g_pre), row(g_post), w13, w2]
    in_specs = [pl.BlockSpec((tm, D_MODEL), first),
                pl.BlockSpec((tm, D_MODEL), second),
                pl.BlockSpec((1, N_MOD, D_MODEL), lambda i: (i // tpg, 0, 0)),
                pl.BlockSpec((1, D_MODEL), lambda i: (0, 0)),
                pl.BlockSpec((1, D_MODEL), lambda i: (0, 0)),
                pl.BlockSpec((None, None, D_MODEL, 2 * FF_HIDDEN), lambda i: (layer, sub, 0, 0),
                             pipeline_mode=resident),
                pl.BlockSpec((None, None, FF_HIDDEN, D_MODEL), lambda i: (layer, sub, 0, 0),
                             pipeline_mode=resident)]
    scratch = [pltpu.VMEM((tm, D_MODEL), BF16), pltpu.VMEM((tm, D_MODEL), F32)]
    if mix is not None:
        mix_ctx, mix_lat, w_mix, g_post_mix = mix
        args += list(mix_ctx) + list(mix_lat) + list(w_mix) + [row(g_post_mix)]
        in_specs += ([pl.BlockSpec((tm, a.shape[1]), first) for a in mix_ctx]
                     + [pl.BlockSpec((tm, a.shape[1]), second) for a in mix_lat]
                     + [pl.BlockSpec((None,) + w.shape[1:], lambda i: (layer, 0, 0), pipeline_mode=resident)
                        for w in w_mix]
                     + [pl.BlockSpec((1, D_MODEL), lambda i: (0, 0))])
        scratch.append(pltpu.VMEM((tm, D_MODEL), F32))
    return pl.pallas_call(
        functools.partial(_ffn_kernel, nhalf, mrow, mix is not None),
        grid=(2 * nhalf,),
        in_specs=in_specs,
        out_specs=[pl.BlockSpec((tm, D_MODEL), first), pl.BlockSpec((tm, D_MODEL), second)],
        out_shape=[jax.ShapeDtypeStruct((nh, D_MODEL), F32)] * 2,
        scratch_shapes=scratch,
        compiler_params=_cparams(("arbitrary",)),
        name="ffn",
    )(*args)


def _inproj_kernel(nhalf, layer, first_layer, *refs):
    xa_ref, xb_ref, m_ref, gpre_ref, w_ref = refs[:5]
    ofn_ref, ossd_ref, oqkv_ref, ok_ref, ov_ref = refs[-5:]
    x_refs = (xa_ref, xb_ref)

    def body(half):
        shift = m_ref[0, 0:1, :]
        scale = m_ref[0, 1:2, :]
        h = (_rms(x_refs[half][...]) * (gpre_ref[...] * (1.0 + scale)) + shift).astype(BF16)
        u = lax.dot_general(h, w_ref[...], (((1,), (1,)), ((), ())), preferred_element_type=F32)
        ofn_ref[...] = u[:, :U_FN].astype(ofn_ref.dtype)
        ossd_ref[...] = u[:, U_FN:U_FN + U_SSD]
        oqkv_ref[...] = u[:, U_FN + U_SSD:].astype(oqkv_ref.dtype)
        if half == 0:
            nseq, seq_len = ok_ref.shape[0], ok_ref.shape[-1]
            for s in range(nseq):
                rows = slice(s * seq_len, (s + 1) * seq_len)
                k_t = u[rows, U_FN + U_SSD + NA_WIDTH:U_FN + U_SSD + 2 * NA_WIDTH].T
                v_t = u[rows, U_FN + U_SSD + 2 * NA_WIDTH:].T
                if first_layer:
                    for l2 in range(ok_ref.shape[1]):
                        ok_ref[s, l2] = k_t if l2 == layer else jnp.zeros_like(k_t)
                        ov_ref[s, l2] = v_t if l2 == layer else jnp.zeros_like(v_t)
                else:
                    ok_ref[s] = k_t
                    ov_ref[s] = v_t

    _on_half(nhalf, body)


def _inproj(xs, gmods, g_pre, w_in_pad, layer, rows_per_group, tm, ctx_len, caches=None):
    nh = xs[0].shape[0]
    n = 2 * nh
    nhalf = nh // tm
    tpg = rows_per_group // tm
    first, second = _halves(nhalf)
    assert tm % ctx_len == 0
    seq_per_tile = tm // ctx_len
    cache_shape = (nh // ctx_len, DEPTH, NA_WIDTH, ctx_len)
    args = [xs[0], xs[1], gmods, g_pre.reshape(1, D_MODEL), w_in_pad]
    in_specs = [pl.BlockSpec((tm, D_MODEL), first),
                pl.BlockSpec((tm, D_MODEL), second),
                pl.BlockSpec((1, 3, D_MODEL), lambda i: (i // tpg, 0, 0)),
                pl.BlockSpec((1, D_MODEL), lambda i: (0, 0)),
                pl.BlockSpec((None, U_TOTAL, D_MODEL), lambda i: (layer, 0, 0))]
    if caches is None:
        cache_spec = pl.BlockSpec((seq_per_tile, DEPTH, NA_WIDTH, ctx_len),
                                  lambda i: (jnp.minimum(i, nhalf - 1), 0, 0, 0))
        aliases = {}
    else:
        cache_spec = pl.BlockSpec((seq_per_tile, None, NA_WIDTH, ctx_len),
                                  lambda i: (jnp.minimum(i, nhalf - 1), layer, 0, 0))
        aliases = {len(args): 3, len(args) + 1: 4}
        args += list(caches)
        in_specs += [pl.BlockSpec(memory_space=pl.ANY)] * 2
    return pl.pallas_call(
        functools.partial(_inproj_kernel, nhalf, layer, caches is None),
        grid=(n // tm,),
        in_specs=in_specs,
        out_specs=[pl.BlockSpec((tm, U_FN), lambda i: (i, 0)),
                   pl.BlockSpec((tm, U_SSD), lambda i: (i, 0)),
                   pl.BlockSpec((tm, U_QKV), lambda i: (i, 0)),
                   cache_spec, cache_spec],
        out_shape=[jax.ShapeDtypeStruct((n, U_FN), BF16),
                   jax.ShapeDtypeStruct((n, U_SSD), F32),
                   jax.ShapeDtypeStruct((n, U_QKV), BF16),
                   jax.ShapeDtypeStruct(cache_shape, F32),
                   jax.ShapeDtypeStruct(cache_shape, F32)],
        input_output_aliases=aliases,
        compiler_params=_cparams(("arbitrary",)),
        name="inproj",
    )(*args)


def _dft_tables(L):
    k = np.arange(L, dtype=np.int64)
    ang = 2.0 * np.pi * ((k[:, None] * k[None, :]) % L).astype(np.float64) / L
    sc = 1.0 / math.sqrt(L * HEAD_DIM)
    cl = (np.cos(ang) * sc).astype(np.float32)
    sl = (-np.sin(ang) * sc).astype(np.float32)
    m = np.arange(HEAD_DIM, dtype=np.int64)
    a64 = 2.0 * np.pi * ((m[:, None] * m[None, :]) % HEAD_DIM).astype(np.float64) / HEAD_DIM
    eye = np.eye(FN_GROUPS)
    w1 = np.concatenate([np.kron(eye, np.cos(a64)), np.kron(eye, np.sin(a64))], axis=1).astype(np.float32)
    return cl, sl, w1


def _fourier_kernel(nseq, L, tl, u_ref, w1_ref, cl_ref, sl_ref, o_ref, ab_ref):
    rt = pl.program_id(0)
    g = pl.program_id(1)

    @pl.when(rt == 0)
    def _():
        ab_ref[g] = _bdot(u_ref[...], w1_ref[...]).astype(BF16)

    for s in range(nseq):
        ab = ab_ref[g, s * L:(s + 1) * L, :]
        y = (jnp.dot(cl_ref[...], ab[:, :FN_WIDTH], preferred_element_type=F32)
             + jnp.dot(sl_ref[...], ab[:, FN_WIDTH:], preferred_element_type=F32))
        o_ref[s * tl:(s + 1) * tl, :] = y.astype(o_ref.dtype)


def _fourier(u_fn, row0, nb, L):
    cl, sl, w1 = _dft_tables(L)
    cl = jnp.asarray(cl).astype(BF16)
    sl = jnp.asarray(sl).astype(BF16)
    w1 = jnp.asarray(w1).astype(BF16)
    tl = min(L, 1024)
    nrt = L // tl
    nseq = max(1, min(nb, 2048 // L)) if nrt == 1 else 1
    while nb % nseq:
        nseq -= 1
    ng = nb // nseq
    blk0 = row0 // (nseq * L)
    assert row0 % (nseq * L) == 0
    return pl.pallas_call(
        functools.partial(_fourier_kernel, nseq, L, tl),
        grid=(nrt, ng),
        in_specs=[pl.BlockSpec((nseq * L, FN_WIDTH), lambda rt, g: (blk0 + jnp.where(rt == 0, g, ng - 1), 0)),
                  pl.BlockSpec((FN_WIDTH, 2 * FN_WIDTH), lambda rt, g: (0, 0)),
                  pl.BlockSpec((tl, L), lambda rt, g: (rt, 0)),
                  pl.BlockSpec((tl, L), lambda rt, g: (rt, 0))],
        out_specs=pl.BlockSpec((nseq * tl, FN_WIDTH), lambda rt, g: (g * nrt + rt, 0)),
        out_shape=jax.ShapeDtypeStruct((nb * L, FN_WIDTH), BF16),
        scratch_shapes=[pltpu.VMEM((ng, nseq * L, 2 * FN_WIDTH), BF16)],
        compiler_params=_cparams(("arbitrary", "arbitrary")),
        name="fourier",
    )(u_fn, w1, cl, sl)


def _head_masks():
    lane = lax.broadcasted_iota(jnp.int32, (1, LANES), 1)
    return lane < HEAD_DIM


def _ctx_attn_kernel(nseq, L, q_ref, k_ref, v_ref, o_ref):
    low = _head_masks()
    for t in range(nseq):
        rows = slice(t * L, (t + 1) * L)
        for p in range(NA_HEADS // 2):
            lanes = slice(p * LANES, (p + 1) * LANES)
            q = q_ref[rows, lanes]
            k = k_ref[rows, lanes].astype(BF16)
            v = v_ref[rows, lanes].astype(BF16)
            q2 = jnp.concatenate([jnp.where(low, q, 0.0), jnp.where(low, 0.0, q)], axis=0)
            s = _bdot_nt(q2, k)
            m = jnp.max(s, axis=-1, keepdims=True)
            e = jnp.exp(s - m)
            inv = 1.0 / jnp.sum(e, axis=-1, keepdims=True)
            o2 = jnp.dot(e.astype(BF16), v, preferred_element_type=F32) * inv
            o_ref[rows, lanes] = jnp.where(low, o2[:L], o2[L:]).astype(o_ref.dtype)


def _ctx_attn(u_qkv, nb, L):
    nseq = 4 if nb % 4 == 0 else 1
    return pl.pallas_call(
        functools.partial(_ctx_attn_kernel, nseq, L),
        grid=(nb // nseq,),
        in_specs=[pl.BlockSpec((nseq * L, NA_WIDTH), lambda b: (b, 0)),
                  pl.BlockSpec((nseq * L, NA_WIDTH), lambda b: (b, 1)),
                  pl.BlockSpec((nseq * L, NA_WIDTH), lambda b: (b, 2))],
        out_specs=pl.BlockSpec((nseq * L, NA_WIDTH), lambda b: (b, 0)),
        out_shape=jax.ShapeDtypeStruct((nb * L, NA_WIDTH), BF16),
        compiler_params=_cparams(("parallel",)),
        name="ctx_attn",
    )(u_qkv, u_qkv, u_qkv)


NA_QR = 8
NA_QC = 16
NA_KR = 16
NA_KC = 32
NA_TQ = NA_QR * NA_QC
NA_TK = NA_KR * NA_KC


def _na_tile_geometry(rows):
    wr = min(NA_WIN_ROWS, rows)
    n_rb = rows // NA_QR
    n_cb = GRID_W // NA_QC
    rb = [0, 1, n_rb - 1]
    cb = [0, 1, n_cb - 1]
    dr = np.zeros((3, NA_QR, NA_KR), np.int64)
    vr = np.zeros((3, NA_QR, NA_KR), bool)
    for ci, i in enumerate(rb):
        kr0 = int(np.clip(NA_QR * i - NA_WIN_ROWS // 2, 0, rows - NA_KR))
        for rr in range(NA_QR):
            r = NA_QR * i + rr
            rs = int(np.clip(r - wr // 2, 0, rows - wr))
            for kk in range(NA_KR):
                kr = kr0 + kk
                vr[ci, rr, kk] = rs <= kr < rs + wr
                dr[ci, rr, kk] = np.clip(kr - r + NA_WIN_ROWS - 1, 0, 2 * NA_WIN_ROWS - 2)
    dc = np.zeros((3, NA_QC, NA_KC), np.int64)
    vc = np.zeros((3, NA_QC, NA_KC), bool)
    for ci, j in enumerate(cb):
        kc0 = int(np.clip(NA_QC * j - NA_WIN_COLS // 2, 0, GRID_W - NA_KC))
        for cq in range(NA_QC):
            c = NA_QC * j + cq
            cs = int(np.clip(c - NA_WIN_COLS // 2, 0, GRID_W - NA_WIN_COLS))
            for ck in range(NA_KC):
                kc = kc0 + ck
                vc[ci, cq, ck] = cs <= kc < cs + NA_WIN_COLS
                dc[ci, cq, ck] = np.clip(kc - c + NA_WIN_COLS - 1, 0, 2 * NA_WIN_COLS - 2)
    return dr, vr, dc, vc


def _na_bias_inputs(rpb, rows):
    dr, vr, dc, vc = _na_tile_geometry(rows)
    n_dc = 2 * NA_WIN_COLS - 1
    oh_c = (dc[..., None] == np.arange(n_dc)).astype(np.float32)
    oh_j = np.broadcast_to(oh_c[:, :, None], (3, NA_QC, NA_KR, NA_KC, n_dc)).reshape(3, NA_QC, NA_TK, n_dc)
    vq = jnp.einsum("lhab,ycjb->lhaycj", rpb.astype(F32), jnp.asarray(oh_j), precision=lax.Precision.HIGHEST)
    valid_c = np.broadcast_to(vc[:, :, None], (3, NA_QC, NA_KR, NA_KC)).reshape(3, NA_QC, NA_TK)
    vq = jnp.where(jnp.asarray(valid_c)[None, None, None], vq, NEG_BIG)
    drt = np.where(vr, dr, -1)[..., None]
    drt = np.broadcast_to(drt, (3, NA_QR, NA_KR, NA_KC)).reshape(3, NA_QR, NA_TK).astype(np.int32)
    offsets = [[sorted(set(dr[x, rr][vr[x, rr]].tolist())) for rr in range(NA_QR)] for x in range(3)]
    return vq, jnp.asarray(drt), offsets


def _na_kernel(rows, offsets, q_ref, k16_ref, v16_ref, kc_ref, vc_ref, vq_ref, drt_ref, o_ref,
               bias_ref, k_ref, v_ref):
    k_ref[...] = k16_ref[...].astype(F32)
    v_ref[...] = v16_ref[...].astype(F32)
    n_rb = rows // NA_QR
    n_cb = GRID_W // NA_QC
    low = _head_masks()
    kctx_t = kc_ref[0, 0].astype(BF16)
    vctx_t = vc_ref[0, 0].astype(BF16)

    @pl.when(pl.program_id(1) == 0)
    def _():
        def build(t, carry):
            hh = t // 3
            cc = t % 3
            for rc in range(3):
                for rr in range(NA_QR):
                    drrow = drt_ref[rc, rr:rr + 1, :]
                    acc = jnp.full((NA_QC, NA_TK), NEG_BIG, F32)
                    for a in offsets[rc][rr]:
                        acc = jnp.where(drrow == a, vq_ref[hh, a, cc], acc)
                    row0 = pl.multiple_of(hh * NA_TQ + rr * NA_QC, NA_QC)
                    bias_ref[rc, cc, pl.ds(row0, NA_QC), :] = acc
            return carry

        lax.fori_loop(0, 6, build, 0)

    def row_block(i, carry):
        kr0 = jnp.clip(NA_QR * i - NA_WIN_ROWS // 2, 0, rows - NA_KR)
        rcfg = jnp.where(i > 0, 1, 0) + jnp.where(i == n_rb - 1, 1, 0)
        for j in range(n_cb):
            kc0 = int(np.clip(NA_QC * j - NA_WIN_COLS // 2, 0, GRID_W - NA_KC))
            ccfg = 0 if j == 0 else (2 if j == n_cb - 1 else 1)
            q_parts = []
            for rr in range(NA_QR):
                start = pl.multiple_of((NA_QR * i + rr) * GRID_W + NA_QC * j, NA_QC)
                q_parts.append(q_ref[pl.ds(start, NA_QC), :])
            q = jnp.concatenate(q_parts, axis=0)
            k_parts, v_parts = [], []
            for kk in range(NA_KR):
                start = pl.multiple_of((kr0 + kk) * GRID_W + kc0, 8)
                k_parts.append(k_ref[pl.ds(start, NA_KC), :])
                v_parts.append(v_ref[pl.ds(start, NA_KC), :])
            k = jnp.concatenate(k_parts, axis=0).astype(BF16)
            v = jnp.concatenate(v_parts, axis=0).astype(BF16)
            q2 = jnp.concatenate([jnp.where(low, q, 0.0), jnp.where(low, 0.0, q)], axis=0).astype(BF16)
            s_loc = _bdot_nt(q2, k) + bias_ref[rcfg, ccfg]
            s_ctx = jnp.dot(q2, kctx_t, preferred_element_type=F32)
            m = jnp.maximum(jnp.max(s_loc, axis=-1, keepdims=True),
                            jnp.max(s_ctx, axis=-1, keepdims=True))
            p_loc = jnp.exp(s_loc - m)
            p_ctx = jnp.exp(s_ctx - m)
            l = jnp.sum(p_loc, axis=-1, keepdims=True) + jnp.sum(p_ctx, axis=-1, keepdims=True)
            o2 = (jnp.dot(p_loc.astype(BF16), v, preferred_element_type=F32)
                  + _bdot_nt(p_ctx, vctx_t)) * (1.0 / l)
            o = jnp.where(low, o2[:NA_TQ], o2[NA_TQ:])
            for rr in range(NA_QR):
                start = pl.multiple_of((NA_QR * i + rr) * GRID_W + NA_QC * j, NA_QC)
                o_ref[pl.ds(start, NA_QC), :] = o[rr * NA_QC:(rr + 1) * NA_QC, :].astype(o_ref.dtype)
        return carry

    lax.fori_loop(0, n_rb, row_block, 0, unroll=2)


def _na_attn(u_qkv, row0, nb, L, cache_k, cache_v, layer, bias_inputs):
    npair = NA_HEADS // 2
    rows = L // GRID_W
    blk0 = row0 // L
    lc = cache_k.shape[3]
    vq, drt, offsets = bias_inputs
    return pl.pallas_call(
        functools.partial(_na_kernel, rows, offsets),
        grid=(npair, nb),
        in_specs=[pl.BlockSpec((L, LANES), lambda p, b: (blk0 + b, p)),
                  pl.BlockSpec((L, LANES), lambda p, b: (blk0 + b, npair + p)),
                  pl.BlockSpec((L, LANES), lambda p, b: (blk0 + b, 2 * npair + p)),
                  pl.BlockSpec((1, 1, LANES, lc), lambda p, b: (b, layer, p, 0)),
                  pl.BlockSpec((1, 1, LANES, lc), lambda p, b: (b, layer, p, 0)),
                  pl.BlockSpec((None, 2) + vq.shape[2:], lambda p, b: (layer, p, 0, 0, 0, 0)),
                  pl.BlockSpec(drt.shape, lambda p, b: (0, 0, 0))],
        out_specs=pl.BlockSpec((L, LANES), lambda p, b: (b, p)),
        out_shape=jax.ShapeDtypeStruct((nb * L, NA_WIDTH), BF16),
        scratch_shapes=[pltpu.VMEM((3, 3, 2 * NA_TQ, NA_TK), F32),
                        pltpu.VMEM((L, LANES), F32), pltpu.VMEM((L, LANES), F32)],
        compiler_params=_cparams(("arbitrary", "arbitrary")),
        name="na_attn",
    )(u_qkv, u_qkv, u_qkv, cache_k, cache_v, vq, drt)


def _ssd_constants():
    tril = np.tril(np.ones((SSD_CHUNK, SSD_CHUNK), np.float32))
    expand = np.zeros((2, LANES, SSD_PAD), np.float32)
    colb = np.zeros((2, LANES, SSD_HEADS * LANES), np.float32)
    for d in range(2):
        for h in range(SSD_HEADS):
            s = _slot_of_head(h)
            expand[d, SLOTS * d + s, HEAD_DIM * s:HEAD_DIM * (s + 1)] = 1.0
            colb[d, SLOTS * d + s, LANES * h:LANES * (h + 1)] = 1.0
    two = lambda m: np.concatenate([m, m], axis=-2)
    return np.concatenate([tril, tril], axis=1), two(expand), two(colb)


def _rope_tables(L):
    t = np.arange(L)
    rows = (t // GRID_W).astype(np.float64)
    cols = (t % GRID_W).astype(np.float64)
    quarter = SSD_STATE // 4
    inv = ROPE_BASE ** (-np.arange(quarter, dtype=np.float64) / quarter)
    n = np.arange(SSD_STATE)
    pos = np.where(n[None, :] < SSD_STATE // 2, rows[:, None], cols[:, None])
    ang = pos * inv[n % quarter][None, :]
    first = (n % (SSD_STATE // 2)) < quarter
    cos = np.cos(ang)
    sin = np.where(first[None, :], -np.sin(ang), np.sin(ang))
    tile = lambda a: np.concatenate([a] * SSD_NGROUPS, axis=1).astype(np.float32)
    return tile(cos), tile(sin)


def _split2(x):
    hi = x.astype(BF16)
    lo = (x - hi.astype(F32)).astype(BF16)
    return hi, lo


def _ssd_kernel(L, use_rope, use_init, layer, create_states, *refs):
    (u_ref, convw_ref, convb_ref, dtb_ref, a_ref, ax_ref, d_ref, nw_ref,
     tril_ref, exp_ref, colb_ref) = refs[:11]
    pos = 11
    if use_rope:
        cos_ref, sin_ref = refs[pos:pos + 2]
        pos += 2
    if use_init:
        s0_refs = refs[pos:pos + 2]
        pos += 2
    y_ref = refs[-7]
    sfin_refs = refs[-6:-4]
    act_ref, dt_ref, s_ref, yacc_ref = refs[-4:]

    nc = L // SSD_CHUNK
    C = SSD_CHUNK
    GW = SSD_PAD // SSD_NGROUPS
    XC = SSD_PAD
    CONV_W = SSD_PAD + 2 * LANES
    DTC = XC + CONV_W
    HALO = 8

    lane = lax.broadcasted_iota(jnp.int32, (1, LANES), 1)
    first_q = (lane % (SSD_STATE // 2)) < (SSD_STATE // 4)

    def prep(c, carry):
        r0 = pl.multiple_of(c * C, C)
        main = u_ref[pl.ds(r0, C), XC:XC + CONV_W]
        pstart = pl.multiple_of(jnp.maximum(r0 - HALO, 0), HALO)
        nstart = pl.multiple_of(jnp.minimum(r0 + C, L - HALO), HALO)
        prev = u_ref[pl.ds(pstart, HALO), XC:XC + CONV_W] * jnp.where(c > 0, 1.0, 0.0)
        nxt = u_ref[pl.ds(nstart, HALO), XC:XC + CONV_W] * jnp.where(c < nc - 1, 1.0, 0.0)
        win = jnp.concatenate([prev, main, nxt], axis=0)
        acc = jnp.zeros((C, CONV_W), F32) + convb_ref[...]
        for k in range(SSD_CONV):
            off = HALO + k - SSD_CONV // 2
            acc = acc + win[off:off + C, :] * convw_ref[k:k + 1, :]
        act = _silu(acc)
        act_ref[pl.ds(r0, C), 0:SSD_PAD] = act[:, 0:SSD_PAD]
        for t in range(2):
            bc = act[:, SSD_PAD + t * LANES:SSD_PAD + (t + 1) * LANES]
            if use_rope:
                partner = jnp.where(first_q, pltpu.roll(bc, LANES - SSD_STATE // 4, 1),
                                    pltpu.roll(bc, SSD_STATE // 4, 1))
                bc = bc * cos_ref[pl.ds(r0, C), :] + partner * sin_ref[pl.ds(r0, C), :]
            act_ref[pl.ds(r0, C), SSD_PAD + t * LANES:SSD_PAD + (t + 1) * LANES] = bc
        raw = u_ref[pl.ds(r0, C), DTC:DTC + LANES] + dtb_ref[...]
        dt_ref[pl.ds(r0, C), :] = jnp.maximum(raw, 0.0) + jnp.log1p(jnp.exp(-jnp.abs(raw)))
        yacc_ref[pl.ds(r0, C), :] = jnp.zeros((C, SSD_PAD), F32)
        return carry

    lax.fori_loop(0, nc, prep, 0)

    if use_init:
        zero = jnp.zeros((HEAD_DIM, SSD_STATE), F32)
        for d in range(2):
            for g in range(SSD_NGROUPS):
                rows_pn = []
                for hh in range(3):
                    blk = s0_refs[d][0, 0, 3 * g + hh]
                    rows_pn.append(jnp.concatenate([blk, zero] if g == 0 else [zero, blk], axis=1))
                rows_pn.append(jnp.zeros((HEAD_DIM, LANES), F32))
                s_ref[d, g] = jnp.concatenate(rows_pn, axis=0).T
    else:
        s_ref[...] = jnp.zeros_like(s_ref)

    a_row = a_ref[...]
    li = lax.broadcasted_iota(jnp.int32, (C, C), 0)
    si = lax.broadcasted_iota(jnp.int32, (C, C), 1)
    causal = [li >= si, si >= li]
    low64 = lane < HEAD_DIM
    grp_mask = [low64, jnp.logical_not(low64)]

    def one_direction(d, c):
        r0 = pl.multiple_of(c * C, C)
        x = act_ref[pl.ds(r0, C), 0:SSD_PAD]
        bmat = act_ref[pl.ds(r0, C), SSD_PAD:SSD_PAD + LANES]
        cmat = act_ref[pl.ds(r0, C), SSD_PAD + LANES:SSD_PAD + 2 * LANES]
        dt = dt_ref[pl.ds(r0, C), :]
        dta = dt * a_row
        hi, lo = _split2(dta)
        cs = jnp.dot(tril_ref[...], jnp.concatenate([hi, lo], axis=0),
                     preferred_element_type=F32)
        q = cs if d == 0 else cs - dta
        expand = lambda v: jnp.dot(jnp.concatenate(_split2(v), axis=1), exp_ref[d],
                                   preferred_element_type=F32)
        dt_x = expand(dt)
        cs_x = expand(cs)
        end_x = cs_x[C - 1:C, :]
        if d == 0:
            off_scale = jnp.exp(cs_x)
            w_state = jnp.exp(end_x - cs_x)
        else:
            e_x = cs_x - dt_x * ax_ref[d]
            off_scale = jnp.exp(end_x - e_x)
            w_state = jnp.exp(e_x)
        chunk_decay = jnp.exp(end_x)
        xdt = x * dt_x
        xdt_b = xdt.astype(BF16)
        rhs_state = (xdt * w_state).astype(BF16)
        qcol = jnp.dot(jnp.concatenate(_split2(q), axis=1), colb_ref[d],
                       preferred_element_type=F32)
        q_t = q.T
        b_t = bmat.T.astype(BF16)
        b_b = bmat.astype(BF16)
        cms = [jnp.where(grp_mask[g], cmat, 0.0).astype(BF16) for g in range(SSD_NGROUPS)]
        gmats = lax.dot_general(jnp.concatenate(cms, axis=0), b_b, (((1,), (1,)), ((), ())),
                                preferred_element_type=F32)
        st_all = jnp.dot(b_t, rhs_state, preferred_element_type=F32)
        for g in range(SSD_NGROUPS):
            gmat = gmats[g * C:(g + 1) * C]
            s_old = s_ref[d, g]
            y_off = (jnp.dot(cms[g], s_old.astype(BF16), preferred_element_type=F32)
                     * off_scale[:, g * GW:(g + 1) * GW])
            ms = []
            for hh in range(3):
                h = 3 * g + hh
                slot = 4 * g + hh
                row = q_t[SLOTS * d + slot:SLOTS * d + slot + 1, :]
                col = qcol[:, h * LANES:(h + 1) * LANES]
                seg = (col - row) if d == 0 else (row - col)
                ms.append((gmat * jnp.exp(jnp.where(causal[d], seg, NEG_BIG))).astype(BF16))
            r01 = jnp.dot(jnp.concatenate(ms[:2], axis=0), xdt_b[:, 2 * g * LANES:(2 * g + 1) * LANES],
                          preferred_element_type=F32)
            r2 = jnp.dot(ms[2], xdt_b[:, (2 * g + 1) * LANES:(2 * g + 2) * LANES],
                         preferred_element_type=F32)
            y_g = jnp.concatenate([jnp.where(low64, r01[:C], r01[C:]), r2], axis=1) + y_off
            yacc_ref[pl.ds(r0, C), g * GW:(g + 1) * GW] += y_g
            s_ref[d, g] = (s_old * chunk_decay[:, g * GW:(g + 1) * GW]
                           + st_all[:, g * GW:(g + 1) * GW])

    def scan(i, carry):
        one_direction(0, i)
        one_direction(1, nc - 1 - i)
        return carry

    lax.fori_loop(0, nc, scan, 0, unroll=2)
    for d in range(2):
        out = sfin_refs[d].at[0, layer] if create_states else sfin_refs[d].at[0]
        if create_states:
            for l2 in range(sfin_refs[d].shape[1]):
                if l2 != layer:
                    sfin_refs[d][0, l2] = jnp.zeros(sfin_refs[d].shape[2:], F32)
        for g in range(SSD_NGROUPS):
            s_t = s_ref[d, g].T
            for hh in range(3):
                out[3 * g + hh] = s_t[hh * HEAD_DIM:(hh + 1) * HEAD_DIM, g * SSD_STATE:(g + 1) * SSD_STATE]

    def finish(c, carry):
        r0 = pl.multiple_of(c * C, C)
        y = yacc_ref[pl.ds(r0, C), :] + act_ref[pl.ds(r0, C), 0:SSD_PAD] * d_ref[...]
        y = y * _silu(u_ref[pl.ds(r0, C), 0:SSD_PAD])
        for g in range(SSD_NGROUPS):
            yg = y[:, g * GW:(g + 1) * GW]
            ms = jnp.sum(yg * yg, axis=-1, keepdims=True) * (1.0 / (SSD_INNER // SSD_NGROUPS))
            yn = yg * lax.rsqrt(ms + EPS) * nw_ref[:, g * GW:(g + 1) * GW]
            y_ref[pl.ds(r0, C), g * GW:(g + 1) * GW] = yn.astype(y_ref.dtype)
        return carry

    lax.fori_loop(0, nc, finish, 0)


def _ssd(u_ssd, row0, nb, L, prm, layer, use_rope, s0, states="layer"):
    blk0 = row0 // L
    tril2, exp2, colb2 = _ssd_constants()
    consts = [jnp.asarray(tril2, dtype=BF16), jnp.asarray(exp2, dtype=BF16), jnp.asarray(colb2, dtype=BF16)]
    full = lambda a: pl.BlockSpec(a.shape, lambda b, _n=a.ndim: (0,) * _n)
    of_layer = lambda a: pl.BlockSpec((None,) + a.shape[1:], lambda b, _n=a.ndim: (layer,) + (0,) * (_n - 1))
    per_layer = [prm[k] for k in ("conv_w", "conv_b", "dt_bias", "a_row", "a_x", "d_row", "norm_w")]
    args = [u_ssd] + per_layer + consts
    in_specs = ([pl.BlockSpec((L, U_SSD), lambda b: (blk0 + b, 0))] + [of_layer(a) for a in per_layer]
                + [full(a) for a in consts])
    if use_rope:
        cos, sin = _rope_tables(L)
        tabs = [jnp.asarray(cos), jnp.asarray(sin)]
        args += tabs
        in_specs += [full(a) for a in tabs]
    state_block = (SSD_HEADS, HEAD_DIM, SSD_STATE)
    if s0 is not None:
        args += list(s0)
        in_specs += [pl.BlockSpec((1, 1) + state_block, lambda b: (b, layer, 0, 0, 0))] * 2
    sshape = (2, SSD_NGROUPS, LANES, SSD_PAD // SSD_NGROUPS)
    aliases = {}
    if isinstance(states, str) and states == "layer":
        state_spec = pl.BlockSpec((1,) + state_block, lambda b: (b, 0, 0, 0))
        state_shape = (nb,) + state_block
    elif isinstance(states, str):
        state_spec = pl.BlockSpec((1, DEPTH) + state_block, lambda b: (b, 0, 0, 0, 0))
        state_shape = (nb, DEPTH) + state_block
    else:
        state_spec = pl.BlockSpec((1, None) + state_block, lambda b: (b, layer, 0, 0, 0))
        state_shape = (nb, DEPTH) + state_block
        aliases = {len(args): 1, len(args) + 1: 2}
        args += list(states)
        in_specs += [pl.BlockSpec(memory_space=pl.ANY)] * 2
    return pl.pallas_call(
        functools.partial(_ssd_kernel, L, use_rope, s0 is not None, layer,
                          isinstance(states, str) and states == "create"),
        grid=(nb,),
        in_specs=in_specs,
        out_specs=[pl.BlockSpec((L, SSD_PAD), lambda b: (b, 0)), state_spec, state_spec],
        out_shape=[jax.ShapeDtypeStruct((nb * L, SSD_PAD), BF16)]
                  + [jax.ShapeDtypeStruct(state_shape, F32)] * 2,
        input_output_aliases=aliases,
        scratch_shapes=[pltpu.VMEM((L, SSD_PAD + 2 * LANES), F32),
                        pltpu.VMEM((L, LANES), F32),
                        pltpu.VMEM(sshape, F32),
                        pltpu.VMEM((L, SSD_PAD), F32)],
        compiler_params=_cparams(("parallel",)),
        name="ssd",
    )(*args)


def _pad_heads(a, axis=-1):
    a = jnp.moveaxis(a, axis, -1)
    lead = a.shape[:-1]
    a = a.reshape(lead + (SSD_NGROUPS, 3, HEAD_DIM))
    a = jnp.pad(a, [(0, 0)] * len(lead) + [(0, 0), (0, 1), (0, 0)])
    return jnp.moveaxis(a.reshape(lead + (SSD_PAD,)), -1, axis)


def _pad_dt_lanes(a):
    lead = a.shape[:-2]
    a = a.reshape(lead + (2, SSD_NGROUPS, 3))
    a = jnp.pad(a, [(0, 0)] * len(lead) + [(0, 0), (0, 0), (0, 1)]).reshape(lead + (2 * SLOTS,))
    return jnp.pad(a, [(0, 0)] * len(lead) + [(0, LANES - 2 * SLOTS)])


W_IN_TC = 256


def _w_in_relayout_kernel(w_ref, pdt_ref, o_ref):
    half_group = SSD_INNER // SSD_NGROUPS
    gw = SSD_PAD // SSD_NGROUPS
    zero = jnp.zeros((gw - half_group, W_IN_TC), o_ref.dtype)

    def put(dst, v):
        o_ref[dst:dst + v.shape[0], :] = v.astype(o_ref.dtype)

    put(0, w_ref[0:FN_WIDTH, :])
    for seg in range(2):
        for g in range(SSD_NGROUPS):
            src = FN_WIDTH + seg * SSD_INNER + g * half_group
            dst = U_FN + seg * SSD_PAD + g * gw
            put(dst, w_ref[src:src + half_group, :])
            put(dst + half_group, zero)
    src = FN_WIDTH + 2 * SSD_INNER
    put(U_FN + 2 * SSD_PAD, w_ref[src:src + 2 * LANES, :])
    src += 2 * LANES
    put(U_FN + 2 * SSD_PAD + 2 * LANES,
        jnp.dot(pdt_ref[...], w_ref[src:src + LANES, :].astype(BF16), preferred_element_type=F32))
    src += 2 * SSD_HEADS
    put(U_FN + U_SSD, w_ref[src:src + NA_WIDTH, :] * (HEAD_DIM ** -0.5))
    put(U_FN + U_SSD + NA_WIDTH, w_ref[src + NA_WIDTH:src + 3 * NA_WIDTH, :])


def _w_in_relayout(w_in):
    pdt = np.zeros((LANES, LANES), np.float32)
    for d in range(2):
        for h in range(SSD_HEADS):
            pdt[SLOTS * d + _slot_of_head(h), SSD_HEADS * d + h] = 1.0
    w_t = jnp.transpose(w_in, (0, 2, 1))
    return pl.pallas_call(
        _w_in_relayout_kernel,
        grid=(DEPTH, D_MODEL // W_IN_TC),
        in_specs=[pl.BlockSpec((None, w_in.shape[-1], W_IN_TC), lambda l, i: (l, 0, i)),
                  pl.BlockSpec((LANES, LANES), lambda l, i: (0, 0))],
        out_specs=pl.BlockSpec((None, U_TOTAL, W_IN_TC), lambda l, i: (l, 0, i)),
        out_shape=jax.ShapeDtypeStruct((DEPTH, U_TOTAL, D_MODEL), BF16),
        compiler_params=_cparams(("parallel", "parallel")),
        name="w_in_relayout",
    )(w_t, jnp.asarray(pdt, dtype=BF16))


def _mixer_params(w_in, w_out, ssd_conv_w, ssd_conv_b, ssd_dt_bias, ssd_a_log, ssd_d, ssd_norm):
    w_in_pad = _w_in_relayout(w_in)
    a = -jnp.exp(ssd_a_log.astype(F32))
    a_x = jnp.repeat(_pad_dt_lanes(a)[:, :2 * SLOTS].reshape(DEPTH, 2, SLOTS), HEAD_DIM, axis=-1)
    ssd = {
        "conv_w": jnp.concatenate([_pad_heads(ssd_conv_w[..., :SSD_INNER]), ssd_conv_w[..., SSD_INNER:]], axis=-1),
        "conv_b": jnp.concatenate([_pad_heads(ssd_conv_b[..., :SSD_INNER]),
                                   ssd_conv_b[..., SSD_INNER:]], axis=-1)[:, None, :],
        "dt_bias": _pad_dt_lanes(ssd_dt_bias)[:, None, :],
        "a_row": _pad_dt_lanes(a)[:, None, :],
        "a_x": a_x.reshape(DEPTH, 2, 1, SSD_PAD),
        "d_row": _pad_heads(jnp.repeat(ssd_d, HEAD_DIM, axis=-1))[:, None, :],
        "norm_w": _pad_heads(ssd_norm)[:, None, :],
    }
    return {
        "w_in": w_in_pad,
        "w_out_fn": w_out[:, :FN_WIDTH].astype(BF16),
        "w_out_ssd": _pad_heads(w_out[:, FN_WIDTH:FN_WIDTH + SSD_INNER], axis=1).astype(BF16),
        "w_out_att": w_out[:, FN_WIDTH + SSD_INNER:].astype(BF16),
        "ssd": ssd,
    }


def _pick_tile(rows, want):
    t = min(rows, want)
    while rows % t:
        t //= 2
    return t


def kernel(x_prompt, x_sample, c, state_ssd_fwd, state_ssd_bwd, cache_attn_k, cache_attn_v, c_ctx, mod_w, mod_b, norm_pre, norm_post, ffn_w13, ffn_w2, w_in, w_out, ssd_conv_w, ssd_conv_b, ssd_dt_bias, ssd_a_log, ssd_d, ssd_norm, na_rpb):
    nbp, lp, _ = x_prompt.shape
    nbs, ls, _ = x_sample.shape
    n_ctx, n_lat = nbp * lp, nbs * ls
    assert n_ctx == n_lat and n_ctx % ls == 0 and nbs + 1 <= 8
    rpg = ls
    xs = (x_prompt.reshape(n_ctx, D_MODEL), x_sample.reshape(n_lat, D_MODEL))

    cvec = jnp.zeros((8, D_MODEL), F32).at[0].set(c_ctx).at[1:1 + nbs].set(c)
    mods = _mods(cvec, mod_w, mod_b).reshape(DEPTH, 8, N_MOD, D_MODEL)
    w13 = ffn_w13.astype(BF16)
    w2 = ffn_w2.astype(BF16)
    p = _mixer_params(w_in, w_out, ssd_conv_w, ssd_conv_b, ssd_dt_bias, ssd_a_log, ssd_d, ssd_norm)
    to_feature_major = lambda a: jnp.transpose(a, (0, 1, 3, 4, 2)).reshape(a.shape[0], DEPTH, NA_WIDTH, a.shape[2])
    from_feature_major = lambda a: jnp.transpose(
        a.reshape(a.shape[0], DEPTH, NA_HEADS, HEAD_DIM, a.shape[3]), (0, 1, 4, 2, 3))
    cache_k = to_feature_major(cache_attn_k)
    cache_v = to_feature_major(cache_attn_v)
    s0 = (state_ssd_fwd.astype(F32), state_ssd_bwd.astype(F32))
    na_bias = _na_bias_inputs(na_rpb, ls // GRID_W)
    tm_ffn = _pick_tile(rpg, 512)
    tm_proj = _pick_tile(rpg, 512)

    new_states = "create"
    new_kv = None
    for l in range(DEPTH):
        gm = jnp.concatenate([jnp.broadcast_to(mods[l, 0], (n_ctx // rpg, N_MOD, D_MODEL)),
                              mods[l, 1:1 + nbs]], axis=0)
        xs = _ffn(xs, gm, 0, norm_pre[l, 0], norm_post[l, 0], w13, w2, l, 0, rpg, tm_ffn)
        u_fn, u_ssd, u_qkv, *new_kv = _inproj(xs, gm[:, 3:6], norm_pre[l, 1], p["w_in"], l, rpg,
                                              tm_proj, lp, caches=new_kv)

        y_ssd_c, *new_states = _ssd(u_ssd, 0, nbp, lp, p["ssd"], l, False, None, states=new_states)
        mix_ctx = (_fourier(u_fn, 0, nbp, lp), y_ssd_c, _ctx_attn(u_qkv, nbp, lp))

        y_ssd_l, _, _ = _ssd(u_ssd, n_ctx, nbs, ls, p["ssd"], l, True, s0)
        mix_lat = (_fourier(u_fn, n_ctx, nbs, ls), y_ssd_l,
                   _na_attn(u_qkv, n_ctx, nbs, ls, cache_k, cache_v, l, na_bias))

        mix = (mix_ctx, mix_lat, (p["w_out_fn"], p["w_out_ssd"], p["w_out_att"]), norm_post[l, 1])
        xs = _ffn(xs, gm, 6, norm_pre[l, 2], norm_post[l, 2], w13, w2, l, 1, rpg, tm_ffn, mix=mix)

    return (xs[0].reshape(nbp, lp, D_MODEL), xs[1].reshape(nbs, ls, D_MODEL),
            new_states[0], new_states[1],
            from_feature_major(new_kv[0]), from_feature_major(new_kv[1]))
```

```python
import functools
import math

import numpy as np
import jax
import jax.numpy as jnp
from jax import lax
from jax.experimental import pallas as pl
from jax.experimental.pallas import tpu as pltpu

F32 = jnp.float32
BF16 = jnp.bfloat16

D_MODEL = 1024
DEPTH = 2
GRID_W = 64
FF_HIDDEN = 2816
N_MOD = 9
HEAD_DIM = 64
FN_WIDTH = 256
FN_GROUPS = 4
SSD_INNER = 384
SSD_HEADS = 6
SSD_STATE = 64
SSD_NGROUPS = 2
SSD_CONV = 5
SSD_CHUNK = 128
SSD_CONV_DIM = 640
SSD_IN = 1420
NA_WIDTH = 384
NA_HEADS = 6
NA_WIN_ROWS = 8
NA_WIN_COLS = 16
ROPE_BASE = 10000.0
EPS = 1e-6

LANES = 128
VMEM_LIMIT = 56 * 1024 * 1024

SLOTS = 8
SSD_PAD = SLOTS * HEAD_DIM
U_FN = FN_WIDTH
U_SSD = 2 * SSD_PAD + 2 * LANES + LANES
U_QKV = 3 * NA_WIDTH
U_TOTAL = U_FN + U_SSD + U_QKV
NEG_BIG = -1e30


def _slot_of_head(h):
    return 4 * (h // 3) + (h % 3)


def _cparams(sem):
    return pltpu.CompilerParams(dimension_semantics=sem, vmem_limit_bytes=VMEM_LIMIT)


def _rms(x):
    return x * lax.rsqrt(jnp.mean(x * x, axis=-1, keepdims=True) + EPS)


def _silu(x):
    return x * jax.nn.sigmoid(x)


def _bdot(a, b):
    return jnp.dot(a.astype(BF16), b.astype(BF16), preferred_element_type=F32)


def _bdot_nt(a, b):
    return lax.dot_general(a.astype(BF16), b.astype(BF16), (((1,), (1,)), ((), ())),
                           preferred_element_type=F32)


MOD_TN = 1152


def _mods_kernel(c_ref, w_ref, b_ref, o_ref):
    s = _silu(c_ref[...])
    o_ref[0] = _bdot(s, w_ref[0]) + b_ref[0]


def _mods(cvec, mod_w, mod_b):
    ncol = N_MOD * D_MODEL
    return pl.pallas_call(
        _mods_kernel,
        grid=(DEPTH, ncol // MOD_TN),
        in_specs=[pl.BlockSpec((8, D_MODEL), lambda l, j: (0, 0)),
                  pl.BlockSpec((1, D_MODEL, MOD_TN), lambda l, j: (l, 0, j)),
                  pl.BlockSpec((1, 1, MOD_TN), lambda l, j: (l, 0, j))],
        out_specs=pl.BlockSpec((1, 8, MOD_TN), lambda l, j: (l, 0, j)),
        out_shape=jax.ShapeDtypeStruct((DEPTH, 8, ncol), F32),
        compiler_params=_cparams(("parallel", "parallel")),
        name="mods",
    )(cvec, mod_w, mod_b.reshape(DEPTH, 1, ncol))


FFN_TH = 256


def _halves(nhalf):
    first = lambda i, *_: (jnp.minimum(i, nhalf - 1), 0)
    second = lambda i, *_: (jnp.maximum(i - nhalf, 0), 0)
    return first, second


def _on_half(nhalf, fn):
    i = pl.program_id(0)
    pl.when(i < nhalf)(functools.partial(fn, 0))
    pl.when(i >= nhalf)(functools.partial(fn, 1))


def _ffn_kernel(nhalf, mrow, with_mix, *refs):
    xa_ref, xb_ref, m_ref, gpre_ref, gpost_ref, w13_ref, w2_ref = refs[:7]
    pos = 7
    if with_mix:
        mix_refs = (refs[pos:pos + 3], refs[pos + 3:pos + 6])
        wmix_refs = refs[pos + 6:pos + 9]
        gmix_ref = refs[pos + 9]
        pos += 10
    oa_ref, ob_ref, h_ref, acc_ref = refs[pos:pos + 4]
    x_refs = (xa_ref, xb_ref)
    o_refs = (oa_ref, ob_ref)
    if with_mix:
        x1_ref = refs[pos + 4]

    def body(half):
        x = x_refs[half][...]
        if with_mix:
            y = None
            for y_ref, w_ref in zip(mix_refs[half], wmix_refs):
                d = jnp.dot(y_ref[...], w_ref[...], preferred_element_type=F32)
                y = d if y is None else y + d
            x = x + _rms(y) * (gmix_ref[...] * m_ref[0, mrow - 1:mrow, :])
            x1_ref[...] = x
        shift = m_ref[0, mrow:mrow + 1, :]
        scale = m_ref[0, mrow + 1:mrow + 2, :]
        h_ref[...] = (_rms(x) * (gpre_ref[...] * (1.0 + scale)) + shift).astype(BF16)

        h = h_ref[...]
        for j in range(FF_HIDDEN // FFN_TH):
            cols = slice(j * FFN_TH, (j + 1) * FFN_TH)
            g = jnp.dot(h, w13_ref[:, cols], preferred_element_type=F32)
            u = jnp.dot(h, w13_ref[:, FF_HIDDEN + j * FFN_TH:FF_HIDDEN + (j + 1) * FFN_TH],
                        preferred_element_type=F32)
            a = (_silu(g) * u).astype(BF16)
            part = jnp.dot(a, w2_ref[cols, :], preferred_element_type=F32)
            if j == 0:
                acc_ref[...] = part
            else:
                acc_ref[...] += part

        gate = m_ref[0, mrow + 2:mrow + 3, :]
        y = _rms(acc_ref[...]) * (gpost_ref[...] * (0.5 * gate))
        x = x1_ref[...] if with_mix else x_refs[half][...]
        o_refs[half][...] = x + y

    _on_half(nhalf, body)


def _ffn(xs, gmods, mrow, g_pre, g_post, w13, w2, layer, sub, rows_per_group, tm, mix=None):
    nh = xs[0].shape[0]
    nhalf = nh // tm
    tpg = rows_per_group // tm
    first, second = _halves(nhalf)
    resident = pl.Buffered(1)
    row = lambda a: a.reshape(1, D_MODEL)
    args = [xs[0], xs[1], gmods, row(g_pre), row(g_post), w13, w2]
    in_specs = [pl.BlockSpec((tm, D_MODEL), first),
                pl.BlockSpec((tm, D_MODEL), second),
                pl.BlockSpec((1, N_MOD, D_MODEL), lambda i: (i // tpg, 0, 0)),
                pl.BlockSpec((1, D_MODEL), lambda i: (0, 0)),
                pl.BlockSpec((1, D_MODEL), lambda i: (0, 0)),
                pl.BlockSpec((None, None, D_MODEL, 2 * FF_HIDDEN), lambda i: (layer, sub, 0, 0),
                             pipeline_mode=resident),
                pl.BlockSpec((None, None, FF_HIDDEN, D_MODEL), lambda i: (layer, sub, 0, 0),
                             pipeline_mode=resident)]
    scratch = [pltpu.VMEM((tm, D_MODEL), BF16), pltpu.VMEM((tm, D_MODEL), F32)]
    if mix is not None:
        mix_ctx, mix_lat, w_mix, g_post_mix = mix
        args += list(mix_ctx) + list(mix_lat) + list(w_mix) + [row(g_post_mix)]
        in_specs += ([pl.BlockSpec((tm, a.shape[1]), first) for a in mix_ctx]
                     + [pl.BlockSpec((tm, a.shape[1]), second) for a in mix_lat]
                     + [pl.BlockSpec((None,) + w.shape[1:], lambda i: (layer, 0, 0), pipeline_mode=resident)
                        for w in w_mix]
                     + [pl.BlockSpec((1, D_MODEL), lambda i: (0, 0))])
        scratch.append(pltpu.VMEM((tm, D_MODEL), F32))
    return pl.pallas_call(
        functools.partial(_ffn_kernel, nhalf, mrow, mix is not None),
        grid=(2 * nhalf,),
        in_specs=in_specs,
        out_specs=[pl.BlockSpec((tm, D_MODEL), first), pl.BlockSpec((tm, D_MODEL), second)],
        out_shape=[jax.ShapeDtypeStruct((nh, D_MODEL), F32)] * 2,
        scratch_shapes=scratch,
        compiler_params=_cparams(("arbitrary",)),
        name="ffn",
    )(*args)


def _inproj_kernel(nhalf, layer, first_layer, *refs):
    xa_ref, xb_ref, m_ref, gpre_ref, w_ref = refs[:5]
    ofn_ref, ossd_ref, oqkv_ref, ok_ref, ov_ref = refs[-5:]
    x_refs = (xa_ref, xb_ref)

    def body(half):
        shift = m_ref[0, 0:1, :]
        scale = m_ref[0, 1:2, :]
        h = (_rms(x_refs[half][...]) * (gpre_ref[...] * (1.0 + scale)) + shift).astype(BF16)
        u = lax.dot_general(h, w_ref[...], (((1,), (1,)), ((), ())), preferred_element_type=F32)
        ofn_ref[...] = u[:, :U_FN].astype(ofn_ref.dtype)
        ossd_ref[...] = u[:, U_FN:U_FN + U_SSD]
        oqkv_ref[...] = u[:, U_FN + U_SSD:].astype(oqkv_ref.dtype)
        if half == 0:
            nseq, seq_len = ok_ref.shape[0], ok_ref.shape[-1]
            for s in range(nseq):
                rows = slice(s * seq_len, (s + 1) * seq_len)
                k_t = u[rows, U_FN + U_SSD + NA_WIDTH:U_FN + U_SSD + 2 * NA_WIDTH].T
                v_t = u[rows, U_FN + U_SSD + 2 * NA_WIDTH:].T
                if first_layer:
                    for l2 in range(ok_ref.shape[1]):
                        ok_ref[s, l2] = k_t if l2 == layer else jnp.zeros_like(k_t)
                        ov_ref[s, l2] = v_t if l2 == layer else jnp.zeros_like(v_t)
                else:
                    ok_ref[s] = k_t
                    ov_ref[s] = v_t

    _on_half(nhalf, body)


def _inproj(xs, gmods, g_pre, w_in_pad, layer, rows_per_group, tm, ctx_len, caches=None):
    nh = xs[0].shape[0]
    n = 2 * nh
    nhalf = nh // tm
    tpg = rows_per_group // tm
    first, second = _halves(nhalf)
    assert tm % ctx_len == 0
    seq_per_tile = tm // ctx_len
    cache_shape = (nh // ctx_len, DEPTH, NA_WIDTH, ctx_len)
    args = [xs[0], xs[1], gmods, g_pre.reshape(1, D_MODEL), w_in_pad]
    in_specs = [pl.BlockSpec((tm, D_MODEL), first),
                pl.BlockSpec((tm, D_MODEL), second),
                pl.BlockSpec((1, 3, D_MODEL), lambda i: (i // tpg, 0, 0)),
                pl.BlockSpec((1, D_MODEL), lambda i: (0, 0)),
                pl.BlockSpec((None, U_TOTAL, D_MODEL), lambda i: (layer, 0, 0))]
    if caches is None:
        cache_spec = pl.BlockSpec((seq_per_tile, DEPTH, NA_WIDTH, ctx_len),
                                  lambda i: (jnp.minimum(i, nhalf - 1), 0, 0, 0))
        aliases = {}
    else:
        cache_spec = pl.BlockSpec((seq_per_tile, None, NA_WIDTH, ctx_len),
                                  lambda i: (jnp.minimum(i, nhalf - 1), layer, 0, 0))
        aliases = {len(args): 3, len(args) + 1: 4}
        args += list(caches)
        in_specs += [pl.BlockSpec(memory_space=pl.ANY)] * 2
    return pl.pallas_call(
        functools.partial(_inproj_kernel, nhalf, layer, caches is None),
        grid=(n // tm,),
        in_specs=in_specs,
        out_specs=[pl.BlockSpec((tm, U_FN), lambda i: (i, 0)),
                   pl.BlockSpec((tm, U_SSD), lambda i: (i, 0)),
                   pl.BlockSpec((tm, U_QKV), lambda i: (i, 0)),
                   cache_spec, cache_spec],
        out_shape=[jax.ShapeDtypeStruct((n, U_FN), BF16),
                   jax.ShapeDtypeStruct((n, U_SSD), F32),
                   jax.ShapeDtypeStruct((n, U_QKV), BF16),
                   jax.ShapeDtypeStruct(cache_shape, F32),
                   jax.ShapeDtypeStruct(cache_shape, F32)],
        input_output_aliases=aliases,
        compiler_params=_cparams(("arbitrary",)),
        name="inproj",
    )(*args)


def _dft_tables(L):
    k = np.arange(L, dtype=np.int64)
    ang = 2.0 * np.pi * ((k[:, None] * k[None, :]) % L).astype(np.float64) / L
    sc = 1.0 / math.sqrt(L * HEAD_DIM)
    cl = (np.cos(ang) * sc).astype(np.float32)
    sl = (-np.sin(ang) * sc).astype(np.float32)
    m = np.arange(HEAD_DIM, dtype=np.int64)
    a64 = 2.0 * np.pi * ((m[:, None] * m[None, :]) % HEAD_DIM).astype(np.float64) / HEAD_DIM
    eye = np.eye(FN_GROUPS)
    w1 = np.concatenate([np.kron(eye, np.cos(a64)), np.kron(eye, np.sin(a64))], axis=1).astype(np.float32)
    return cl, sl, w1


def _fourier_kernel(nseq, L, tl, u_ref, w1_ref, cl_ref, sl_ref, o_ref, ab_ref):
    rt = pl.program_id(0)
    g = pl.program_id(1)

    @pl.when(rt == 0)
    def _():
        ab_ref[g] = _bdot(u_ref[...], w1_ref[...]).astype(BF16)

    for s in range(nseq):
        ab = ab_ref[g, s * L:(s + 1) * L, :]
        y = (jnp.dot(cl_ref[...], ab[:, :FN_WIDTH], preferred_element_type=F32)
             + jnp.dot(sl_ref[...], ab[:, FN_WIDTH:], preferred_element_type=F32))
        o_ref[s * tl:(s + 1) * tl, :] = y.astype(o_ref.dtype)


def _fourier(u_fn, row0, nb, L):
    cl, sl, w1 = _dft_tables(L)
    cl = jnp.asarray(cl).astype(BF16)
    sl = jnp.asarray(sl).astype(BF16)
    w1 = jnp.asarray(w1).astype(BF16)
    tl = min(L, 1024)
    nrt = L // tl
    nseq = max(1, min(nb, 2048 // L)) if nrt == 1 else 1
    while nb % nseq:
        nseq -= 1
    ng = nb // nseq
    blk0 = row0 // (nseq * L)
    assert row0 % (nseq * L) == 0
    return pl.pallas_call(
        functools.partial(_fourier_kernel, nseq, L, tl),
        grid=(nrt, ng),
        in_specs=[pl.BlockSpec((nseq * L, FN_WIDTH), lambda rt, g: (blk0 + jnp.where(rt == 0, g, ng - 1), 0)),
                  pl.BlockSpec((FN_WIDTH, 2 * FN_WIDTH), lambda rt, g: (0, 0)),
                  pl.BlockSpec((tl, L), lambda rt, g: (rt, 0)),
                  pl.BlockSpec((tl, L), lambda rt, g: (rt, 0))],
        out_specs=pl.BlockSpec((nseq * tl, FN_WIDTH), lambda rt, g: (g * nrt + rt, 0)),
        out_shape=jax.ShapeDtypeStruct((nb * L, FN_WIDTH), BF16),
        scratch_shapes=[pltpu.VMEM((ng, nseq * L, 2 * FN_WIDTH), BF16)],
        compiler_params=_cparams(("arbitrary", "arbitrary")),
        name="fourier",
    )(u_fn, w1, cl, sl)


def _head_masks():
    lane = lax.broadcasted_iota(jnp.int32, (1, LANES), 1)
    return lane < HEAD_DIM


def _ctx_attn_kernel(nseq, L, q_ref, k_ref, v_ref, o_ref):
    low = _head_masks()
    for t in range(nseq):
        rows = slice(t * L, (t + 1) * L)
        for p in range(NA_HEADS // 2):
            lanes = slice(p * LANES, (p + 1) * LANES)
            q = q_ref[rows, lanes]
            k = k_ref[rows, lanes].astype(BF16)
            v = v_ref[rows, lanes].astype(BF16)
            q2 = jnp.concatenate([jnp.where(low, q, 0.0), jnp.where(low, 0.0, q)], axis=0)
            s = _bdot_nt(q2, k)
            m = jnp.max(s, axis=-1, keepdims=True)
            e = jnp.exp(s - m)
            inv = 1.0 / jnp.sum(e, axis=-1, keepdims=True)
            o2 = jnp.dot(e.astype(BF16), v, preferred_element_type=F32) * inv
            o_ref[rows, lanes] = jnp.where(low, o2[:L], o2[L:]).astype(o_ref.dtype)


def _ctx_attn(u_qkv, nb, L):
    nseq = 4 if nb % 4 == 0 else 1
    return pl.pallas_call(
        functools.partial(_ctx_attn_kernel, nseq, L),
        grid=(nb // nseq,),
        in_specs=[pl.BlockSpec((nseq * L, NA_WIDTH), lambda b: (b, 0)),
                  pl.BlockSpec((nseq * L, NA_WIDTH), lambda b: (b, 1)),
                  pl.BlockSpec((nseq * L, NA_WIDTH), lambda b: (b, 2))],
        out_specs=pl.BlockSpec((nseq * L, NA_WIDTH), lambda b: (b, 0)),
        out_shape=jax.ShapeDtypeStruct((nb * L, NA_WIDTH), BF16),
        compiler_params=_cparams(("parallel",)),
        name="ctx_attn",
    )(u_qkv, u_qkv, u_qkv)


NA_QR = 8
NA_QC = 16
NA_KR = 16
NA_KC = 32
NA_TQ = NA_QR * NA_QC
NA_TK = NA_KR * NA_KC
NA_KCHUNK = 512


def _na_tile_geometry(rows):
    wr = min(NA_WIN_ROWS, rows)
    n_rb = rows // NA_QR
    n_cb = GRID_W // NA_QC
    rb = [0, 1, n_rb - 1]
    cb = [0, 1, n_cb - 1]
    dr = np.zeros((3, NA_QR, NA_KR), np.int64)
    vr = np.zeros((3, NA_QR, NA_KR), bool)
    for ci, i in enumerate(rb):
        kr0 = int(np.clip(NA_QR * i - NA_WIN_ROWS // 2, 0, rows - NA_KR))
        for rr in range(NA_QR):
            r = NA_QR * i + rr
            rs = int(np.clip(r - wr // 2, 0, rows - wr))
            for kk in range(NA_KR):
                kr = kr0 + kk
                vr[ci, rr, kk] = rs <= kr < rs + wr
                dr[ci, rr, kk] = np.clip(kr - r + NA_WIN_ROWS - 1, 0, 2 * NA_WIN_ROWS - 2)
    dc = np.zeros((3, NA_QC, NA_KC), np.int64)
    vc = np.zeros((3, NA_QC, NA_KC), bool)
    for ci, j in enumerate(cb):
        kc0 = int(np.clip(NA_QC * j - NA_WIN_COLS // 2, 0, GRID_W - NA_KC))
        for cq in range(NA_QC):
            c = NA_QC * j + cq
            cs = int(np.clip(c - NA_WIN_COLS // 2, 0, GRID_W - NA_WIN_COLS))
            for ck in range(NA_KC):
                kc = kc0 + ck
                vc[ci, cq, ck] = cs <= kc < cs + NA_WIN_COLS
                dc[ci, cq, ck] = np.clip(kc - c + NA_WIN_COLS - 1, 0, 2 * NA_WIN_COLS - 2)
    return dr, vr, dc, vc


def _na_bias_inputs(rpb, rows):
    dr, vr, dc, vc = _na_tile_geometry(rows)
    n_dc = 2 * NA_WIN_COLS - 1
    oh_c = (dc[..., None] == np.arange(n_dc)).astype(np.float32)
    oh_j = np.broadcast_to(oh_c[:, :, None], (3, NA_QC, NA_KR, NA_KC, n_dc)).reshape(3, NA_QC, NA_TK, n_dc)
    vq = jnp.einsum("lhab,ycjb->lhaycj", rpb.astype(F32), jnp.asarray(oh_j), precision=lax.Precision.HIGHEST)
    valid_c = np.broadcast_to(vc[:, :, None], (3, NA_QC, NA_KR, NA_KC)).reshape(3, NA_QC, NA_TK)
    vq = jnp.where(jnp.asarray(valid_c)[None, None, None], vq, NEG_BIG)
    drt = np.where(vr, dr, -1)[..., None]
    drt = np.broadcast_to(drt, (3, NA_QR, NA_KR, NA_KC)).reshape(3, NA_QR, NA_TK).astype(np.int32)
    offsets = [[sorted(set(dr[x, rr][vr[x, rr]].tolist())) for rr in range(NA_QR)] for x in range(3)]
    return vq, jnp.asarray(drt), offsets


def _na_kernel(rows, offsets, q_ref, k16_ref, v16_ref, kc_ref, vc_ref, vq_ref, drt_ref, o_ref,
               bias_ref, k_ref, v_ref):
    k_ref[...] = k16_ref[...].astype(F32)
    v_ref[...] = v16_ref[...].astype(F32)
    n_rb = rows // NA_QR
    n_cb = GRID_W // NA_QC
    low = _head_masks()
    kctx_t = kc_ref[0, 0].astype(BF16)
    vctx_t = vc_ref[0, 0].astype(BF16)

    @pl.when(pl.program_id(1) == 0)
    def _():
        def build(t, carry):
            hh = t // 3
            cc = t % 3
            for rc in range(3):
                for rr in range(NA_QR):
                    drrow = drt_ref[rc, rr:rr + 1, :]
                    acc = jnp.full((NA_QC, NA_TK), NEG_BIG, F32)
                    for a in offsets[rc][rr]:
                        acc = jnp.where(drrow == a, vq_ref[hh, a, cc], acc)
                    row0 = pl.multiple_of(hh * NA_TQ + rr * NA_QC, NA_QC)
                    bias_ref[rc, cc, pl.ds(row0, NA_QC), :] = acc
            return carry

        lax.fori_loop(0, 6, build, 0)

    def row_block(i, carry):
        kr0 = jnp.clip(NA_QR * i - NA_WIN_ROWS // 2, 0, rows - NA_KR)
        rcfg = jnp.where(i > 0, 1, 0) + jnp.where(i == n_rb - 1, 1, 0)
        for j in range(n_cb):
            kc0 = int(np.clip(NA_QC * j - NA_WIN_COLS // 2, 0, GRID_W - NA_KC))
            ccfg = 0 if j == 0 else (2 if j == n_cb - 1 else 1)
            q_parts = []
            for rr in range(NA_QR):
                start = pl.multiple_of((NA_QR * i + rr) * GRID_W + NA_QC * j, NA_QC)
                q_parts.append(q_ref[pl.ds(start, NA_QC), :])
            q = jnp.concatenate(q_parts, axis=0)
            k_parts, v_parts = [], []
            for kk in range(NA_KR):
                start = pl.multiple_of((kr0 + kk) * GRID_W + kc0, 8)
                k_parts.append(k_ref[pl.ds(start, NA_KC), :])
                v_parts.append(v_ref[pl.ds(start, NA_KC), :])
            k = jnp.concatenate(k_parts, axis=0).astype(BF16)
            v = jnp.concatenate(v_parts, axis=0).astype(BF16)
            q2 = jnp.concatenate([jnp.where(low, q, 0.0), jnp.where(low, 0.0, q)], axis=0).astype(BF16)
            m = l = acc = None
            chunks = ([("ctx", c) for c in range(kctx_t.shape[1] // NA_KCHUNK)]
                      + [("loc", c) for c in range(NA_TK // NA_KCHUNK)])
            for kind, c in chunks:
                cols = slice(c * NA_KCHUNK, (c + 1) * NA_KCHUNK)
                if kind == "ctx":
                    s = jnp.dot(q2, kctx_t[:, cols], preferred_element_type=F32)
                else:
                    s = _bdot_nt(q2, k[cols]) + bias_ref[rcfg, ccfg, :, cols]
                row_max = jnp.max(s, axis=-1, keepdims=True)
                m_new = row_max if m is None else jnp.maximum(m, row_max)
                p = jnp.exp(s - m_new)
                row_sum = jnp.sum(p, axis=-1, keepdims=True)
                if kind == "ctx":
                    pv = _bdot_nt(p, vctx_t[:, cols])
                else:
                    pv = jnp.dot(p.astype(BF16), v[cols], preferred_element_type=F32)
                if m is None:
                    l, acc = row_sum, pv
                else:
                    alpha = jnp.exp(m - m_new)
                    l = alpha * l + row_sum
                    acc = alpha * acc + pv
                m = m_new
            o2 = acc * (1.0 / l)
            o = jnp.where(low, o2[:NA_TQ], o2[NA_TQ:])
            for rr in range(NA_QR):
                start = pl.multiple_of((NA_QR * i + rr) * GRID_W + NA_QC * j, NA_QC)
                o_ref[pl.ds(start, NA_QC), :] = o[rr * NA_QC:(rr + 1) * NA_QC, :].astype(o_ref.dtype)
        return carry

    lax.fori_loop(0, n_rb, row_block, 0, unroll=2)


def _na_attn(u_qkv, row0, nb, L, cache_k, cache_v, layer, bias_inputs):
    npair = NA_HEADS // 2
    rows = L // GRID_W
    blk0 = row0 // L
    lc = cache_k.shape[3]
    vq, drt, offsets = bias_inputs
    return pl.pallas_call(
        functools.partial(_na_kernel, rows, offsets),
        grid=(npair, nb),
        in_specs=[pl.BlockSpec((L, LANES), lambda p, b: (blk0 + b, p)),
                  pl.BlockSpec((L, LANES), lambda p, b: (blk0 + b, npair + p)),
                  pl.BlockSpec((L, LANES), lambda p, b: (blk0 + b, 2 * npair + p)),
                  pl.BlockSpec((1, 1, LANES, lc), lambda p, b: (b, layer, p, 0)),
                  pl.BlockSpec((1, 1, LANES, lc), lambda p, b: (b, layer, p, 0)),
                  pl.BlockSpec((None, 2) + vq.shape[2:], lambda p, b: (layer, p, 0, 0, 0, 0)),
                  pl.BlockSpec(drt.shape, lambda p, b: (0, 0, 0))],
        out_specs=pl.BlockSpec((L, LANES), lambda p, b: (b, p)),
        out_shape=jax.ShapeDtypeStruct((nb * L, NA_WIDTH), BF16),
        scratch_shapes=[pltpu.VMEM((3, 3, 2 * NA_TQ, NA_TK), F32),
                        pltpu.VMEM((L, LANES), F32), pltpu.VMEM((L, LANES), F32)],
        compiler_params=_cparams(("arbitrary", "arbitrary")),
        name="na_attn",
    )(u_qkv, u_qkv, u_qkv, cache_k, cache_v, vq, drt)


def _ssd_constants():
    tril = np.tril(np.ones((SSD_CHUNK, SSD_CHUNK), np.float32))
    expand = np.zeros((2, LANES, SSD_PAD), np.float32)
    colb = np.zeros((2, LANES, SSD_HEADS * LANES), np.float32)
    for d in range(2):
        for h in range(SSD_HEADS):
            s = _slot_of_head(h)
            expand[d, SLOTS * d + s, HEAD_DIM * s:HEAD_DIM * (s + 1)] = 1.0
            colb[d, SLOTS * d + s, LANES * h:LANES * (h + 1)] = 1.0
    two = lambda m: np.concatenate([m, m], axis=-2)
    return np.concatenate([tril, tril], axis=1), two(expand), two(colb)


def _rope_tables(L):
    t = np.arange(L)
    rows = (t // GRID_W).astype(np.float64)
    cols = (t % GRID_W).astype(np.float64)
    quarter = SSD_STATE // 4
    inv = ROPE_BASE ** (-np.arange(quarter, dtype=np.float64) / quarter)
    n = np.arange(SSD_STATE)
    pos = np.where(n[None, :] < SSD_STATE // 2, rows[:, None], cols[:, None])
    ang = pos * inv[n % quarter][None, :]
    first = (n % (SSD_STATE // 2)) < quarter
    cos = np.cos(ang)
    sin = np.where(first[None, :], -np.sin(ang), np.sin(ang))
    tile = lambda a: np.concatenate([a] * SSD_NGROUPS, axis=1).astype(np.float32)
    return tile(cos), tile(sin)


def _split2(x):
    hi = x.astype(BF16)
    lo = (x - hi.astype(F32)).astype(BF16)
    return hi, lo


def _ssd_kernel(L, use_rope, use_init, layer, create_states, *refs):
    (u_ref, convw_ref, convb_ref, dtb_ref, a_ref, ax_ref, d_ref, nw_ref,
     tril_ref, exp_ref, colb_ref) = refs[:11]
    pos = 11
    if use_rope:
        cos_ref, sin_ref = refs[pos:pos + 2]
        pos += 2
    if use_init:
        s0_refs = refs[pos:pos + 2]
        pos += 2
    y_ref = refs[-7]
    sfin_refs = refs[-6:-4]
    act_ref, dt_ref, s_ref, yacc_ref = refs[-4:]

    nc = L // SSD_CHUNK
    C = SSD_CHUNK
    GW = SSD_PAD // SSD_NGROUPS
    XC = SSD_PAD
    CONV_W = SSD_PAD + 2 * LANES
    DTC = XC + CONV_W
    HALO = 8

    lane = lax.broadcasted_iota(jnp.int32, (1, LANES), 1)
    first_q = (lane % (SSD_STATE // 2)) < (SSD_STATE // 4)

    def prep(c, carry):
        r0 = pl.multiple_of(c * C, C)
        main = u_ref[pl.ds(r0, C), XC:XC + CONV_W]
        pstart = pl.multiple_of(jnp.maximum(r0 - HALO, 0), HALO)
        nstart = pl.multiple_of(jnp.minimum(r0 + C, L - HALO), HALO)
        prev = u_ref[pl.ds(pstart, HALO), XC:XC + CONV_W] * jnp.where(c > 0, 1.0, 0.0)
        nxt = u_ref[pl.ds(nstart, HALO), XC:XC + CONV_W] * jnp.where(c < nc - 1, 1.0, 0.0)
        win = jnp.concatenate([prev, main, nxt], axis=0)
        acc = jnp.zeros((C, CONV_W), F32) + convb_ref[...]
        for k in range(SSD_CONV):
            off = HALO + k - SSD_CONV // 2
            acc = acc + win[off:off + C, :] * convw_ref[k:k + 1, :]
        act = _silu(acc)
        act_ref[pl.ds(r0, C), 0:SSD_PAD] = act[:, 0:SSD_PAD]
        for t in range(2):
            bc = act[:, SSD_PAD + t * LANES:SSD_PAD + (t + 1) * LANES]
            if use_rope:
                partner = jnp.where(first_q, pltpu.roll(bc, LANES - SSD_STATE // 4, 1),
                                    pltpu.roll(bc, SSD_STATE // 4, 1))
                bc = bc * cos_ref[pl.ds(r0, C), :] + partner * sin_ref[pl.ds(r0, C), :]
            act_ref[pl.ds(r0, C), SSD_PAD + t * LANES:SSD_PAD + (t + 1) * LANES] = bc
        raw = u_ref[pl.ds(r0, C), DTC:DTC + LANES] + dtb_ref[...]
        dt_ref[pl.ds(r0, C), :] = jnp.maximum(raw, 0.0) + jnp.log1p(jnp.exp(-jnp.abs(raw)))
        yacc_ref[pl.ds(r0, C), :] = jnp.zeros((C, SSD_PAD), F32)
        return carry

    lax.fori_loop(0, nc, prep, 0)

    if use_init:
        zero = jnp.zeros((HEAD_DIM, SSD_STATE), F32)
        for d in range(2):
            for g in range(SSD_NGROUPS):
                rows_pn = []
                for hh in range(3):
                    blk = s0_refs[d][0, 0, 3 * g + hh]
                    rows_pn.append(jnp.concatenate([blk, zero] if g == 0 else [zero, blk], axis=1))
                rows_pn.append(jnp.zeros((HEAD_DIM, LANES), F32))
                s_ref[d, g] = jnp.concatenate(rows_pn, axis=0).T
    else:
        s_ref[...] = jnp.zeros_like(s_ref)

    a_row = a_ref[...]
    li = lax.broadcasted_iota(jnp.int32, (C, C), 0)
    si = lax.broadcasted_iota(jnp.int32, (C, C), 1)
    causal = [li >= si, si >= li]
    low64 = lane < HEAD_DIM
    grp_mask = [low64, jnp.logical_not(low64)]

    def one_direction(d, c):
        r0 = pl.multiple_of(c * C, C)
        x = act_ref[pl.ds(r0, C), 0:SSD_PAD]
        bmat = act_ref[pl.ds(r0, C), SSD_PAD:SSD_PAD + LANES]
        cmat = act_ref[pl.ds(r0, C), SSD_PAD + LANES:SSD_PAD + 2 * LANES]
        dt = dt_ref[pl.ds(r0, C), :]
        dta = dt * a_row
        hi, lo = _split2(dta)
        cs = jnp.dot(tril_ref[...], jnp.concatenate([hi, lo], axis=0),
                     preferred_element_type=F32)
        q = cs if d == 0 else cs - dta
        expand = lambda v: jnp.dot(jnp.concatenate(_split2(v), axis=1), exp_ref[d],
                                   preferred_element_type=F32)
        dt_x = expand(dt)
        cs_x = expand(cs)
        end_x = cs_x[C - 1:C, :]
        if d == 0:
            off_scale = jnp.exp(cs_x)
            w_state = jnp.exp(end_x - cs_x)
        else:
            e_x = cs_x - dt_x * ax_ref[d]
            off_scale = jnp.exp(end_x - e_x)
            w_state = jnp.exp(e_x)
        chunk_decay = jnp.exp(end_x)
        xdt = x * dt_x
        xdt_b = xdt.astype(BF16)
        rhs_state = (xdt * w_state).astype(BF16)
        qcol = jnp.dot(jnp.concatenate(_split2(q), axis=1), colb_ref[d],
                       preferred_element_type=F32)
        q_t = q.T
        b_t = bmat.T.astype(BF16)
        b_b = bmat.astype(BF16)
        cms = [jnp.where(grp_mask[g], cmat, 0.0).astype(BF16) for g in range(SSD_NGROUPS)]
        gmats = lax.dot_general(jnp.concatenate(cms, axis=0), b_b, (((1,), (1,)), ((), ())),
                                preferred_element_type=F32)
        st_all = jnp.dot(b_t, rhs_state, preferred_element_type=F32)
        for g in range(SSD_NGROUPS):
            gmat = gmats[g * C:(g + 1) * C]
            s_old = s_ref[d, g]
            y_off = (jnp.dot(cms[g], s_old.astype(BF16), preferred_element_type=F32)
                     * off_scale[:, g * GW:(g + 1) * GW])
            ms = []
            for hh in range(3):
                h = 3 * g + hh
                slot = 4 * g + hh
                row = q_t[SLOTS * d + slot:SLOTS * d + slot + 1, :]
                col = qcol[:, h * LANES:(h + 1) * LANES]
                seg = (col - row) if d == 0 else (row - col)
                ms.append((gmat * jnp.exp(jnp.where(causal[d], seg, NEG_BIG))).astype(BF16))
            r01 = jnp.dot(jnp.concatenate(ms[:2], axis=0), xdt_b[:, 2 * g * LANES:(2 * g + 1) * LANES],
                          preferred_element_type=F32)
            r2 = jnp.dot(ms[2], xdt_b[:, (2 * g + 1) * LANES:(2 * g + 2) * LANES],
                         preferred_element_type=F32)
            y_g = jnp.concatenate([jnp.where(low64, r01[:C], r01[C:]), r2], axis=1) + y_off
            yacc_ref[pl.ds(r0, C), g * GW:(g + 1) * GW] += y_g
            s_ref[d, g] = (s_old * chunk_decay[:, g * GW:(g + 1) * GW]
                           + st_all[:, g * GW:(g + 1) * GW])

    def scan(i, carry):
        one_direction(0, i)
        one_direction(1, nc - 1 - i)
        return carry

    lax.fori_loop(0, nc, scan, 0, unroll=2)
    for d in range(2):
        out = sfin_refs[d].at[0, layer] if create_states else sfin_refs[d].at[0]
        if create_states:
            for l2 in range(sfin_refs[d].shape[1]):
                if l2 != layer:
                    sfin_refs[d][0, l2] = jnp.zeros(sfin_refs[d].shape[2:], F32)
        for g in range(SSD_NGROUPS):
            s_t = s_ref[d, g].T
            for hh in range(3):
                out[3 * g + hh] = s_t[hh * HEAD_DIM:(hh + 1) * HEAD_DIM, g * SSD_STATE:(g + 1) * SSD_STATE]

    def finish(c, carry):
        r0 = pl.multiple_of(c * C, C)
        y = yacc_ref[pl.ds(r0, C), :] + act_ref[pl.ds(r0, C), 0:SSD_PAD] * d_ref[...]
        y = y * _silu(u_ref[pl.ds(r0, C), 0:SSD_PAD])
        for g in range(SSD_NGROUPS):
            yg = y[:, g * GW:(g + 1) * GW]
            ms = jnp.sum(yg * yg, axis=-1, keepdims=True) * (1.0 / (SSD_INNER // SSD_NGROUPS))
            yn = yg * lax.rsqrt(ms + EPS) * nw_ref[:, g * GW:(g + 1) * GW]
            y_ref[pl.ds(r0, C), g * GW:(g + 1) * GW] = yn.astype(y_ref.dtype)
        return carry

    lax.fori_loop(0, nc, finish, 0)


def _ssd(u_ssd, row0, nb, L, prm, layer, use_rope, s0, states="layer"):
    blk0 = row0 // L
    tril2, exp2, colb2 = _ssd_constants()
    consts = [jnp.asarray(tril2, dtype=BF16), jnp.asarray(exp2, dtype=BF16), jnp.asarray(colb2, dtype=BF16)]
    full = lambda a: pl.BlockSpec(a.shape, lambda b, _n=a.ndim: (0,) * _n)
    of_layer = lambda a: pl.BlockSpec((None,) + a.shape[1:], lambda b, _n=a.ndim: (layer,) + (0,) * (_n - 1))
    per_layer = [prm[k] for k in ("conv_w", "conv_b", "dt_bias", "a_row", "a_x", "d_row", "norm_w")]
    args = [u_ssd] + per_layer + consts
    in_specs = ([pl.BlockSpec((L, U_SSD), lambda b: (blk0 + b, 0))] + [of_layer(a) for a in per_layer]
                + [full(a) for a in consts])
    if use_rope:
        cos, sin = _rope_tables(L)
        tabs = [jnp.asarray(cos), jnp.asarray(sin)]
        args += tabs
        in_specs += [full(a) for a in tabs]
    state_block = (SSD_HEADS, HEAD_DIM, SSD_STATE)
    if s0 is not None:
        args += list(s0)
        in_specs += [pl.BlockSpec((1, 1) + state_block, lambda b: (b, layer, 0, 0, 0))] * 2
    sshape = (2, SSD_NGROUPS, LANES, SSD_PAD // SSD_NGROUPS)
    aliases = {}
    if isinstance(states, str) and states == "layer":
        state_spec = pl.BlockSpec((1,) + state_block, lambda b: (b, 0, 0, 0))
        state_shape = (nb,) + state_block
    elif isinstance(states, str):
        state_spec = pl.BlockSpec((1, DEPTH) + state_block, lambda b: (b, 0, 0, 0, 0))
        state_shape = (nb, DEPTH) + state_block
    else:
        state_spec = pl.BlockSpec((1, None) + state_block, lambda b: (b, layer, 0, 0, 0))
        state_shape = (nb, DEPTH) + state_block
        aliases = {len(args): 1, len(args) + 1: 2}
        args += list(states)
        in_specs += [pl.BlockSpec(memory_space=pl.ANY)] * 2
    return pl.pallas_call(
        functools.partial(_ssd_kernel, L, use_rope, s0 is not None, layer,
                          isinstance(states, str) and states == "create"),
        grid=(nb,),
        in_specs=in_specs,
        out_specs=[pl.BlockSpec((L, SSD_PAD), lambda b: (b, 0)), state_spec, state_spec],
        out_shape=[jax.ShapeDtypeStruct((nb * L, SSD_PAD), BF16)]
                  + [jax.ShapeDtypeStruct(state_shape, F32)] * 2,
        input_output_aliases=aliases,
        scratch_shapes=[pltpu.VMEM((L, SSD_PAD + 2 * LANES), F32),
                        pltpu.VMEM((L, LANES), F32),
                        pltpu.VMEM(sshape, F32),
                        pltpu.VMEM((L, SSD_PAD), F32)],
        compiler_params=_cparams(("parallel",)),
        name="ssd",
    )(*args)


def _pad_heads(a, axis=-1):
    a = jnp.moveaxis(a, axis, -1)
    lead = a.shape[:-1]
    a = a.reshape(lead + (SSD_NGROUPS, 3, HEAD_DIM))
    a = jnp.pad(a, [(0, 0)] * len(lead) + [(0, 0), (0, 1), (0, 0)])
    return jnp.moveaxis(a.reshape(lead + (SSD_PAD,)), -1, axis)


def _pad_dt_lanes(a):
    lead = a.shape[:-2]
    a = a.reshape(lead + (2, SSD_NGROUPS, 3))
    a = jnp.pad(a, [(0, 0)] * len(lead) + [(0, 0), (0, 0), (0, 1)]).reshape(lead + (2 * SLOTS,))
    return jnp.pad(a, [(0, 0)] * len(lead) + [(0, LANES - 2 * SLOTS)])


W_IN_TC = 256


def _w_in_relayout_kernel(w_ref, pdt_ref, o_ref):
    half_group = SSD_INNER // SSD_NGROUPS
    gw = SSD_PAD // SSD_NGROUPS
    zero = jnp.zeros((gw - half_group, W_IN_TC), o_ref.dtype)

    def put(dst, v):
        o_ref[dst:dst + v.shape[0], :] = v.astype(o_ref.dtype)

    put(0, w_ref[0:FN_WIDTH, :])
    for seg in range(2):
        for g in range(SSD_NGROUPS):
            src = FN_WIDTH + seg * SSD_INNER + g * half_group
            dst = U_FN + seg * SSD_PAD + g * gw
            put(dst, w_ref[src:src + half_group, :])
            put(dst + half_group, zero)
    src = FN_WIDTH + 2 * SSD_INNER
    put(U_FN + 2 * SSD_PAD, w_ref[src:src + 2 * LANES, :])
    src += 2 * LANES
    put(U_FN + 2 * SSD_PAD + 2 * LANES,
        jnp.dot(pdt_ref[...], w_ref[src:src + LANES, :].astype(BF16), preferred_element_type=F32))
    src += 2 * SSD_HEADS
    put(U_FN + U_SSD, w_ref[src:src + NA_WIDTH, :] * (HEAD_DIM ** -0.5))
    put(U_FN + U_SSD + NA_WIDTH, w_ref[src + NA_WIDTH:src + 3 * NA_WIDTH, :])


def _w_in_relayout(w_in):
    pdt = np.zeros((LANES, LANES), np.float32)
    for d in range(2):
        for h in range(SSD_HEADS):
            pdt[SLOTS * d + _slot_of_head(h), SSD_HEADS * d + h] = 1.0
    w_t = jnp.transpose(w_in, (0, 2, 1))
    return pl.pallas_call(
        _w_in_relayout_kernel,
        grid=(DEPTH, D_MODEL // W_IN_TC),
        in_specs=[pl.BlockSpec((None, w_in.shape[-1], W_IN_TC), lambda l, i: (l, 0, i)),
                  pl.BlockSpec((LANES, LANES), lambda l, i: (0, 0))],
        out_specs=pl.BlockSpec((None, U_TOTAL, W_IN_TC), lambda l, i: (l, 0, i)),
        out_shape=jax.ShapeDtypeStruct((DEPTH, U_TOTAL, D_MODEL), BF16),
        compiler_params=_cparams(("parallel", "parallel")),
        name="w_in_relayout",
    )(w_t, jnp.asarray(pdt, dtype=BF16))


def _mixer_params(w_in, w_out, ssd_conv_w, ssd_conv_b, ssd_dt_bias, ssd_a_log, ssd_d, ssd_norm):
    w_in_pad = _w_in_relayout(w_in)
    a = -jnp.exp(ssd_a_log.astype(F32))
    a_x = jnp.repeat(_pad_dt_lanes(a)[:, :2 * SLOTS].reshape(DEPTH, 2, SLOTS), HEAD_DIM, axis=-1)
    ssd = {
        "conv_w": jnp.concatenate([_pad_heads(ssd_conv_w[..., :SSD_INNER]), ssd_conv_w[..., SSD_INNER:]], axis=-1),
        "conv_b": jnp.concatenate([_pad_heads(ssd_conv_b[..., :SSD_INNER]),
                                   ssd_conv_b[..., SSD_INNER:]], axis=-1)[:, None, :],
        "dt_bias": _pad_dt_lanes(ssd_dt_bias)[:, None, :],
        "a_row": _pad_dt_lanes(a)[:, None, :],
        "a_x": a_x.reshape(DEPTH, 2, 1, SSD_PAD),
        "d_row": _pad_heads(jnp.repeat(ssd_d, HEAD_DIM, axis=-1))[:, None, :],
        "norm_w": _pad_heads(ssd_norm)[:, None, :],
    }
    return {
        "w_in": w_in_pad,
        "w_out_fn": w_out[:, :FN_WIDTH].astype(BF16),
        "w_out_ssd": _pad_heads(w_out[:, FN_WIDTH:FN_WIDTH + SSD_INNER], axis=1).astype(BF16),
        "w_out_att": w_out[:, FN_WIDTH + SSD_INNER:].astype(BF16),
        "ssd": ssd,
    }


def _pick_tile(rows, want):
    t = min(rows, want)
    while rows % t:
        t //= 2
    return t


def kernel(x_prompt, x_sample, c, state_ssd_fwd, state_ssd_bwd, cache_attn_k, cache_attn_v, c_ctx, mod_w, mod_b, norm_pre, norm_post, ffn_w13, ffn_w2, w_in, w_out, ssd_conv_w, ssd_conv_b, ssd_dt_bias, ssd_a_log, ssd_d, ssd_norm, na_rpb):
    nbp, lp, _ = x_prompt.shape
    nbs, ls, _ = x_sample.shape
    n_ctx, n_lat = nbp * lp, nbs * ls
    assert n_ctx == n_lat and n_ctx % ls == 0 and nbs + 1 <= 8
    rpg = ls
    xs = (x_prompt.reshape(n_ctx, D_MODEL), x_sample.reshape(n_lat, D_MODEL))

    cvec = jnp.zeros((8, D_MODEL), F32).at[0].set(c_ctx).at[1:1 + nbs].set(c)
    mods = _mods(cvec, mod_w, mod_b).reshape(DEPTH, 8, N_MOD, D_MODEL)
    w13 = ffn_w13.astype(BF16)
    w2 = ffn_w2.astype(BF16)
    p = _mixer_params(w_in, w_out, ssd_conv_w, ssd_conv_b, ssd_dt_bias, ssd_a_log, ssd_d, ssd_norm)
    to_feature_major = lambda a: jnp.transpose(a, (0, 1, 3, 4, 2)).reshape(a.shape[0], DEPTH, NA_WIDTH, a.shape[2])
    from_feature_major = lambda a: jnp.transpose(
        a.reshape(a.shape[0], DEPTH, NA_HEADS, HEAD_DIM, a.shape[3]), (0, 1, 4, 2, 3))
    cache_k = to_feature_major(cache_attn_k)
    cache_v = to_feature_major(cache_attn_v)
    s0 = (state_ssd_fwd.astype(F32), state_ssd_bwd.astype(F32))
    na_bias = _na_bias_inputs(na_rpb, ls // GRID_W)
    tm_ffn = _pick_tile(rpg, 512)
    tm_proj = _pick_tile(rpg, 512)

    new_states = "create"
    new_kv = None
    for l in range(DEPTH):
        gm = jnp.concatenate([jnp.broadcast_to(mods[l, 0], (n_ctx // rpg, N_MOD, D_MODEL)),
                              mods[l, 1:1 + nbs]], axis=0)
        xs = _ffn(xs, gm, 0, norm_pre[l, 0], norm_post[l, 0], w13, w2, l, 0, rpg, tm_ffn)
        u_fn, u_ssd, u_qkv, *new_kv = _inproj(xs, gm[:, 3:6], norm_pre[l, 1], p["w_in"], l, rpg,
                                              tm_proj, lp, caches=new_kv)

        y_ssd_c, *new_states = _ssd(u_ssd, 0, nbp, lp, p["ssd"], l, False, None, states=new_states)
        mix_ctx = (_fourier(u_fn, 0, nbp, lp), y_ssd_c, _ctx_attn(u_qkv, nbp, lp))

        y_ssd_l, _, _ = _ssd(u_ssd, n_ctx, nbs, ls, p["ssd"], l, True, s0)
        mix_lat = (_fourier(u_fn, n_ctx, nbs, ls), y_ssd_l,
                   _na_attn(u_qkv, n_ctx, nbs, ls, cache_k, cache_v, l, na_bias))

        mix = (mix_ctx, mix_lat, (p["w_out_fn"], p["w_out_ssd"], p["w_out_att"]), norm_post[l, 1])
        xs = _ffn(xs, gm, 6, norm_pre[l, 2], norm_post[l, 2], w13, w2, l, 1, rpg, tm_ffn, mix=mix)

    return (xs[0].reshape(nbp, lp, D_MODEL), xs[1].reshape(nbs, ls, D_MODEL),
            new_states[0], new_states[1],
            from_feature_major(new_kv[0]), from_feature_major(new_kv[1]))
```
